```python
import jax
import jax.numpy as jnp
from jax import lax
import numpy as np

D_MODEL = 2048
BATCH = 2
SEQ = 8192
DEPTH = 4

GRID_W = 64
CTX_LEN = 256
HEAD_DIM = 64
CONV_W = 512
GQA_HEADS = 8
GQA_KV_HEADS = 2
SC_W = 512
NA_HEADS = 8
GQA_Q_W = GQA_HEADS * HEAD_DIM
GQA_KV_W = GQA_KV_HEADS * HEAD_DIM
NA_W = NA_HEADS * HEAD_DIM
MIX_W = CONV_W + GQA_Q_W + SC_W + NA_W
IN_SIZES = (CONV_W, CONV_W, GQA_Q_W, GQA_KV_W, GQA_KV_W, SC_W, SC_W, SC_W, NA_W, NA_W, NA_W)
IN_COLS = sum(IN_SIZES)
GQA_KV_LO = 2 * CONV_W + GQA_Q_W
GQA_KV_HI = GQA_KV_LO + 2 * GQA_KV_W
NA_KV_LO = GQA_KV_HI + 3 * SC_W + NA_W
NA_KV_HI = NA_KV_LO + 2 * NA_W
CONV_K = 31
SC_K = 3
Q_BLOCK = 128
WIN_ROWS = 8
WIN_COLS = 16
ROPE_THETA = 10000.0
ATTN_SCALE = HEAD_DIM ** -0.5
N_EXPERTS = 32
TOP_K = 4
D_EXPERT = 768
SWIGLU_LIMIT = 7.0
SWIGLU_ALPHA = 1.702
MOE_BLOCK = 128
N_MOD = 6
EPS = 1e-6

kernel_name = "hybrid_dit_headgroup_moe"


def rms_norm(x, g):
    xf = x.astype(jnp.float32)
    y = xf * lax.rsqrt(jnp.mean(xf * xf, axis=-1, keepdims=True) + EPS)
    return (y * g.astype(jnp.float32)).astype(x.dtype)


def layer_norm(x, g, b):
    xf = x.astype(jnp.float32)
    mu = jnp.mean(xf, axis=-1, keepdims=True)
    var = jnp.mean(jnp.square(xf - mu), axis=-1, keepdims=True)
    y = (xf - mu) * lax.rsqrt(var + EPS)
    return (y * g.astype(jnp.float32) + b.astype(jnp.float32)).astype(x.dtype)


def split_cols(p, sizes):
    offs = np.cumsum(sizes)[:-1].tolist()
    return jnp.split(p, offs, axis=-1)


def to_heads(t, n_heads):
    return t.reshape(t.shape[0], t.shape[1], n_heads, HEAD_DIM)


def adaln(cvec, w, b, n_chunks):
    return jax.nn.silu(cvec) @ w[:, :n_chunks * D_MODEL] + b[:n_chunks * D_MODEL]


def modulate(h, shift, scale):
    return h * (1 + scale) + shift


def depthwise_conv(u, w):
    k = w.shape[0]
    return lax.conv_general_dilated(
        u, w[:, None, :].astype(u.dtype), window_strides=(1,), padding=[(k // 2, k // 2)],
        dimension_numbers=("NWC", "WIO", "NWC"), feature_group_count=w.shape[1])


def axial_rope_tables(n_tokens):
    t = jnp.arange(n_tokens)
    axis_dim = HEAD_DIM // 2
    inv = ROPE_THETA ** (-jnp.arange(0, axis_dim, 2, dtype=jnp.float32) / axis_dim)
    pos = jnp.stack([t // GRID_W, t % GRID_W], axis=-1).astype(jnp.float32)
    ang = pos[:, :, None] * inv[None, None, :]
    return jnp.cos(ang), jnp.sin(ang)


def apply_axial_rope(x, cos, sin):
    xs = x.reshape(*x.shape[:-1], 2, 2, HEAD_DIM // 4)
    x1, x2 = xs[..., 0, :], xs[..., 1, :]
    c = cos[None, :, None]
    s = sin[None, :, None]
    out = jnp.stack([x1 * c - x2 * s, x1 * s + x2 * c], axis=-2)
    return out.reshape(x.shape).astype(x.dtype)


def attend(q5, k, v):
    s = jnp.einsum("bqkgd,bskd->bkgqs", q5, k).astype(jnp.float32) * ATTN_SCALE
    p = jax.nn.softmax(s, axis=-1).astype(v.dtype)
    return jnp.einsum("bkgqs,bskd->bqkgd", p, v)


def dense_attn(q, k, v):
    b_, l_, hq, dh = q.shape
    hkv = k.shape[2]
    o = attend(q.reshape(b_, l_, hkv, hq // hkv, dh), k, v)
    return o.reshape(b_, l_, hq * dh)


def gqa_blocked(q, k_all, v_all):
    b_, s_, _, dh = q.shape
    g = GQA_HEADS // GQA_KV_HEADS
    nb = s_ // Q_BLOCK
    qb = q.reshape(b_, nb, Q_BLOCK, GQA_KV_HEADS, g, dh).transpose(1, 0, 2, 3, 4, 5)
    o = lax.map(lambda qblk: attend(qblk, k_all, v_all), qb)
    return o.transpose(1, 0, 2, 3, 4, 5).reshape(b_, s_, GQA_HEADS * dh)


def neighbourhood_attn(q, k, v, kc, vc, rpb):
    b_, s_, h_, dh = q.shape
    rows = s_ // GRID_W
    wr = min(WIN_ROWS, rows)
    wc = WIN_COLS
    qg = q.reshape(b_, rows, GRID_W, h_, dh).transpose(1, 0, 2, 3, 4)
    kg = k.reshape(b_, rows, GRID_W, h_, dh)
    vg = v.reshape(b_, rows, GRID_W, h_, dh)
    cols = jnp.arange(GRID_W)
    col_idx = jnp.clip(cols - wc // 2, 0, GRID_W - wc)[:, None] + jnp.arange(wc)[None, :]
    col_rel = col_idx - cols[:, None] + (WIN_COLS - 1)
    rpb_c = rpb[:, :, col_rel]

    def row_block(args):
        q_row, r = args
        r0 = jnp.clip(r - wr // 2, 0, rows - wr)
        k_win = lax.dynamic_slice_in_dim(kg, r0, wr, axis=1)[:, :, col_idx]
        v_win = lax.dynamic_slice_in_dim(vg, r0, wr, axis=1)[:, :, col_idx]
        row_rel = r0 + jnp.arange(wr) - r + (WIN_ROWS - 1)
        bias = rpb_c[:, row_rel].transpose(0, 2, 1, 3).astype(jnp.float32)
        s_win = jnp.einsum("bqhd,brqchd->bhqrc", q_row, k_win).astype(jnp.float32) * ATTN_SCALE + bias[None]
        s_ctx = jnp.einsum("bqhd,blhd->bhql", q_row, kc).astype(jnp.float32) * ATTN_SCALE
        s = jnp.concatenate([s_win.reshape(b_, h_, GRID_W, wr * wc), s_ctx], axis=-1)
        p = jax.nn.softmax(s, axis=-1).astype(v.dtype)
        p_win = p[..., :wr * wc].reshape(b_, h_, GRID_W, wr, wc)
        p_ctx = p[..., wr * wc:]
        return (jnp.einsum("bhqrc,brqchd->bqhd", p_win, v_win)
                + jnp.einsum("bhql,blhd->bqhd", p_ctx, vc))

    o = lax.map(row_block, (qg, jnp.arange(rows)))
    return o.transpose(1, 0, 2, 3, 4).reshape(b_, s_, h_ * dh)


def conformer_conv(a_val, a_gate, dw_w, dw_b, ln_g, ln_b):
    u = a_val * jax.nn.sigmoid(a_gate)
    u = depthwise_conv(u, dw_w) + dw_b
    return jax.nn.silu(layer_norm(u, ln_g, ln_b))


def short_conv(h, gate_b, gate_c, w3):
    return gate_b * depthwise_conv(gate_c * h, w3)


def mixer(hx, hc, w_in, w_out, dw_w, dw_b, ln_g, ln_b, qn_g, kn_g, sc_w, rpb, cos, sin, ctx_out):
    a_v, a_g, gq, gk, gv, s_h, s_b, s_c, nq, nk, nv = split_cols(hx @ w_in, IN_SIZES)
    if ctx_out:
        ca_v, ca_g, cq, ck, cv, cs_h, cs_b, cs_c, cnq, cnk, cnv = split_cols(hc @ w_in, IN_SIZES)
    else:
        ck, cv = split_cols(hc @ w_in[:, GQA_KV_LO:GQA_KV_HI], (GQA_KV_W, GQA_KV_W))
        cnk, cnv = split_cols(hc @ w_in[:, NA_KV_LO:NA_KV_HI], (NA_W, NA_W))
    ck = rms_norm(to_heads(ck, GQA_KV_HEADS), kn_g)
    cv = to_heads(cv, GQA_KV_HEADS)
    cnk = to_heads(cnk, NA_HEADS)
    cnv = to_heads(cnv, NA_HEADS)

    out_a = conformer_conv(a_v, a_g, dw_w, dw_b, ln_g, ln_b)
    q = apply_axial_rope(rms_norm(to_heads(gq, GQA_HEADS), qn_g), cos, sin)
    k = apply_axial_rope(rms_norm(to_heads(gk, GQA_KV_HEADS), kn_g), cos, sin)
    k_all = jnp.concatenate([ck, k], axis=1)
    v_all = jnp.concatenate([cv, to_heads(gv, GQA_KV_HEADS)], axis=1)
    out_b = gqa_blocked(q, k_all, v_all)
    out_c = short_conv(s_h, s_b, s_c, sc_w)
    out_d = neighbourhood_attn(to_heads(nq, NA_HEADS), to_heads(nk, NA_HEADS), to_heads(nv, NA_HEADS), cnk, cnv, rpb)
    y_x = jnp.concatenate([out_a, out_b, out_c, out_d], axis=-1) @ w_out
    if not ctx_out:
        return y_x, None
    cout_a = conformer_conv(ca_v, ca_g, dw_w, dw_b, ln_g, ln_b)
    cout_b = dense_attn(rms_norm(to_heads(cq, GQA_HEADS), qn_g), ck, cv)
    cout_c = short_conv(cs_h, cs_b, cs_c, sc_w)
    cout_d = dense_attn(to_heads(cnq, NA_HEADS), cnk, cnv)
    y_c = jnp.concatenate([cout_a, cout_b, cout_c, cout_d], axis=-1) @ w_out
    return y_x, y_c


def clamped_swiglu(h):
    x_glu = jnp.minimum(h[..., ::2], SWIGLU_LIMIT)
    x_lin = jnp.clip(h[..., 1::2], -SWIGLU_LIMIT, SWIGLU_LIMIT)
    return x_glu * jax.nn.sigmoid(SWIGLU_ALPHA * x_glu) * (x_lin + 1)


def moe(t, w_router, b_router, w_gu, b_gu, w_down, b_down):
    n = t.shape[0]
    logits = t.astype(jnp.float32) @ w_router.astype(jnp.float32) + b_router.astype(jnp.float32)
    top_vals, top_idx = lax.top_k(logits, TOP_K)
    gates = jax.nn.softmax(top_vals, axis=-1)
    n_assign = n * TOP_K
    flat_e = top_idx.reshape(-1).astype(jnp.int32)
    order = jnp.argsort(flat_e)
    sorted_e = flat_e[order]
    counts = jnp.bincount(flat_e, length=N_EXPERTS)
    padded = (counts + MOE_BLOCK - 1) // MOE_BLOCK * MOE_BLOCK
    pad_end = jnp.cumsum(padded)
    pad_start = pad_end - padded
    start = jnp.cumsum(counts) - counts
    dest = pad_start[sorted_e] + jnp.arange(n_assign) - start[sorted_e]
    n_blk = -(-n_assign // MOE_BLOCK) + N_EXPERTS
    slot_tok = jnp.full((n_blk * MOE_BLOCK,), n, jnp.int32).at[dest].set((order // TOP_K).astype(jnp.int32))
    slot_gate = jnp.zeros((n_blk * MOE_BLOCK,), jnp.float32).at[dest].set(gates.reshape(-1)[order])
    blk_expert = jnp.minimum(jnp.searchsorted(pad_end, jnp.arange(n_blk) * MOE_BLOCK, side="right"), N_EXPERTS - 1)
    t_pad = jnp.concatenate([t, jnp.zeros((1, t.shape[1]), t.dtype)], axis=0)

    def expert_block(args):
        tok, e = args
        h = t_pad[tok] @ w_gu[e] + b_gu[e]
        return clamped_swiglu(h) @ w_down[e] + b_down[e]

    y_slots = lax.map(expert_block, (slot_tok.reshape(n_blk, MOE_BLOCK), blk_expert))
    y = jnp.zeros_like(t_pad).at[slot_tok].add(y_slots.reshape(-1, t.shape[1]) * slot_gate[:, None].astype(t.dtype))
    return y[:n]


def setup_inputs(seed: int = 0) -> dict:
    key = jax.random.key(seed)
    ks = jax.random.split(key, 25)
    f32 = jnp.float32
    L, D = DEPTH, D_MODEL

    def nrm(k, shape, scale):
        return jax.random.normal(k, shape, f32) * scale

    return {
        "x": nrm(ks[0], (BATCH, SEQ, D), 1.0),
        "c": nrm(ks[1], (BATCH, D), 1.0),
        "ctx": nrm(ks[2], (BATCH, CTX_LEN, D), 1.0),
        "c_ctx": nrm(ks[3], (D,), 1.0),
        "w_ada": nrm(ks[4], (L, D, N_MOD * D), 0.5 * D ** -0.5),
        "b_ada": nrm(ks[5], (L, N_MOD * D), 0.01),
        "norm1_g": 1.0 + nrm(ks[6], (L, D), 0.05),
        "norm2_g": 1.0 + nrm(ks[7], (L, D), 0.05),
        "w_in": nrm(ks[8], (L, D, IN_COLS), D ** -0.5),
        "w_out": nrm(ks[9], (L, MIX_W, D), MIX_W ** -0.5),
        "conv_dw_w": nrm(ks[10], (L, CONV_K, CONV_W), CONV_K ** -0.5),
        "conv_dw_b": nrm(ks[11], (L, CONV_W), 0.01),
        "conv_ln_g": 1.0 + nrm(ks[12], (L, CONV_W), 0.05),
        "conv_ln_b": nrm(ks[13], (L, CONV_W), 0.01),
        "q_norm_g": 1.0 + nrm(ks[14], (L, HEAD_DIM), 0.05),
        "k_norm_g": 1.0 + nrm(ks[15], (L, HEAD_DIM), 0.05),
        "sc_conv_w": nrm(ks[16], (L, SC_K, SC_W), SC_K ** -0.5),
        "na_rpb": nrm(ks[17], (L, NA_HEADS, 2 * WIN_ROWS - 1, 2 * WIN_COLS - 1), 0.1),
        "w_router": nrm(ks[18], (L, D, N_EXPERTS), D ** -0.5),
        "b_router": nrm(ks[19], (L, N_EXPERTS), 0.01),
        "w_gate_up": nrm(ks[20], (L, N_EXPERTS, D, 2 * D_EXPERT), D ** -0.5),
        "b_gate_up": nrm(ks[21], (L, N_EXPERTS, 2 * D_EXPERT), 0.01),
        "w_down": nrm(ks[22], (L, N_EXPERTS, D_EXPERT, D), D_EXPERT ** -0.5),
        "b_down": nrm(ks[23], (L, N_EXPERTS, D), 0.01),
        "final_norm_g": 1.0 + nrm(ks[24], (D,), 0.05),
    }


def reference(x, c, ctx, c_ctx, w_ada, b_ada, norm1_g, norm2_g, w_in, w_out, conv_dw_w, conv_dw_b,
              conv_ln_g, conv_ln_b, q_norm_g, k_norm_g, sc_conv_w, na_rpb, w_router, b_router,
              w_gate_up, b_gate_up, w_down, b_down, final_norm_g):
    cos, sin = axial_rope_tables(x.shape[1])
    h_ctx = ctx
    for l in range(DEPTH):
        last = l == DEPTH - 1
        mx = jnp.split(adaln(c, w_ada[l], b_ada[l], N_MOD)[:, None, :], N_MOD, axis=-1)
        n_cm = 2 if last else N_MOD
        mc = jnp.split(adaln(c_ctx, w_ada[l], b_ada[l], n_cm).reshape(1, 1, -1), n_cm, axis=-1)
        hx = modulate(rms_norm(x, norm1_g[l]), mx[0], mx[1])
        hc = modulate(rms_norm(h_ctx, norm1_g[l]), mc[0], mc[1])
        mix_x, mix_c = mixer(hx, hc, w_in[l], w_out[l], conv_dw_w[l], conv_dw_b[l], conv_ln_g[l],
                             conv_ln_b[l], q_norm_g[l], k_norm_g[l], sc_conv_w[l], na_rpb[l], cos, sin,
                             not last)
        x = x + mx[2] * mix_x
        hx2 = modulate(rms_norm(x, norm2_g[l]), mx[3], mx[4])
        if last:
            y = moe(hx2.reshape(-1, D_MODEL), w_router[l], b_router[l], w_gate_up[l], b_gate_up[l],
                    w_down[l], b_down[l])
            x = x + mx[5] * y.reshape(x.shape)
        else:
            h_ctx = h_ctx + mc[2] * mix_c
            hc2 = modulate(rms_norm(h_ctx, norm2_g[l]), mc[3], mc[4])
            n_c = hc2.shape[0] * hc2.shape[1]
            tokens = jnp.concatenate([hc2.reshape(-1, D_MODEL), hx2.reshape(-1, D_MODEL)], axis=0)
            y = moe(tokens, w_router[l], b_router[l], w_gate_up[l], b_gate_up[l], w_down[l], b_down[l])
            h_ctx = h_ctx + mc[5] * y[:n_c].reshape(h_ctx.shape)
            x = x + mx[5] * y[n_c:].reshape(x.shape)
    return rms_norm(x, final_norm_g)
```

```python
import functools

import jax
import jax.numpy as jnp
import numpy as np
from jax import lax
from jax.experimental import pallas as pl
from jax.experimental.pallas import tpu as pltpu

F32 = jnp.float32
BF16 = jnp.bfloat16
I32 = jnp.int32

D_MODEL = 2048
GRID_W = 64
HEAD_DIM = 64
CONV_W = 512
GQA_HEADS = 8
GQA_KV_HEADS = 2
SC_W = 512
NA_HEADS = 8
GQA_Q_W = GQA_HEADS * HEAD_DIM
GQA_KV_W = GQA_KV_HEADS * HEAD_DIM
NA_W = NA_HEADS * HEAD_DIM
MIX_W = CONV_W + GQA_Q_W + SC_W + NA_W
IN_COLS = 2 * CONV_W + GQA_Q_W + 2 * GQA_KV_W + 3 * SC_W + 3 * NA_W
CONV_K = 31
SC_K = 3
WIN_ROWS = 8
WIN_COLS = 16
ROPE_THETA = 10000.0
ATTN_SCALE = HEAD_DIM ** -0.5
TOP_K = 4
SWIGLU_LIMIT = 7.0
SWIGLU_ALPHA = 1.702
N_MOD = 6
EPS = 1e-6
MASK_VALUE = -1e30

C_AV, C_GQ = 0, 2 * CONV_W
C_GK = C_GQ + GQA_Q_W
C_GV = C_GK + GQA_KV_W
C_SH = C_GV + GQA_KV_W
C_NQ = C_SH + 3 * SC_W
C_NK = C_NQ + NA_W
C_NV = C_NK + NA_W

VMEM_LIMIT = 56 * 1024 * 1024
LANE = 128

TM_PROJ = 512
TM_CONV = 256
HALO = 16
TQ_GQA = 128
KC_GQA = 512
NA_RB = 8
TR_ROUTE = 256
TD_MOE = 256
BLK_MOE = 256


def _cparams(sem):
    return pltpu.CompilerParams(dimension_semantics=sem, vmem_limit_bytes=VMEM_LIMIT)


def _resident(shape, index_map):
    return pl.BlockSpec(shape, index_map, pipeline_mode=pl.Buffered(1))


def _split_bf16(a):
    hi = a.astype(BF16)
    lo = (a - hi.astype(F32)).astype(BF16)
    return hi, lo


def _ada_kernel(c_ref, w_ref, b_ref, o_ref):
    c = c_ref[...]
    s = c * jax.nn.sigmoid(c)
    hi, lo = _split_bf16(s)
    lhs = jnp.concatenate([hi, lo], axis=0)
    r = jnp.dot(lhs, w_ref[0].astype(BF16), preferred_element_type=F32)
    o_ref[0] = r[:8] + r[8:] + b_ref[0]


def _ada_all(cvec, w_ada, b_ada):
    n_l, d, n6 = w_ada.shape
    tn = 1024
    return pl.pallas_call(
        _ada_kernel,
        out_shape=jax.ShapeDtypeStruct((n_l, 8, n6), F32),
        grid=(n_l, n6 // tn),
        in_specs=[
            pl.BlockSpec((8, d), lambda l, j: (0, 0)),
            pl.BlockSpec((1, d, tn), lambda l, j: (l, 0, j)),
            pl.BlockSpec((1, 1, tn), lambda l, j: (l, 0, j)),
        ],
        out_specs=pl.BlockSpec((1, 8, tn), lambda l, j: (l, 0, j)),
        compiler_params=_cparams(("parallel", "parallel")),
        name="ada_mod",
    )(cvec, w_ada, b_ada.reshape(n_l, 1, n6))


def _rope(x, cos, sin_signed):
    n = x.shape[1]
    lane = lax.broadcasted_iota(I32, x.shape, 1)
    first = (lane % 32) < 16
    partner = jnp.where(first, pltpu.roll(x, n - 16, 1), pltpu.roll(x, 16, 1))
    return x * cos + partner * sin_signed


def _head_mean_sq(y, bd):
    hi, lo = _split_bf16(y * y)
    return (jnp.dot(hi, bd, preferred_element_type=F32)
            + jnp.dot(lo, bd, preferred_element_type=F32))


def _in_kernel(x_ref, mod_ref, g_ref, w_ref, cos_ref, sin_ref, qg_ref, kg_ref, bd_ref,
               pa_ref, q_ref, k_ref, v_ref, ps_ref, nq_ref, nk_ref, nv_ref):
    x = x_ref[...]
    ms = jnp.mean(x * x, axis=-1, keepdims=True)
    y = x * lax.rsqrt(ms + EPS) * g_ref[...]
    shift = mod_ref[0, 0:1, :]
    scale = mod_ref[0, 1:2, :]
    h = (y * (1 + scale) + shift).astype(BF16)

    def proj(lo, hi):
        return jnp.dot(h, w_ref[:, lo:hi], preferred_element_type=F32)

    pa_ref[...] = proj(C_AV, C_GQ).astype(BF16)

    cos = cos_ref[...]
    sin = sin_ref[...]
    gq = proj(C_GQ, C_GK)
    qn = gq * lax.rsqrt(_head_mean_sq(gq, bd_ref[...]) + EPS) * qg_ref[...]
    cos4 = jnp.concatenate([cos] * 4, axis=1)
    sin4 = jnp.concatenate([sin] * 4, axis=1)
    q_ref[...] = (_rope(qn, cos4, sin4) * ATTN_SCALE).astype(BF16)

    gk = proj(C_GK, C_GV)
    kn = gk * lax.rsqrt(_head_mean_sq(gk, bd_ref[0:GQA_KV_W, 0:GQA_KV_W]) + EPS) * kg_ref[...]
    k_ref[...] = _rope(kn, cos, sin).astype(BF16)
    v_ref[...] = proj(C_GV, C_SH).astype(BF16)
    ps_ref[...] = proj(C_SH, C_NQ).astype(BF16)
    nq_ref[...] = (proj(C_NQ, C_NK) * ATTN_SCALE).astype(BF16)
    nk_ref[...] = proj(C_NK, C_NV).astype(BF16)
    nv_ref[...] = proj(C_NV, IN_COLS).astype(BF16)


def _in_proj(xs, mod, g1, w_in, cos_t, sin_t, qg, kg, bd, seg_of_tile):
    nt, d = xs.shape
    tm = TM_PROJ
    row = lambda i: (i, 0)
    const = lambda i: (0, 0)
    widths = (2 * CONV_W, GQA_Q_W, GQA_KV_W, GQA_KV_W, 3 * SC_W, NA_W, NA_W, NA_W)
    return pl.pallas_call(
        _in_kernel,
        out_shape=[jax.ShapeDtypeStruct((nt, w), BF16) for w in widths],
        grid=(nt // tm,),
        in_specs=[
            pl.BlockSpec((tm, d), row),
            pl.BlockSpec((1, N_MOD, d), lambda i: (seg_of_tile(i), 0, 0)),
            pl.BlockSpec((1, d), const),
            _resident((d, IN_COLS), const),
            pl.BlockSpec((tm, LANE), row),
            pl.BlockSpec((tm, LANE), row),
            pl.BlockSpec((1, GQA_Q_W), const),
            pl.BlockSpec((1, GQA_KV_W), const),
            pl.BlockSpec((GQA_Q_W, GQA_Q_W), const),
        ],
        out_specs=[pl.BlockSpec((tm, w), row) for w in widths],
        compiler_params=_cparams(("parallel",)),
        name="in_proj",
    )(xs, mod, g1, w_in, cos_t, sin_t, qg, kg, bd)


def _conv_kernel(seq, n_x_rows, ctx_len,
                 pa_c, pa_p, pa_n, ps_c, ps_p, ps_n, dww_ref, dwb_ref, lng_ref, lnb_ref, scw_ref,
                 oa_ref, oc_ref, ext_ref, ext2_ref, acc_ref):
    tc = TM_CONV
    row0 = pl.program_id(0) * tc
    in_x = row0 < n_x_rows
    rel = jnp.where(in_x, row0, row0 - n_x_rows)
    seg = jnp.where(in_x, seq, ctx_len)
    keep_p = jnp.where(rel % seg == 0, 0.0, 1.0).astype(F32)
    keep_n = jnp.where((rel + tc) % seg == 0, 0.0, 1.0).astype(F32)

    def glu(ref):
        a = ref[...].astype(F32)
        return a[:, :CONV_W] * jax.nn.sigmoid(a[:, CONV_W:])

    ext_ref[0:HALO, :] = glu(pa_p) * keep_p
    ext_ref[HALO:HALO + tc, :] = glu(pa_c)
    ext_ref[HALO + tc:, :] = glu(pa_n) * keep_n

    pad = HALO - CONV_K // 2
    rc = 64
    for c in range(CONV_W // LANE):
        cs = slice(c * LANE, (c + 1) * LANE)
        for r in range(tc // rc):
            acc = jnp.zeros((rc, LANE), F32)
            for k in range(CONV_K):
                lo = r * rc + k + pad
                acc = acc + dww_ref[k:k + 1, cs] * ext_ref[lo:lo + rc, cs]
            acc_ref[r * rc:(r + 1) * rc, cs] = acc + dwb_ref[:, cs]

    u = acc_ref[...]
    mu = jnp.mean(u, axis=-1, keepdims=True)
    var = jnp.mean(jnp.square(u - mu), axis=-1, keepdims=True)
    yn = (u - mu) * lax.rsqrt(var + EPS) * lng_ref[...] + lnb_ref[...]
    oa_ref[...] = (yn * jax.nn.sigmoid(yn)).astype(BF16)

    def gated(ref, lo, hi):
        a = ref[lo:hi, :].astype(F32)
        return a[:, 2 * SC_W:] * a[:, :SC_W]

    ext2_ref[0:8, :] = gated(ps_p, HALO - 8, HALO) * keep_p
    ext2_ref[8:8 + tc, :] = gated(ps_c, 0, tc)
    ext2_ref[8 + tc:, :] = gated(ps_n, 0, 8) * keep_n
    conv = (scw_ref[0:1, :] * ext2_ref[7:7 + tc, :]
            + scw_ref[1:2, :] * ext2_ref[8:8 + tc, :]
            + scw_ref[2:3, :] * ext2_ref[9:9 + tc, :])
    oc_ref[...] = (ps_c[:, SC_W:2 * SC_W].astype(F32) * conv).astype(BF16)


def _conv_mix(pa, ps, dww, dwb, lng, lnb, scw, seq, n_x_rows, ctx_len):
    nt = pa.shape[0]
    tc = TM_CONV
    hb = tc // HALO
    n_hb = nt // HALO
    row = lambda i: (i, 0)
    prev = lambda i: (jnp.maximum(i * hb - 1, 0), 0)
    nxt = lambda i: (jnp.minimum((i + 1) * hb, n_hb - 1), 0)
    const = lambda i: (0, 0)
    wa, ws = pa.shape[1], ps.shape[1]
    return pl.pallas_call(
        functools.partial(_conv_kernel, seq, n_x_rows, ctx_len),
        out_shape=[jax.ShapeDtypeStruct((nt, CONV_W), BF16), jax.ShapeDtypeStruct((nt, SC_W), BF16)],
        grid=(nt // tc,),
        in_specs=[
            pl.BlockSpec((tc, wa), row), pl.BlockSpec((HALO, wa), prev), pl.BlockSpec((HALO, wa), nxt),
            pl.BlockSpec((tc, ws), row), pl.BlockSpec((HALO, ws), prev), pl.BlockSpec((HALO, ws), nxt),
            pl.BlockSpec((CONV_K, CONV_W), const), pl.BlockSpec((1, CONV_W), const),
            pl.BlockSpec((1, CONV_W), const), pl.BlockSpec((1, CONV_W), const),
            pl.BlockSpec((SC_K, SC_W), const),
        ],
        out_specs=[pl.BlockSpec((tc, CONV_W), row), pl.BlockSpec((tc, SC_W), row)],
        scratch_shapes=[pltpu.VMEM((tc + 2 * HALO, CONV_W), F32),
                        pltpu.VMEM((tc + 16, SC_W), F32),
                        pltpu.VMEM((tc, CONV_W), F32)],
        compiler_params=_cparams(("parallel",)),
        name="conv_mix",
    )(pa, pa, pa, ps, ps, ps, dww, dwb, lng, lnb, scw)


def _qk(q, k):
    return lax.dot_general(q, k, (((1,), (1,)), ((), ())), preferred_element_type=F32)


def _softmax_pv(s, v):
    m = jnp.max(s, axis=-1, keepdims=True)
    p = jnp.exp(s - m)
    l = jnp.sum(p, axis=-1, keepdims=True)
    return jnp.dot(p.astype(BF16), v, preferred_element_type=F32) / l


def _ctx_attn_kernel(n_kv, q_ref, k_ref, v_ref, o_in_ref, o_ref):
    del o_in_ref
    n_h = q_ref.shape[1] // HEAD_DIM
    outs = []
    for h in range(n_h):
        g = h // (n_h // n_kv)
        q = q_ref[:, h * HEAD_DIM:(h + 1) * HEAD_DIM]
        k = k_ref[:, g * HEAD_DIM:(g + 1) * HEAD_DIM]
        v = v_ref[:, g * HEAD_DIM:(g + 1) * HEAD_DIM]
        outs.append(_softmax_pv(_qk(q, k), v))
    o_ref[...] = jnp.concatenate(outs, axis=1).astype(BF16)


def _ctx_attn(q, k, v, o_all, n_b, ctx_len, blk0):
    kw = k.shape[1]
    n_kv = kw // HEAD_DIM
    qw = q.shape[1]
    rows = lambda b: (blk0 + b, 0)
    return pl.pallas_call(
        functools.partial(_ctx_attn_kernel, n_kv),
        out_shape=jax.ShapeDtypeStruct(o_all.shape, o_all.dtype),
        grid=(n_b,),
        in_specs=[pl.BlockSpec((ctx_len, qw), rows), pl.BlockSpec((ctx_len, kw), rows),
                  pl.BlockSpec((ctx_len, kw), rows), pl.BlockSpec(memory_space=pl.ANY)],
        out_specs=pl.BlockSpec((ctx_len, qw), rows),
        input_output_aliases={3: 0},
        compiler_params=_cparams(("parallel",)),
        name="ctx_attn",
    )(q, k, v, o_all)


def _gqa_kernel(n_chunks, q_ref, kx_ref, vx_ref, kc_ref, vc_ref, o_ref):
    tq = q_ref.shape[0]
    grp = GQA_HEADS // GQA_KV_HEADS
    pieces = []
    for g in range(GQA_KV_HEADS):
        gs = slice(g * HEAD_DIM, (g + 1) * HEAD_DIM)
        qg = jnp.concatenate(
            [q_ref[:, (g * grp + j) * HEAD_DIM:(g * grp + j + 1) * HEAD_DIM] for j in range(grp)], axis=0)
        s = _qk(qg, kc_ref[:, gs])
        m = jnp.max(s, axis=-1, keepdims=True)
        p = jnp.exp(s - m)
        l = jnp.sum(p, axis=-1, keepdims=True)
        acc = jnp.dot(p.astype(BF16), vc_ref[:, gs], preferred_element_type=F32)

        def body(c, carry, qg=qg, gs=gs):
            m, l, acc = carry
            rows = pl.ds(pl.multiple_of(c * KC_GQA, KC_GQA), KC_GQA)
            s = _qk(qg, kx_ref[rows, gs])
            m_new = jnp.maximum(m, jnp.max(s, axis=-1, keepdims=True))
            alpha = jnp.exp(m - m_new)
            p = jnp.exp(s - m_new)
            l = alpha * l + jnp.sum(p, axis=-1, keepdims=True)
            acc = alpha * acc + jnp.dot(p.astype(BF16), vx_ref[rows, gs], preferred_element_type=F32)
            return m_new, l, acc

        m, l, acc = lax.fori_loop(0, n_chunks, body, (m, l, acc))
        o = acc / l
        pieces.extend(o[j * tq:(j + 1) * tq] for j in range(grp))
    o_ref[...] = jnp.concatenate(pieces, axis=1).astype(BF16)


def _gqa(q, k, v, n_b, seq, ctx_len):
    nt = q.shape[0]
    tq = TQ_GQA
    nq = seq // tq
    cblk0 = n_b * seq // ctx_len
    return pl.pallas_call(
        functools.partial(_gqa_kernel, seq // KC_GQA),
        out_shape=jax.ShapeDtypeStruct((nt, GQA_Q_W), BF16),
        grid=(n_b, nq),
        in_specs=[
            pl.BlockSpec((tq, GQA_Q_W), lambda b, j: (b * nq + j, 0)),
            pl.BlockSpec((seq, GQA_KV_W), lambda b, j: (b, 0)),
            pl.BlockSpec((seq, GQA_KV_W), lambda b, j: (b, 0)),
            pl.BlockSpec((ctx_len, GQA_KV_W), lambda b, j: (cblk0 + b, 0)),
            pl.BlockSpec((ctx_len, GQA_KV_W), lambda b, j: (cblk0 + b, 0)),
        ],
        out_specs=pl.BlockSpec((tq, GQA_Q_W), lambda b, j: (b * nq + j, 0)),
        compiler_params=_cparams(("parallel", "parallel")),
        name="gqa_flash",
    )(q, k, v, k, v)


def _na_kernel(n_rows, q_ref, k_ref, v_ref, kc_ref, vc_ref, t_ref, o_ref):
    rb = pl.program_id(2)
    win = WIN_ROWS * GRID_W
    heads = []
    for j in range(LANE // HEAD_DIM):
        hs = slice(j * HEAD_DIM, (j + 1) * HEAD_DIM)
        kc = kc_ref[:, hs]
        vc = vc_ref[:, hs]
        rows_out = []
        for i in range(NA_RB):
            r = rb * NA_RB + i
            r0 = jnp.clip(r - WIN_ROWS // 2, 0, n_rows - WIN_ROWS)
            shift = r0 - r + (WIN_ROWS - 1)
            krows = pl.ds(pl.multiple_of(r0 * GRID_W, GRID_W), win)
            q = q_ref[i * GRID_W:(i + 1) * GRID_W, hs]
            s_win = _qk(q, k_ref[krows, hs])
            bias = jnp.concatenate([t_ref[j, 2 * wp + shift] for wp in range(WIN_ROWS // 2)], axis=1)
            s_win = s_win + bias
            s_ctx = _qk(q, kc)
            m = jnp.maximum(jnp.max(s_win, axis=-1, keepdims=True), jnp.max(s_ctx, axis=-1, keepdims=True))
            p_win = jnp.exp(s_win - m)
            p_ctx = jnp.exp(s_ctx - m)
            l = jnp.sum(p_win, axis=-1, keepdims=True) + jnp.sum(p_ctx, axis=-1, keepdims=True)
            o = (jnp.dot(p_win.astype(BF16), v_ref[krows, hs], preferred_element_type=F32)
                 + jnp.dot(p_ctx.astype(BF16), vc, preferred_element_type=F32))
            rows_out.append(o / l)
        heads.append(jnp.concatenate(rows_out, axis=0))
    o_ref[...] = jnp.concatenate(heads, axis=1).astype(BF16)


def _na(nq, nk, nv, table, n_b, seq, ctx_len):
    nt = nq.shape[0]
    n_rows = seq // GRID_W
    n_rb = n_rows // NA_RB
    tq = NA_RB * GRID_W
    cblk0 = n_b * seq // ctx_len
    hp = LANE // HEAD_DIM
    return pl.pallas_call(
        functools.partial(_na_kernel, n_rows),
        out_shape=jax.ShapeDtypeStruct((nt, NA_W), BF16),
        grid=(n_b, NA_W // LANE, n_rb),
        in_specs=[
            pl.BlockSpec((tq, LANE), lambda b, h, r: (b * n_rb + r, h)),
            pl.BlockSpec((seq, LANE), lambda b, h, r: (b, h)),
            pl.BlockSpec((seq, LANE), lambda b, h, r: (b, h)),
            pl.BlockSpec((ctx_len, LANE), lambda b, h, r: (cblk0 + b, h)),
            pl.BlockSpec((ctx_len, LANE), lambda b, h, r: (cblk0 + b, h)),
            pl.BlockSpec((hp, 2 * WIN_ROWS - 2, GRID_W, LANE), lambda b, h, r: (h, 0, 0, 0)),
        ],
        out_specs=pl.BlockSpec((tq, LANE), lambda b, h, r: (b * n_rb + r, h)),
        compiler_params=_cparams(("parallel", "parallel", "parallel")),
        name="na_attn",
    )(nq, nk, nv, nk, nv, table)


def _na_bias_table(rpb):
    cols = np.arange(GRID_W)
    c0 = np.clip(cols - WIN_COLS // 2, 0, GRID_W - WIN_COLS)
    kc = np.arange(GRID_W)
    rel = kc[None, :] - cols[:, None] + (WIN_COLS - 1)
    valid = (kc[None, :] >= c0[:, None]) & (kc[None, :] < c0[:, None] + WIN_COLS)
    rel = np.clip(rel, 0, 2 * WIN_COLS - 2)
    t = rpb[:, :, rel]
    t = jnp.where(jnp.asarray(valid)[None, None], t, MASK_VALUE).astype(F32)
    return jnp.concatenate([t[:, :-1], t[:, 1:]], axis=-1)


def _out_kernel(oa_ref, ob_ref, oc_ref, od_ref, x_ref, mod_ref, g_ref, w_ref, wr_cat_ref, wr_hi_ref, br_ref,
                x1_ref, h2_ref, lg_ref):
    acc = jnp.dot(oa_ref[...], w_ref[0:CONV_W, :], preferred_element_type=F32)
    acc = acc + jnp.dot(ob_ref[...], w_ref[CONV_W:CONV_W + GQA_Q_W, :], preferred_element_type=F32)
    acc = acc + jnp.dot(oc_ref[...], w_ref[CONV_W + GQA_Q_W:MIX_W - NA_W, :], preferred_element_type=F32)
    acc = acc + jnp.dot(od_ref[...], w_ref[MIX_W - NA_W:MIX_W, :], preferred_element_type=F32)
    x1 = x_ref[...] + mod_ref[0, 2:3, :] * acc
    x1_ref[...] = x1
    ms = jnp.mean(x1 * x1, axis=-1, keepdims=True)
    y = x1 * lax.rsqrt(ms + EPS) * g_ref[...]
    h2 = y * (1 + mod_ref[0, 4:5, :]) + mod_ref[0, 3:4, :]
    h2_ref[...] = h2
    hi, lo = _split_bf16(h2)
    n_e = lg_ref.shape[1]
    a = jnp.dot(hi, wr_cat_ref[...], preferred_element_type=F32)
    b = jnp.dot(lo, wr_hi_ref[...], preferred_element_type=F32)
    lg_ref[...] = a[:, :n_e] + a[:, n_e:] + b + br_ref[...]


def _out_proj(oa, ob, oc, od, xs, mod, g2, w_out, wr_cat, wr_hi, br, seg_of_tile):
    nt, d = xs.shape
    tm = TM_PROJ
    n_e = wr_hi.shape[1]
    row = lambda i: (i, 0)
    const = lambda i: (0, 0)
    return pl.pallas_call(
        _out_kernel,
        out_shape=[jax.ShapeDtypeStruct((nt, d), F32), jax.ShapeDtypeStruct((nt, d), F32),
                   jax.ShapeDtypeStruct((nt, n_e), F32)],
        grid=(nt // tm,),
        in_specs=[
            pl.BlockSpec((tm, CONV_W), row), pl.BlockSpec((tm, GQA_Q_W), row),
            pl.BlockSpec((tm, SC_W), row), pl.BlockSpec((tm, NA_W), row),
            pl.BlockSpec((tm, d), row),
            pl.BlockSpec((1, N_MOD, d), lambda i: (seg_of_tile(i), 0, 0)),
            pl.BlockSpec((1, d), const),
            _resident((MIX_W, d), const),
            pl.BlockSpec((d, 2 * n_e), const), pl.BlockSpec((d, n_e), const), pl.BlockSpec((1, n_e), const),
        ],
        out_specs=[pl.BlockSpec((tm, d), row), pl.BlockSpec((tm, d), row), pl.BlockSpec((tm, n_e), row)],
        compiler_params=_cparams(("parallel",)),
        name="out_proj",
    )(oa, ob, oc, od, xs, mod, g2, w_out, wr_cat, wr_hi, br)


def _route_kernel(lg_ref, e_ref, pos_ref, gate_ref, cnt_ref, carry_ref):
    i = pl.program_id(0)

    @pl.when(i == 0)
    def _():
        carry_ref[...] = jnp.zeros_like(carry_ref)

    lg = lg_ref[...]
    tr, n_e = lg.shape
    lane = lax.broadcasted_iota(I32, (tr, n_e), 1)
    work = lg
    vals, idxs = [], []
    for _ in range(TOP_K):
        m = jnp.max(work, axis=-1, keepdims=True)
        idx = jnp.min(jnp.where(work == m, lane, n_e), axis=-1, keepdims=True)
        vals.append(m)
        idxs.append(idx)
        work = jnp.where(lane == idx, -jnp.inf, work)
    exps = [jnp.exp(v - vals[0]) for v in vals]
    den = exps[0] + exps[1] + exps[2] + exps[3]

    mask = jnp.zeros((tr, n_e), F32)
    for idx in idxs:
        mask = mask + jnp.where(lane == idx, 1.0, 0.0)
    r_i = lax.broadcasted_iota(I32, (tr, tr), 0)
    c_i = lax.broadcasted_iota(I32, (tr, tr), 1)
    tri = jnp.where(c_i <= r_i, 1.0, 0.0).astype(BF16)
    incl = jnp.dot(tri, mask.astype(BF16), preferred_element_type=F32)
    before = carry_ref[...] + incl - mask

    lane_o = lax.broadcasted_iota(I32, (tr, LANE), 1)
    e_out = jnp.zeros((tr, LANE), I32)
    pos_out = jnp.zeros((tr, LANE), I32)
    gate_out = jnp.zeros((tr, LANE), F32)
    for k in range(TOP_K):
        pos_k = jnp.sum(jnp.where(lane == idxs[k], before, 0.0), axis=-1, keepdims=True)
        e_out = jnp.where(lane_o == k, idxs[k], e_out)
        pos_out = jnp.where(lane_o == k, pos_k.astype(I32), pos_out)
        gate_out = jnp.where(lane_o == k, exps[k] / den, gate_out)
    e_ref[...] = e_out
    pos_ref[...] = pos_out
    gate_ref[...] = gate_out
    total = carry_ref[...] + incl[tr - 1:tr, :]
    carry_ref[...] = total
    cnt_ref[...] = total.astype(I32)


def _route(logits):
    nt, n_e = logits.shape
    tr = TR_ROUTE
    row = lambda i: (i, 0)
    return pl.pallas_call(
        _route_kernel,
        out_shape=[jax.ShapeDtypeStruct((nt, LANE), I32), jax.ShapeDtypeStruct((nt, LANE), I32),
                   jax.ShapeDtypeStruct((nt, LANE), F32), jax.ShapeDtypeStruct((1, n_e), I32)],
        grid=(nt // tr,),
        in_specs=[pl.BlockSpec((tr, n_e), row)],
        out_specs=[pl.BlockSpec((tr, LANE), row), pl.BlockSpec((tr, LANE), row),
                   pl.BlockSpec((tr, LANE), row), pl.BlockSpec((1, n_e), lambda i: (0, 0))],
        scratch_shapes=[pltpu.VMEM((1, n_e), F32)],
        compiler_params=_cparams(("arbitrary",)),
        name="route",
    )(logits)


def _row_copy(src_ref, src_row, dst_ref, dst_row, sem):
    return pltpu.make_async_copy(src_ref.at[pl.ds(src_row, 1)], dst_ref.at[pl.ds(dst_row, 1)], sem)


def _dispatch_kernel(dest_ref, h_ref, xs_in_ref, xs_ref, sem):
    del xs_in_ref
    td = h_ref.shape[0]
    base = pl.program_id(0) * td * TOP_K

    def issue(t, _):
        for k in range(TOP_K):
            _row_copy(h_ref, t, xs_ref, dest_ref[base + t * TOP_K + k], sem).start()
        return 0

    lax.fori_loop(0, td, issue, 0)

    def drain(t, _):
        for k in range(TOP_K):
            _row_copy(h_ref, t, xs_ref, dest_ref[base + t * TOP_K + k], sem).wait()
        return 0

    lax.fori_loop(0, td, drain, 0)


def _dispatch(dest, h2, xs_init):
    nt, d = h2.shape
    td = TD_MOE
    return pl.pallas_call(
        _dispatch_kernel,
        out_shape=jax.ShapeDtypeStruct(xs_init.shape, xs_init.dtype),
        grid_spec=pltpu.PrefetchScalarGridSpec(
            num_scalar_prefetch=1,
            grid=(nt // td,),
            in_specs=[pl.BlockSpec((td, d), lambda i, dest: (i, 0)), pl.BlockSpec(memory_space=pl.ANY)],
            out_specs=pl.BlockSpec(memory_space=pl.ANY),
            scratch_shapes=[pltpu.SemaphoreType.DMA],
        ),
        input_output_aliases={2: 0},
        compiler_params=_cparams(("arbitrary",)),
        name="moe_dispatch",
    )(dest, h2, xs_init)


def _ffn_kernel(blk_e_ref, n_act_ref, x_ref, wgu_ref, bgu_ref, wd_ref, bd_ref, y_ref):
    del blk_e_ref

    @pl.when(pl.program_id(0) < n_act_ref[0])
    def _():
        d_e = wd_ref.shape[1]
        xb = x_ref[...].astype(BF16)
        h = jnp.dot(xb, wgu_ref[0], preferred_element_type=F32) + bgu_ref[0]
        glu = jnp.minimum(h[:, :d_e], SWIGLU_LIMIT)
        lin = jnp.clip(h[:, d_e:], -SWIGLU_LIMIT, SWIGLU_LIMIT)
        act = glu * jax.nn.sigmoid(SWIGLU_ALPHA * glu) * (lin + 1)
        y_ref[...] = jnp.dot(act.astype(BF16), wd_ref[0], preferred_element_type=F32) + bd_ref[0]


def _ffn(blk_e, n_act, xs, wgu, bgu, wd, bd):
    ns, d = xs.shape
    blk = BLK_MOE
    n_e, _, d_e2 = wgu.shape
    d_e = d_e2 // 2

    def rows(b, be, na):
        return (jnp.minimum(b, na[0] - 1), 0)

    def expert3(b, be, na):
        return (be[jnp.minimum(b, na[0] - 1)], 0, 0)

    return pl.pallas_call(
        _ffn_kernel,
        out_shape=jax.ShapeDtypeStruct((ns, d), F32),
        grid_spec=pltpu.PrefetchScalarGridSpec(
            num_scalar_prefetch=2,
            grid=(ns // blk,),
            in_specs=[
                pl.BlockSpec((blk, d), rows),
                pl.BlockSpec((1, d, d_e2), expert3), pl.BlockSpec((1, 1, d_e2), expert3),
                pl.BlockSpec((1, d_e, d), expert3), pl.BlockSpec((1, 1, d), expert3),
            ],
            out_specs=pl.BlockSpec((blk, d), rows),
        ),
        compiler_params=_cparams(("arbitrary",)),
        name="moe_ffn",
    )(blk_e, n_act, xs, wgu, bgu.reshape(n_e, 1, d_e2), wd, bd.reshape(n_e, 1, d))


def _combine_kernel(dest_ref, ys_ref, gate_ref, x1_ref, mod_ref, o_ref, buf_ref, sem):
    td = x1_ref.shape[0]
    base = pl.program_id(0) * td * TOP_K

    def issue(t, _):
        for k in range(TOP_K):
            _row_copy(ys_ref, dest_ref[base + t * TOP_K + k], buf_ref.at[k], t, sem).start()
        return 0

    lax.fori_loop(0, td, issue, 0)

    def drain(t, _):
        for k in range(TOP_K):
            _row_copy(ys_ref, dest_ref[base + t * TOP_K + k], buf_ref.at[k], t, sem).wait()
        return 0

    lax.fori_loop(0, td, drain, 0)

    g = gate_ref[...]
    y = g[:, 0:1] * buf_ref[0]
    for k in range(1, TOP_K):
        y = y + g[:, k:k + 1] * buf_ref[k]
    o_ref[...] = x1_ref[...] + mod_ref[0, 5:6, :] * y


def _combine(dest, ys, gate, x1, mod, seg_of_tile):
    nt, d = x1.shape
    td = TD_MOE
    ratio = TM_PROJ // td
    return pl.pallas_call(
        _combine_kernel,
        out_shape=jax.ShapeDtypeStruct((nt, d), F32),
        grid_spec=pltpu.PrefetchScalarGridSpec(
            num_scalar_prefetch=1,
            grid=(nt // td,),
            in_specs=[
                pl.BlockSpec(memory_space=pl.ANY),
                pl.BlockSpec((td, LANE), lambda i, dest: (i, 0)),
                pl.BlockSpec((td, d), lambda i, dest: (i, 0)),
                pl.BlockSpec((1, N_MOD, d), lambda i, dest: (seg_of_tile(i // ratio), 0, 0)),
            ],
            out_specs=pl.BlockSpec((td, d), lambda i, dest: (i, 0)),
            scratch_shapes=[pltpu.VMEM((TOP_K, td, d), F32), pltpu.SemaphoreType.DMA],
        ),
        compiler_params=_cparams(("arbitrary",)),
        name="moe_combine",
    )(dest, ys, gate, x1, mod)


def _final_kernel(x_ref, g_ref, o_ref):
    x = x_ref[...]
    ms = jnp.mean(x * x, axis=-1, keepdims=True)
    o_ref[...] = x * lax.rsqrt(ms + EPS) * g_ref[...]


def _final_norm(xs, g, n_rows):
    d = xs.shape[1]
    tm = TM_PROJ
    return pl.pallas_call(
        _final_kernel,
        out_shape=jax.ShapeDtypeStruct((n_rows, d), F32),
        grid=(n_rows // tm,),
        in_specs=[pl.BlockSpec((tm, d), lambda i: (i, 0)), pl.BlockSpec((1, d), lambda i: (0, 0))],
        out_specs=pl.BlockSpec((tm, d), lambda i: (i, 0)),
        compiler_params=_cparams(("parallel",)),
        name="final_norm",
    )(xs, g)


def _rope_tables(n_b, seq, ctx_len):
    t = np.arange(seq)
    axis_dim = HEAD_DIM // 2
    inv = (ROPE_THETA ** (-np.arange(0, axis_dim, 2, dtype=np.float32) / axis_dim)).astype(np.float32)
    pos = np.stack([t // GRID_W, t % GRID_W], axis=-1).astype(np.float32)
    ang = jnp.asarray(pos[:, :, None] * inv[None, None, :])
    cos, sin = jnp.cos(ang), jnp.sin(ang)
    cos_h = jnp.concatenate([cos, cos], axis=-1).reshape(seq, HEAD_DIM)
    sin_h = jnp.concatenate([-sin, sin], axis=-1).reshape(seq, HEAD_DIM)
    cos_x = jnp.tile(cos_h, (n_b, LANE // HEAD_DIM))
    sin_x = jnp.tile(sin_h, (n_b, LANE // HEAD_DIM))
    ones = jnp.ones((n_b * ctx_len, LANE), F32)
    return (jnp.concatenate([cos_x, ones], axis=0),
            jnp.concatenate([sin_x, jnp.zeros_like(ones)], axis=0))


def kernel(x, c, ctx, c_ctx, w_ada, b_ada, norm1_g, norm2_g, w_in, w_out, conv_dw_w, conv_dw_b, conv_ln_g,
           conv_ln_b, q_norm_g, k_norm_g, sc_conv_w, na_rpb, w_router, b_router, w_gate_up, b_gate_up, w_down,
           b_down, final_norm_g):
    n_b, seq, d = x.shape
    ctx_len = ctx.shape[1]
    depth = w_ada.shape[0]
    n_e = w_router.shape[2]
    d_e = w_down.shape[2]
    n_x_rows = n_b * seq
    nt = n_x_rows + n_b * ctx_len
    assert d == D_MODEL and seq % GRID_W == 0 and (seq // GRID_W) % NA_RB == 0
    assert seq % TM_PROJ == 0 and (n_b * ctx_len) % TM_PROJ == 0
    assert seq % TM_CONV == 0 and ctx_len % TM_CONV == 0 and seq % ctx_len == 0
    assert seq % KC_GQA == 0 and nt % TD_MOE == 0 and nt % TR_ROUTE == 0 and n_b + 1 <= 8

    tiles_x = n_x_rows // TM_PROJ
    tiles_per_b = seq // TM_PROJ

    def seg_of_tile(i):
        return jnp.where(i < tiles_x, 1 + i // tiles_per_b, 0)

    w_in_b = w_in.astype(BF16)
    w_out_b = w_out.astype(BF16)
    wgu_b = jnp.concatenate([w_gate_up[..., 0::2], w_gate_up[..., 1::2]], axis=-1).astype(BF16)
    bgu_p = jnp.concatenate([b_gate_up[..., 0::2], b_gate_up[..., 1::2]], axis=-1)
    wd_b = w_down.astype(BF16)
    wr_hi = w_router.astype(BF16)
    wr_lo = (w_router - wr_hi.astype(F32)).astype(BF16)
    wr_cat = jnp.concatenate([wr_hi, wr_lo], axis=-1)
    bd = jnp.asarray(np.kron(np.eye(GQA_HEADS, dtype=np.float32),
                             np.full((HEAD_DIM, HEAD_DIM), 1.0 / HEAD_DIM, np.float32))).astype(BF16)
    cos_t, sin_t = _rope_tables(n_b, seq, ctx_len)

    cvec = jnp.zeros((8, d), F32).at[0].set(c_ctx).at[1:1 + n_b].set(c)
    mods = _ada_all(cvec, w_ada, b_ada).reshape(depth, 8, N_MOD, d)

    n_assign = nt * TOP_K
    n_blk = -(-n_assign // BLK_MOE) + n_e
    xs_zero = jnp.zeros((n_blk * BLK_MOE, d), F32)

    xs = jnp.concatenate([x.reshape(n_x_rows, d), ctx.reshape(n_b * ctx_len, d)], axis=0)
    cblk0 = n_x_rows // ctx_len

    for l in range(depth):
        mod = mods[l]
        pa, q, k, v, ps, nq, nk, nv = _in_proj(
            xs, mod, norm1_g[l][None], w_in_b[l], cos_t, sin_t,
            jnp.tile(q_norm_g[l], GQA_HEADS)[None], jnp.tile(k_norm_g[l], GQA_KV_HEADS)[None], bd, seg_of_tile)
        oa, oc = _conv_mix(pa, ps, conv_dw_w[l], conv_dw_b[l][None], conv_ln_g[l][None], conv_ln_b[l][None],
                           sc_conv_w[l], seq, n_x_rows, ctx_len)
        ob = _gqa(q, k, v, n_b, seq, ctx_len)
        ob = _ctx_attn(q, k, v, ob, n_b, ctx_len, cblk0)
        od = _na(nq, nk, nv, _na_bias_table(na_rpb[l]), n_b, seq, ctx_len)
        od = _ctx_attn(nq, nk, nv, od, n_b, ctx_len, cblk0)
        x1, h2, logits = _out_proj(oa, ob, oc, od, xs, mod, norm2_g[l][None], w_out_b[l],
                                   wr_cat[l], wr_hi[l], b_router[l][None], seg_of_tile)

        e_o, pos_o, gate_o, cnt = _route(logits)
        counts = cnt[0]
        padded = (counts + BLK_MOE - 1) // BLK_MOE * BLK_MOE
        pad_end = jnp.cumsum(padded)
        pad_start = pad_end - padded
        dest = (pad_start[e_o[:, :TOP_K]] + pos_o[:, :TOP_K]).reshape(-1).astype(I32)
        blk_e = jnp.minimum(jnp.searchsorted(pad_end, jnp.arange(n_blk, dtype=I32) * BLK_MOE, side="right"),
                            n_e - 1).astype(I32)
        n_act = (pad_end[-1:] // BLK_MOE).astype(I32)

        x_sorted = _dispatch(dest, h2, xs_zero)
        ys = _ffn(blk_e, n_act, x_sorted, wgu_b[l], bgu_p[l], wd_b[l], b_down[l])
        xs = _combine(dest, ys, gate_o, x1, mod, seg_of_tile)

    out = _final_norm(xs, final_norm_g[None], n_x_rows)
    return out.reshape(n_b, seq, d)
```

```python
import functools

import jax
import jax.numpy as jnp
import numpy as np
from jax import lax
from jax.experimental import pallas as pl
from jax.experimental.pallas import tpu as pltpu

F32 = jnp.float32
BF16 = jnp.bfloat16
I32 = jnp.int32

D_MODEL = 2048
GRID_W = 64
HEAD_DIM = 64
CONV_W = 512
GQA_HEADS = 8
GQA_KV_HEADS = 2
SC_W = 512
NA_HEADS = 8
GQA_Q_W = GQA_HEADS * HEAD_DIM
GQA_KV_W = GQA_KV_HEADS * HEAD_DIM
NA_W = NA_HEADS * HEAD_DIM
MIX_W = CONV_W + GQA_Q_W + SC_W + NA_W
IN_COLS = 2 * CONV_W + GQA_Q_W + 2 * GQA_KV_W + 3 * SC_W + 3 * NA_W
CONV_K = 31
SC_K = 3
WIN_ROWS = 8
WIN_COLS = 16
ROPE_THETA = 10000.0
ATTN_SCALE = HEAD_DIM ** -0.5
TOP_K = 4
SWIGLU_LIMIT = 7.0
SWIGLU_ALPHA = 1.702
N_MOD = 6
EPS = 1e-6
MASK_VALUE = -1e30
LOG2E = 1.4426950408889634
Q_SCALE = ATTN_SCALE * LOG2E

C_AV, C_GQ = 0, 2 * CONV_W
C_GK = C_GQ + GQA_Q_W
C_GV = C_GK + GQA_KV_W
C_SH = C_GV + GQA_KV_W
C_NQ = C_SH + 3 * SC_W
C_NK = C_NQ + NA_W
C_NV = C_NK + NA_W

VMEM_LIMIT = 56 * 1024 * 1024
LANE = 128

TM_PROJ = 512
TM_CONV = 256
HALO = 16
TQ_GQA = 128
KC_GQA = 512
NA_RB = 8
TR_ROUTE = 256
TD_MOE = 256
BLK_MOE = 256


def _cparams(sem):
    return pltpu.CompilerParams(dimension_semantics=sem, vmem_limit_bytes=VMEM_LIMIT)


def _resident(shape, index_map):
    return pl.BlockSpec(shape, index_map, pipeline_mode=pl.Buffered(1))


def _split_bf16(a):
    hi = a.astype(BF16)
    lo = (a - hi.astype(F32)).astype(BF16)
    return hi, lo


PREP_COLS = 256
PREP_ROWS = 1024


def _deint_kernel(w_ref, p_ref, o_ref):
    n = w_ref.shape[1]
    half = PREP_COLS // 2
    for b in range(n // PREP_COLS):
        t = jnp.dot(w_ref[:, b * PREP_COLS:(b + 1) * PREP_COLS].astype(BF16), p_ref[...],
                    preferred_element_type=F32).astype(BF16)
        o_ref[:, b * half:(b + 1) * half] = t[:, :half]
        o_ref[:, n // 2 + b * half:n // 2 + (b + 1) * half] = t[:, half:]


def _deinterleave_cast(w):
    shape = w.shape
    n = shape[-1]
    rows = int(np.prod(shape[:-1]))
    assert rows % PREP_ROWS == 0 and n % PREP_COLS == 0
    perm = np.zeros((PREP_COLS, PREP_COLS), np.float32)
    j = np.arange(PREP_COLS // 2)
    perm[2 * j, j] = 1.0
    perm[2 * j + 1, PREP_COLS // 2 + j] = 1.0
    out = pl.pallas_call(
        _deint_kernel,
        out_shape=jax.ShapeDtypeStruct((rows, n), BF16),
        grid=(rows // PREP_ROWS,),
        in_specs=[pl.BlockSpec((PREP_ROWS, n), lambda i: (i, 0)),
                  pl.BlockSpec((PREP_COLS, PREP_COLS), lambda i: (0, 0))],
        out_specs=pl.BlockSpec((PREP_ROWS, n), lambda i: (i, 0)),
        compiler_params=_cparams(("parallel",)),
        name="wgu_prep",
    )(w.reshape(rows, n), jnp.asarray(perm, BF16))
    return out.reshape(shape)


def _ada_kernel(c_ref, w_ref, b_ref, o_ref):
    c = c_ref[...]
    s = c * jax.nn.sigmoid(c)
    hi, lo = _split_bf16(s)
    lhs = jnp.concatenate([hi, lo], axis=0)
    r = jnp.dot(lhs, w_ref[0].astype(BF16), preferred_element_type=F32)
    o_ref[0] = r[:8] + r[8:] + b_ref[0]


def _ada_all(cvec, w_ada, b_ada):
    n_l, d, n6 = w_ada.shape
    tn = 1024
    return pl.pallas_call(
        _ada_kernel,
        out_shape=jax.ShapeDtypeStruct((n_l, 8, n6), F32),
        grid=(n_l, n6 // tn),
        in_specs=[
            pl.BlockSpec((8, d), lambda l, j: (0, 0)),
            pl.BlockSpec((1, d, tn), lambda l, j: (l, 0, j)),
            pl.BlockSpec((1, 1, tn), lambda l, j: (l, 0, j)),
        ],
        out_specs=pl.BlockSpec((1, 8, tn), lambda l, j: (l, 0, j)),
        compiler_params=_cparams(("parallel", "parallel")),
        name="ada_mod",
    )(cvec, w_ada, b_ada.reshape(n_l, 1, n6))


def _rope(x, cos, sin_signed):
    n = x.shape[1]
    lane = lax.broadcasted_iota(I32, x.shape, 1)
    first = (lane % 32) < 16
    partner = jnp.where(first, pltpu.roll(x, n - 16, 1), pltpu.roll(x, 16, 1))
    return x * cos + partner * sin_signed


def _head_mean_sq(y, bd):
    hi, lo = _split_bf16(y * y)
    return (jnp.dot(hi, bd, preferred_element_type=F32)
            + jnp.dot(lo, bd, preferred_element_type=F32))


def _in_kernel(x_ref, mod_ref, g_ref, w_ref, cos_ref, sin_ref, qg_ref, kg_ref, bd_ref,
               pa_ref, q_ref, k_ref, v_ref, ps_ref, nq_ref, nk_ref, nv_ref):
    x = x_ref[...]
    ms = jnp.mean(x * x, axis=-1, keepdims=True)
    y = x * lax.rsqrt(ms + EPS) * g_ref[...]
    shift = mod_ref[0, 0:1, :]
    scale = mod_ref[0, 1:2, :]
    h = (y * (1 + scale) + shift).astype(BF16)

    def proj(lo, hi):
        return jnp.dot(h, w_ref[:, lo:hi], preferred_element_type=F32)

    pa_ref[...] = proj(C_AV, C_GQ).astype(BF16)

    cos = cos_ref[...]
    sin = sin_ref[...]
    gq = proj(C_GQ, C_GK)
    qn = gq * lax.rsqrt(_head_mean_sq(gq, bd_ref[...]) + EPS) * qg_ref[...]
    cos4 = jnp.concatenate([cos] * 4, axis=1)
    sin4 = jnp.concatenate([sin] * 4, axis=1)
    q_ref[...] = (_rope(qn, cos4, sin4) * Q_SCALE).astype(BF16)

    gk = proj(C_GK, C_GV)
    kn = gk * lax.rsqrt(_head_mean_sq(gk, bd_ref[0:GQA_KV_W, 0:GQA_KV_W]) + EPS) * kg_ref[...]
    kr = _rope(kn, cos, sin).astype(BF16)
    hd = HEAD_DIM
    k_ref[...] = jnp.concatenate([kr[:, :hd], kr[:, :hd], kr[:, hd:], kr[:, hd:]], axis=1)
    vv = proj(C_GV, C_SH).astype(BF16)
    ones = jnp.ones((vv.shape[0], hd), BF16)
    v_ref[...] = jnp.concatenate([vv[:, :hd], ones, vv[:, hd:], ones], axis=1)
    ps_ref[...] = proj(C_SH, C_NQ).astype(BF16)
    nq_ref[...] = (proj(C_NQ, C_NK) * Q_SCALE).astype(BF16)
    nk_ref[...] = proj(C_NK, C_NV).astype(BF16)
    nv_ref[...] = proj(C_NV, IN_COLS).astype(BF16)


def _in_proj(xs, mod, g1, w_in, cos_t, sin_t, qg, kg, bd, seg_of_tile):
    nt, d = xs.shape
    tm = TM_PROJ
    row = lambda i: (i, 0)
    const = lambda i: (0, 0)
    widths = (2 * CONV_W, GQA_Q_W, 2 * GQA_KV_W, 2 * GQA_KV_W, 3 * SC_W, NA_W, NA_W, NA_W)
    return pl.pallas_call(
        _in_kernel,
        out_shape=[jax.ShapeDtypeStruct((nt, w), BF16) for w in widths],
        grid=(nt // tm,),
        in_specs=[
            pl.BlockSpec((tm, d), row),
            pl.BlockSpec((1, N_MOD, d), lambda i: (seg_of_tile(i), 0, 0)),
            pl.BlockSpec((1, d), const),
            _resident((d, IN_COLS), const),
            pl.BlockSpec((tm, LANE), row),
            pl.BlockSpec((tm, LANE), row),
            pl.BlockSpec((1, GQA_Q_W), const),
            pl.BlockSpec((1, GQA_KV_W), const),
            pl.BlockSpec((GQA_Q_W, GQA_Q_W), const),
        ],
        out_specs=[pl.BlockSpec((tm, w), row) for w in widths],
        compiler_params=_cparams(("parallel",)),
        name="in_proj",
    )(xs, mod, g1, w_in, cos_t, sin_t, qg, kg, bd)


def _conv_kernel(seq, n_x_rows, ctx_len,
                 pa_c, pa_p, pa_n, ps_c, ps_p, ps_n, dww_ref, dwb_ref, lng_ref, lnb_ref, scw_ref,
                 oa_ref, oc_ref, ext_ref, ext2_ref, acc_ref):
    tc = TM_CONV
    row0 = pl.program_id(0) * tc
    in_x = row0 < n_x_rows
    rel = jnp.where(in_x, row0, row0 - n_x_rows)
    seg = jnp.where(in_x, seq, ctx_len)
    keep_p = jnp.where(rel % seg == 0, 0.0, 1.0).astype(F32)
    keep_n = jnp.where((rel + tc) % seg == 0, 0.0, 1.0).astype(F32)

    def glu(ref):
        a = ref[...].astype(F32)
        return a[:, :CONV_W] * jax.nn.sigmoid(a[:, CONV_W:])

    ext_ref[0:HALO, :] = glu(pa_p) * keep_p
    ext_ref[HALO:HALO + tc, :] = glu(pa_c)
    ext_ref[HALO + tc:, :] = glu(pa_n) * keep_n

    pad = HALO - CONV_K // 2
    rc = 64
    for c in range(CONV_W // LANE):
        cs = slice(c * LANE, (c + 1) * LANE)
        for r in range(tc // rc):
            acc = jnp.zeros((rc, LANE), F32)
            for k in range(CONV_K):
                lo = r * rc + k + pad
                acc = acc + dww_ref[k:k + 1, cs] * ext_ref[lo:lo + rc, cs]
            acc_ref[r * rc:(r + 1) * rc, cs] = acc + dwb_ref[:, cs]

    u = acc_ref[...]
    mu = jnp.mean(u, axis=-1, keepdims=True)
    var = jnp.mean(jnp.square(u - mu), axis=-1, keepdims=True)
    yn = (u - mu) * lax.rsqrt(var + EPS) * lng_ref[...] + lnb_ref[...]
    oa_ref[...] = (yn * jax.nn.sigmoid(yn)).astype(BF16)

    def gated(ref, lo, hi):
        a = ref[lo:hi, :].astype(F32)
        return a[:, 2 * SC_W:] * a[:, :SC_W]

    ext2_ref[0:8, :] = gated(ps_p, HALO - 8, HALO) * keep_p
    ext2_ref[8:8 + tc, :] = gated(ps_c, 0, tc)
    ext2_ref[8 + tc:, :] = gated(ps_n, 0, 8) * keep_n
    conv = (scw_ref[0:1, :] * ext2_ref[7:7 + tc, :]
            + scw_ref[1:2, :] * ext2_ref[8:8 + tc, :]
            + scw_ref[2:3, :] * ext2_ref[9:9 + tc, :])
    oc_ref[...] = (ps_c[:, SC_W:2 * SC_W].astype(F32) * conv).astype(BF16)


def _conv_mix(pa, ps, dww, dwb, lng, lnb, scw, seq, n_x_rows, ctx_len):
    nt = pa.shape[0]
    tc = TM_CONV
    hb = tc // HALO
    n_hb = nt // HALO
    row = lambda i: (i, 0)
    prev = lambda i: (jnp.maximum(i * hb - 1, 0), 0)
    nxt = lambda i: (jnp.minimum((i + 1) * hb, n_hb - 1), 0)
    const = lambda i: (0, 0)
    wa, ws = pa.shape[1], ps.shape[1]
    return pl.pallas_call(
        functools.partial(_conv_kernel, seq, n_x_rows, ctx_len),
        out_shape=[jax.ShapeDtypeStruct((nt, CONV_W), BF16), jax.ShapeDtypeStruct((nt, SC_W), BF16)],
        grid=(nt // tc,),
        in_specs=[
            pl.BlockSpec((tc, wa), row), pl.BlockSpec((HALO, wa), prev), pl.BlockSpec((HALO, wa), nxt),
            pl.BlockSpec((tc, ws), row), pl.BlockSpec((HALO, ws), prev), pl.BlockSpec((HALO, ws), nxt),
            pl.BlockSpec((CONV_K, CONV_W), const), pl.BlockSpec((1, CONV_W), const),
            pl.BlockSpec((1, CONV_W), const), pl.BlockSpec((1, CONV_W), const),
            pl.BlockSpec((SC_K, SC_W), const),
        ],
        out_specs=[pl.BlockSpec((tc, CONV_W), row), pl.BlockSpec((tc, SC_W), row)],
        scratch_shapes=[pltpu.VMEM((tc + 2 * HALO, CONV_W), F32),
                        pltpu.VMEM((tc + 16, SC_W), F32),
                        pltpu.VMEM((tc, CONV_W), F32)],
        compiler_params=_cparams(("parallel",)),
        name="conv_mix",
    )(pa, pa, pa, ps, ps, ps, dww, dwb, lng, lnb, scw)


def _qk(q, k):
    return lax.dot_general(q, k, (((1,), (1,)), ((), ())), preferred_element_type=F32)


def _softmax_pv(s, v):
    m = jnp.max(s, axis=-1, keepdims=True)
    p = jnp.exp2(s - m)
    l = jnp.sum(p, axis=-1, keepdims=True)
    return jnp.dot(p.astype(BF16), v, preferred_element_type=F32) / l


def _half_masks(rows):
    lane = lax.broadcasted_iota(I32, (rows, LANE), 1)
    lo = lane < HEAD_DIM
    return lo, jnp.logical_not(lo)


def _ctx_attn_kernel(kv_rep, kv_step, q_ref, k_ref, v_ref, o_in_ref, o_ref):
    del o_in_ref
    n_h = q_ref.shape[1] // HEAD_DIM
    outs = []
    for h in range(n_h):
        off = (h // kv_rep) * kv_step
        q = q_ref[:, h * HEAD_DIM:(h + 1) * HEAD_DIM]
        k = k_ref[:, off:off + HEAD_DIM]
        v = v_ref[:, off:off + HEAD_DIM]
        outs.append(_softmax_pv(_qk(q, k), v))
    o_ref[...] = jnp.concatenate(outs, axis=1).astype(BF16)


def _ctx_attn(q, k, v, o_all, n_b, ctx_len, blk0, kv_rep, kv_step):
    kw = k.shape[1]
    qw = q.shape[1]
    rows = lambda b: (blk0 + b, 0)
    return pl.pallas_call(
        functools.partial(_ctx_attn_kernel, kv_rep, kv_step),
        out_shape=jax.ShapeDtypeStruct(o_all.shape, o_all.dtype),
        grid=(n_b,),
        in_specs=[pl.BlockSpec((ctx_len, qw), rows), pl.BlockSpec((ctx_len, kw), rows),
                  pl.BlockSpec((ctx_len, kw), rows), pl.BlockSpec(memory_space=pl.ANY)],
        out_specs=pl.BlockSpec((ctx_len, qw), rows),
        input_output_aliases={3: 0},
        compiler_params=_cparams(("parallel",)),
        name="ctx_attn",
    )(q, k, v, o_all)


def _gqa_kernel(n_chunks, q_ref, kx_ref, vx_ref, kc_ref, vc_ref, o_ref):
    tq = q_ref.shape[0]
    grp = GQA_HEADS // GQA_KV_HEADS
    lo, hi = _half_masks(tq)

    def q_group(g):
        parts = []
        for j in range(grp):
            h = g * grp + j
            blk = q_ref[:, (h // 2) * LANE:(h // 2 + 1) * LANE].astype(F32)
            parts.append(jnp.where(lo if h % 2 == 0 else hi, blk, 0.0).astype(BF16))
        return jnp.concatenate(parts, axis=0)

    def update(q, k, v, m, acc):
        s = _qk(q, k)
        m_new = jnp.maximum(m, jnp.max(s, axis=-1, keepdims=True))
        p = jnp.exp2(s - m_new)
        acc = jnp.exp2(m - m_new) * acc + jnp.dot(p.astype(BF16), v, preferred_element_type=F32)
        return m_new, acc

    qs = [q_group(g) for g in range(GQA_KV_HEADS)]
    gsl = [slice(g * LANE, (g + 1) * LANE) for g in range(GQA_KV_HEADS)]
    state = []
    for g in range(GQA_KV_HEADS):
        s = _qk(qs[g], kc_ref[:, gsl[g]])
        m = jnp.max(s, axis=-1, keepdims=True)
        p = jnp.exp2(s - m)
        state += [m, jnp.dot(p.astype(BF16), vc_ref[:, gsl[g]], preferred_element_type=F32)]

    def body(c, carry):
        rows = pl.ds(pl.multiple_of(c * KC_GQA, KC_GQA), KC_GQA)
        out = []
        for g in range(GQA_KV_HEADS):
            out += update(qs[g], kx_ref[rows, gsl[g]], vx_ref[rows, gsl[g]], carry[2 * g], carry[2 * g + 1])
        return tuple(out)

    state = lax.fori_loop(0, n_chunks, body, tuple(state), unroll=4)

    lo4, _ = _half_masks(grp * tq)
    heads = []
    for g in range(GQA_KV_HEADS):
        acc = state[2 * g + 1]
        inv = 1.0 / jnp.where(lo4, 1.0, acc)
        o = acc * pltpu.roll(inv, HEAD_DIM, 1)
        heads.extend(o[j * tq:(j + 1) * tq] for j in range(grp))
    pairs = [jnp.where(lo, heads[2 * i], pltpu.roll(heads[2 * i + 1], HEAD_DIM, 1))
             for i in range(GQA_HEADS // 2)]
    o_ref[...] = jnp.concatenate(pairs, axis=1).astype(BF16)


def _gqa(q, k, v, n_b, seq, ctx_len):
    nt = q.shape[0]
    tq = TQ_GQA
    nq = seq // tq
    cblk0 = n_b * seq // ctx_len
    kw = k.shape[1]
    return pl.pallas_call(
        functools.partial(_gqa_kernel, seq // KC_GQA),
        out_shape=jax.ShapeDtypeStruct((nt, GQA_Q_W), BF16),
        grid=(n_b, nq),
        in_specs=[
            pl.BlockSpec((tq, GQA_Q_W), lambda b, j: (b * nq + j, 0)),
            pl.BlockSpec((seq, kw), lambda b, j: (b, 0)),
            pl.BlockSpec((seq, kw), lambda b, j: (b, 0)),
            pl.BlockSpec((ctx_len, kw), lambda b, j: (cblk0 + b, 0)),
            pl.BlockSpec((ctx_len, kw), lambda b, j: (cblk0 + b, 0)),
        ],
        out_specs=pl.BlockSpec((tq, GQA_Q_W), lambda b, j: (b * nq + j, 0)),
        compiler_params=_cparams(("parallel", "parallel")),
        name="gqa_flash",
    )(q, k, v, k, v)


def _na_kernel(n_rows, q_ref, k_ref, v_ref, kc_ref, vc_ref, t_ref, o_ref):
    rb = pl.program_id(2)
    win = WIN_ROWS * GRID_W
    tq = q_ref.shape[0]
    lo, hi = _half_masks(tq)
    qf = q_ref[...].astype(F32)
    kc = kc_ref[...]
    vc = vc_ref[...]
    halves = []
    for j in range(LANE // HEAD_DIM):
        qa = jnp.where(lo if j == 0 else hi, qf, 0.0).astype(BF16)
        s_ctx = _qk(qa, kc)
        m_ctx = jnp.max(s_ctx, axis=-1, keepdims=True)
        s_wins, ms, krows_l = [], [], []
        for i in range(NA_RB):
            r = rb * NA_RB + i
            r0 = jnp.clip(r - WIN_ROWS // 2, 0, n_rows - WIN_ROWS)
            shift = r0 - r + (WIN_ROWS - 1)
            krows = pl.ds(pl.multiple_of(r0 * GRID_W, GRID_W), win)
            rs = slice(i * GRID_W, (i + 1) * GRID_W)
            bias = jnp.concatenate([t_ref[j, 2 * wp + shift] for wp in range(WIN_ROWS // 2)], axis=1)
            s_win = _qk(qa[rs], k_ref[krows, :]) + bias
            s_wins.append(s_win)
            ms.append(jnp.maximum(jnp.max(s_win, axis=-1, keepdims=True), m_ctx[rs]))
            krows_l.append(krows)
        m_all = jnp.concatenate(ms, axis=0)
        p_ctx = jnp.exp2(s_ctx - m_all)
        l_ctx = jnp.sum(p_ctx, axis=-1, keepdims=True)
        o_ctx = jnp.dot(p_ctx.astype(BF16), vc, preferred_element_type=F32)
        outs = []
        for i in range(NA_RB):
            rs = slice(i * GRID_W, (i + 1) * GRID_W)
            p = jnp.exp2(s_wins[i] - ms[i])
            l = jnp.sum(p, axis=-1, keepdims=True) + l_ctx[rs]
            o = jnp.dot(p.astype(BF16), v_ref[krows_l[i], :], preferred_element_type=F32) + o_ctx[rs]
            outs.append(o / l)
        halves.append(jnp.concatenate(outs, axis=0))
    o_ref[...] = jnp.where(lo, halves[0], halves[1]).astype(BF16)


def _na(nq, nk, nv, table, n_b, seq, ctx_len):
    nt = nq.shape[0]
    n_rows = seq // GRID_W
    n_rb = n_rows // NA_RB
    tq = NA_RB * GRID_W
    cblk0 = n_b * seq // ctx_len
    hp = LANE // HEAD_DIM
    return pl.pallas_call(
        functools.partial(_na_kernel, n_rows),
        out_shape=jax.ShapeDtypeStruct((nt, NA_W), BF16),
        grid=(n_b, NA_W // LANE, n_rb),
        in_specs=[
            pl.BlockSpec((tq, LANE), lambda b, h, r: (b * n_rb + r, h)),
            pl.BlockSpec((seq, LANE), lambda b, h, r: (b, h)),
            pl.BlockSpec((seq, LANE), lambda b, h, r: (b, h)),
            pl.BlockSpec((ctx_len, LANE), lambda b, h, r: (cblk0 + b, h)),
            pl.BlockSpec((ctx_len, LANE), lambda b, h, r: (cblk0 + b, h)),
            pl.BlockSpec((hp, 2 * WIN_ROWS - 2, GRID_W, LANE), lambda b, h, r: (h, 0, 0, 0)),
        ],
        out_specs=pl.BlockSpec((tq, LANE), lambda b, h, r: (b * n_rb + r, h)),
        compiler_params=_cparams(("parallel", "parallel", "parallel")),
        name="na_attn",
    )(nq, nk, nv, nk, nv, table)


def _na_bias_table(rpb):
    cols = np.arange(GRID_W)
    c0 = np.clip(cols - WIN_COLS // 2, 0, GRID_W - WIN_COLS)
    kc = np.arange(GRID_W)
    rel = kc[None, :] - cols[:, None] + (WIN_COLS - 1)
    valid = (kc[None, :] >= c0[:, None]) & (kc[None, :] < c0[:, None] + WIN_COLS)
    rel = np.clip(rel, 0, 2 * WIN_COLS - 2)
    t = rpb[:, :, rel]
    t = jnp.where(jnp.asarray(valid)[None, None], t * LOG2E, MASK_VALUE).astype(F32)
    return jnp.concatenate([t[:, :-1], t[:, 1:]], axis=-1)


def _out_kernel(oa_ref, ob_ref, oc_ref, od_ref, x_ref, mod_ref, g_ref, w_ref, wr_cat_ref, wr_hi_ref, br_ref,
                x1_ref, h2_ref, lg_ref):
    acc = jnp.dot(oa_ref[...], w_ref[0:CONV_W, :], preferred_element_type=F32)
    acc = acc + jnp.dot(ob_ref[...], w_ref[CONV_W:CONV_W + GQA_Q_W, :], preferred_element_type=F32)
    acc = acc + jnp.dot(oc_ref[...], w_ref[CONV_W + GQA_Q_W:MIX_W - NA_W, :], preferred_element_type=F32)
    acc = acc + jnp.dot(od_ref[...], w_ref[MIX_W - NA_W:MIX_W, :], preferred_element_type=F32)
    x1 = x_ref[...] + mod_ref[0, 2:3, :] * acc
    x1_ref[...] = x1
    ms = jnp.mean(x1 * x1, axis=-1, keepdims=True)
    y = x1 * lax.rsqrt(ms + EPS) * g_ref[...]
    h2 = y * (1 + mod_ref[0, 4:5, :]) + mod_ref[0, 3:4, :]
    h2_ref[...] = h2
    hi, lo = _split_bf16(h2)
    n_e = lg_ref.shape[1]
    a = jnp.dot(hi, wr_cat_ref[...], preferred_element_type=F32)
    b = jnp.dot(lo, wr_hi_ref[...], preferred_element_type=F32)
    lg_ref[...] = a[:, :n_e] + a[:, n_e:] + b + br_ref[...]


def _out_proj(oa, ob, oc, od, xs, mod, g2, w_out, wr_cat, wr_hi, br, seg_of_tile):
    nt, d = xs.shape
    tm = TM_PROJ
    n_e = wr_hi.shape[1]
    row = lambda i: (i, 0)
    const = lambda i: (0, 0)
    return pl.pallas_call(
        _out_kernel,
        out_shape=[jax.ShapeDtypeStruct((nt, d), F32), jax.ShapeDtypeStruct((nt, d), F32),
                   jax.ShapeDtypeStruct((nt, n_e), F32)],
        grid=(nt // tm,),
        in_specs=[
            pl.BlockSpec((tm, CONV_W), row), pl.BlockSpec((tm, GQA_Q_W), row),
            pl.BlockSpec((tm, SC_W), row), pl.BlockSpec((tm, NA_W), row),
            pl.BlockSpec((tm, d), row),
            pl.BlockSpec((1, N_MOD, d), lambda i: (seg_of_tile(i), 0, 0)),
            pl.BlockSpec((1, d), const),
            _resident((MIX_W, d), const),
            pl.BlockSpec((d, 2 * n_e), const), pl.BlockSpec((d, n_e), const), pl.BlockSpec((1, n_e), const),
        ],
        out_specs=[pl.BlockSpec((tm, d), row), pl.BlockSpec((tm, d), row), pl.BlockSpec((tm, n_e), row)],
        compiler_params=_cparams(("parallel",)),
        name="out_proj",
    )(oa, ob, oc, od, xs, mod, g2, w_out, wr_cat, wr_hi, br)


def _route_kernel(lg_ref, e_ref, pos_ref, gate_ref, cnt_ref, carry_ref):
    i = pl.program_id(0)

    @pl.when(i == 0)
    def _():
        carry_ref[...] = jnp.zeros_like(carry_ref)

    lg = lg_ref[...]
    tr, n_e = lg.shape
    lane = lax.broadcasted_iota(I32, (tr, n_e), 1)
    work = lg
    vals, idxs = [], []
    for _ in range(TOP_K):
        m = jnp.max(work, axis=-1, keepdims=True)
        idx = jnp.min(jnp.where(work == m, lane, n_e), axis=-1, keepdims=True)
        vals.append(m)
        idxs.append(idx)
        work = jnp.where(lane == idx, -jnp.inf, work)
    exps = [jnp.exp(v - vals[0]) for v in vals]
    den = exps[0] + exps[1] + exps[2] + exps[3]

    mask = jnp.zeros((tr, n_e), F32)
    for idx in idxs:
        mask = mask + jnp.where(lane == idx, 1.0, 0.0)
    r_i = lax.broadcasted_iota(I32, (tr, tr), 0)
    c_i = lax.broadcasted_iota(I32, (tr, tr), 1)
    tri = jnp.where(c_i <= r_i, 1.0, 0.0).astype(BF16)
    incl = jnp.dot(tri, mask.astype(BF16), preferred_element_type=F32)
    before = carry_ref[...] + incl - mask

    lane_o = lax.broadcasted_iota(I32, (tr, LANE), 1)
    e_out = jnp.zeros((tr, LANE), I32)
    pos_out = jnp.zeros((tr, LANE), I32)
    gate_out = jnp.zeros((tr, LANE), F32)
    for k in range(TOP_K):
        pos_k = jnp.sum(jnp.where(lane == idxs[k], before, 0.0), axis=-1, keepdims=True)
        e_out = jnp.where(lane_o == k, idxs[k], e_out)
        pos_out = jnp.where(lane_o == k, pos_k.astype(I32), pos_out)
        gate_out = jnp.where(lane_o == k, exps[k] / den, gate_out)
    e_ref[...] = e_out
    pos_ref[...] = pos_out
    gate_ref[...] = gate_out
    total = carry_ref[...] + incl[tr - 1:tr, :]
    carry_ref[...] = total
    cnt_ref[...] = total.astype(I32)


def _route(logits):
    nt, n_e = logits.shape
    tr = TR_ROUTE
    row = lambda i: (i, 0)
    return pl.pallas_call(
        _route_kernel,
        out_shape=[jax.ShapeDtypeStruct((nt, LANE), I32), jax.ShapeDtypeStruct((nt, LANE), I32),
                   jax.ShapeDtypeStruct((nt, LANE), F32), jax.ShapeDtypeStruct((1, n_e), I32)],
        grid=(nt // tr,),
        in_specs=[pl.BlockSpec((tr, n_e), row)],
        out_specs=[pl.BlockSpec((tr, LANE), row), pl.BlockSpec((tr, LANE), row),
                   pl.BlockSpec((tr, LANE), row), pl.BlockSpec((1, n_e), lambda i: (0, 0))],
        scratch_shapes=[pltpu.VMEM((1, n_e), F32)],
        compiler_params=_cparams(("arbitrary",)),
        name="route",
    )(logits)


def _row_copy(src_ref, src_row, dst_ref, dst_row, sem):
    return pltpu.make_async_copy(src_ref.at[pl.ds(src_row, 1)], dst_ref.at[pl.ds(dst_row, 1)], sem)


def _dispatch_kernel(dest_ref, pad_end_ref, h_ref, xs_ref, zbuf_ref, sem, zsem):
    td = h_ref.shape[0]
    blk = zbuf_ref.shape[0]
    n_e = pad_end_ref.shape[0]
    base = pl.program_id(0) * td * TOP_K

    @pl.when(pl.program_id(0) == 0)
    def _():
        zbuf_ref[...] = jnp.zeros_like(zbuf_ref)

        def fill(e):
            start = pl.multiple_of(pad_end_ref[e] - blk, blk)
            return pltpu.make_async_copy(zbuf_ref, xs_ref.at[pl.ds(start, blk)], zsem)

        def nonempty(e):
            prev = pad_end_ref[jnp.maximum(e - 1, 0)]
            return pad_end_ref[e] > jnp.where(e == 0, 0, prev)

        def start(e, _):
            @pl.when(nonempty(e))
            def _():
                fill(e).start()
            return 0

        def wait(e, _):
            @pl.when(nonempty(e))
            def _():
                fill(e).wait()
            return 0

        lax.fori_loop(0, n_e, start, 0)
        lax.fori_loop(0, n_e, wait, 0)

    def issue(t, _):
        for k in range(TOP_K):
            _row_copy(h_ref, t, xs_ref, dest_ref[base + t * TOP_K + k], sem).start()
        return 0

    lax.fori_loop(0, td, issue, 0)

    def drain(t, _):
        for k in range(TOP_K):
            _row_copy(h_ref, t, xs_ref, dest_ref[base + t * TOP_K + k], sem).wait()
        return 0

    lax.fori_loop(0, td, drain, 0)


def _dispatch(dest, pad_end, h2, n_slots):
    nt, d = h2.shape
    td = TD_MOE
    return pl.pallas_call(
        _dispatch_kernel,
        out_shape=jax.ShapeDtypeStruct((n_slots, d), F32),
        grid_spec=pltpu.PrefetchScalarGridSpec(
            num_scalar_prefetch=2,
            grid=(nt // td,),
            in_specs=[pl.BlockSpec((td, d), lambda i, dest, pe: (i, 0))],
            out_specs=pl.BlockSpec(memory_space=pl.ANY),
            scratch_shapes=[pltpu.VMEM((BLK_MOE, d), F32), pltpu.SemaphoreType.DMA, pltpu.SemaphoreType.DMA],
        ),
        compiler_params=_cparams(("arbitrary",)),
        name="moe_dispatch",
    )(dest, pad_end, h2)


def _ffn_kernel(blk_e_ref, n_act_ref, x_ref, wgu_ref, bgu_ref, wd_ref, bd_ref, y_ref):
    del blk_e_ref

    @pl.when(pl.program_id(0) < n_act_ref[0])
    def _():
        d_e = wd_ref.shape[1]
        xb = x_ref[...].astype(BF16)
        h = jnp.dot(xb, wgu_ref[0], preferred_element_type=F32) + bgu_ref[0]
        glu = jnp.minimum(h[:, :d_e], SWIGLU_LIMIT)
        lin = jnp.clip(h[:, d_e:], -SWIGLU_LIMIT, SWIGLU_LIMIT)
        act = glu * jax.nn.sigmoid(SWIGLU_ALPHA * glu) * (lin + 1)
        y_ref[...] = jnp.dot(act.astype(BF16), wd_ref[0], preferred_element_type=F32) + bd_ref[0]


def _ffn(blk_e, n_act, xs, wgu, bgu, wd, bd):
    ns, d = xs.shape
    blk = BLK_MOE
    n_e, _, d_e2 = wgu.shape
    d_e = d_e2 // 2

    def rows(b, be, na):
        return (jnp.minimum(b, na[0] - 1), 0)

    def expert3(b, be, na):
        return (be[jnp.minimum(b, na[0] - 1)], 0, 0)

    return pl.pallas_call(
        _ffn_kernel,
        out_shape=jax.ShapeDtypeStruct((ns, d), F32),
        grid_spec=pltpu.PrefetchScalarGridSpec(
            num_scalar_prefetch=2,
            grid=(ns // blk,),
            in_specs=[
                pl.BlockSpec((blk, d), rows),
                pl.BlockSpec((1, d, d_e2), expert3), pl.BlockSpec((1, 1, d_e2), expert3),
                pl.BlockSpec((1, d_e, d), expert3), pl.BlockSpec((1, 1, d), expert3),
            ],
            out_specs=pl.BlockSpec((blk, d), rows),
        ),
        compiler_params=_cparams(("arbitrary",)),
        name="moe_ffn",
    )(blk_e, n_act, xs, wgu, bgu.reshape(n_e, 1, d_e2), wd, bd.reshape(n_e, 1, d))


def _combine_kernel(dest_ref, ys_ref, gate_ref, x1_ref, mod_ref, o_ref, buf_ref, sem):
    td = x1_ref.shape[0]
    base = pl.program_id(0) * td * TOP_K

    def issue(t, _):
        for k in range(TOP_K):
            _row_copy(ys_ref, dest_ref[base + t * TOP_K + k], buf_ref.at[k], t, sem).start()
        return 0

    lax.fori_loop(0, td, issue, 0)

    def drain(t, _):
        for k in range(TOP_K):
            _row_copy(ys_ref, dest_ref[base + t * TOP_K + k], buf_ref.at[k], t, sem).wait()
        return 0

    lax.fori_loop(0, td, drain, 0)

    g = gate_ref[...]
    y = g[:, 0:1] * buf_ref[0]
    for k in range(1, TOP_K):
        y = y + g[:, k:k + 1] * buf_ref[k]
    o_ref[...] = x1_ref[...] + mod_ref[0, 5:6, :] * y


def _combine(dest, ys, gate, x1, mod, seg_of_tile):
    nt, d = x1.shape
    td = TD_MOE
    ratio = TM_PROJ // td
    return pl.pallas_call(
        _combine_kernel,
        out_shape=jax.ShapeDtypeStruct((nt, d), F32),
        grid_spec=pltpu.PrefetchScalarGridSpec(
            num_scalar_prefetch=1,
            grid=(nt // td,),
            in_specs=[
                pl.BlockSpec(memory_space=pl.ANY),
                pl.BlockSpec((td, LANE), lambda i, dest: (i, 0)),
                pl.BlockSpec((td, d), lambda i, dest: (i, 0)),
                pl.BlockSpec((1, N_MOD, d), lambda i, dest: (seg_of_tile(i // ratio), 0, 0)),
            ],
            out_specs=pl.BlockSpec((td, d), lambda i, dest: (i, 0)),
            scratch_shapes=[pltpu.VMEM((TOP_K, td, d), F32), pltpu.SemaphoreType.DMA],
        ),
        compiler_params=_cparams(("arbitrary",)),
        name="moe_combine",
    )(dest, ys, gate, x1, mod)


def _final_kernel(x_ref, g_ref, o_ref):
    x = x_ref[...]
    ms = jnp.mean(x * x, axis=-1, keepdims=True)
    o_ref[...] = x * lax.rsqrt(ms + EPS) * g_ref[...]


def _final_norm(xs, g, n_rows):
    d = xs.shape[1]
    tm = TM_PROJ
    return pl.pallas_call(
        _final_kernel,
        out_shape=jax.ShapeDtypeStruct((n_rows, d), F32),
        grid=(n_rows // tm,),
        in_specs=[pl.BlockSpec((tm, d), lambda i: (i, 0)), pl.BlockSpec((1, d), lambda i: (0, 0))],
        out_specs=pl.BlockSpec((tm, d), lambda i: (i, 0)),
        compiler_params=_cparams(("parallel",)),
        name="final_norm",
    )(xs, g)


def _rope_tables(n_b, seq, ctx_len):
    t = np.arange(seq)
    axis_dim = HEAD_DIM // 2
    inv = (ROPE_THETA ** (-np.arange(0, axis_dim, 2, dtype=np.float32) / axis_dim)).astype(np.float32)
    pos = np.stack([t // GRID_W, t % GRID_W], axis=-1).astype(np.float32)
    ang = jnp.asarray(pos[:, :, None] * inv[None, None, :])
    cos, sin = jnp.cos(ang), jnp.sin(ang)
    cos_h = jnp.concatenate([cos, cos], axis=-1).reshape(seq, HEAD_DIM)
    sin_h = jnp.concatenate([-sin, sin], axis=-1).reshape(seq, HEAD_DIM)
    cos_x = jnp.tile(cos_h, (n_b, LANE // HEAD_DIM))
    sin_x = jnp.tile(sin_h, (n_b, LANE // HEAD_DIM))
    ones = jnp.ones((n_b * ctx_len, LANE), F32)
    return (jnp.concatenate([cos_x, ones], axis=0),
            jnp.concatenate([sin_x, jnp.zeros_like(ones)], axis=0))


def kernel(x, c, ctx, c_ctx, w_ada, b_ada, norm1_g, norm2_g, w_in, w_out, conv_dw_w, conv_dw_b, conv_ln_g,
           conv_ln_b, q_norm_g, k_norm_g, sc_conv_w, na_rpb, w_router, b_router, w_gate_up, b_gate_up, w_down,
           b_down, final_norm_g):
    n_b, seq, d = x.shape
    ctx_len = ctx.shape[1]
    depth = w_ada.shape[0]
    n_e = w_router.shape[2]
    d_e = w_down.shape[2]
    n_x_rows = n_b * seq
    nt = n_x_rows + n_b * ctx_len
    assert d == D_MODEL and seq % GRID_W == 0 and (seq // GRID_W) % NA_RB == 0
    assert seq % TM_PROJ == 0 and (n_b * ctx_len) % TM_PROJ == 0
    assert seq % TM_CONV == 0 and ctx_len % TM_CONV == 0 and seq % ctx_len == 0
    assert seq % KC_GQA == 0 and nt % TD_MOE == 0 and nt % TR_ROUTE == 0 and n_b + 1 <= 8

    tiles_x = n_x_rows // TM_PROJ
    tiles_per_b = seq // TM_PROJ

    def seg_of_tile(i):
        return jnp.where(i < tiles_x, 1 + i // tiles_per_b, 0)

    w_in_b = w_in.astype(BF16)
    w_out_b = w_out.astype(BF16)
    wgu_b = _deinterleave_cast(w_gate_up)
    bgu_p = jnp.concatenate([b_gate_up[..., 0::2], b_gate_up[..., 1::2]], axis=-1)
    wd_b = w_down.astype(BF16)
    wr_hi = w_router.astype(BF16)
    wr_lo = (w_router - wr_hi.astype(F32)).astype(BF16)
    wr_cat = jnp.concatenate([wr_hi, wr_lo], axis=-1)
    bd = jnp.asarray(np.kron(np.eye(GQA_HEADS, dtype=np.float32),
                             np.full((HEAD_DIM, HEAD_DIM), 1.0 / HEAD_DIM, np.float32))).astype(BF16)
    cos_t, sin_t = _rope_tables(n_b, seq, ctx_len)

    cvec = jnp.zeros((8, d), F32).at[0].set(c_ctx).at[1:1 + n_b].set(c)
    mods = _ada_all(cvec, w_ada, b_ada).reshape(depth, 8, N_MOD, d)

    n_assign = nt * TOP_K
    n_blk = -(-n_assign // BLK_MOE) + n_e

    xs = jnp.concatenate([x.reshape(n_x_rows, d), ctx.reshape(n_b * ctx_len, d)], axis=0)
    cblk0 = n_x_rows // ctx_len

    for l in range(depth):
        mod = mods[l]
        pa, q, k, v, ps, nq, nk, nv = _in_proj(
            xs, mod, norm1_g[l][None], w_in_b[l], cos_t, sin_t,
            jnp.tile(q_norm_g[l], GQA_HEADS)[None], jnp.tile(k_norm_g[l], GQA_KV_HEADS)[None], bd, seg_of_tile)
        oa, oc = _conv_mix(pa, ps, conv_dw_w[l], conv_dw_b[l][None], conv_ln_g[l][None], conv_ln_b[l][None],
                           sc_conv_w[l], seq, n_x_rows, ctx_len)
        ob = _gqa(q, k, v, n_b, seq, ctx_len)
        ob = _ctx_attn(q, k, v, ob, n_b, ctx_len, cblk0, GQA_HEADS // GQA_KV_HEADS, LANE)
        od = _na(nq, nk, nv, _na_bias_table(na_rpb[l]), n_b, seq, ctx_len)
        od = _ctx_attn(nq, nk, nv, od, n_b, ctx_len, cblk0, 1, HEAD_DIM)
        x1, h2, logits = _out_proj(oa, ob, oc, od, xs, mod, norm2_g[l][None], w_out_b[l],
                                   wr_cat[l], wr_hi[l], b_router[l][None], seg_of_tile)

        e_o, pos_o, gate_o, cnt = _route(logits)
        counts = cnt[0]
        padded = (counts + BLK_MOE - 1) // BLK_MOE * BLK_MOE
        pad_end = jnp.cumsum(padded)
        pad_start = pad_end - padded
        dest = (pad_start[e_o[:, :TOP_K]] + pos_o[:, :TOP_K]).reshape(-1).astype(I32)
        blk_lo = jnp.arange(n_blk, dtype=I32) * BLK_MOE
        blk_e = jnp.minimum(jnp.sum((pad_end[None, :] <= blk_lo[:, None]).astype(I32), axis=1), n_e - 1)
        n_act = (pad_end[-1:] // BLK_MOE).astype(I32)

        x_sorted = _dispatch(dest, pad_end.astype(I32), h2, n_blk * BLK_MOE)
        ys = _ffn(blk_e, n_act, x_sorted, wgu_b[l], bgu_p[l], wd_b[l], b_down[l])
        xs = _combine(dest, ys, gate_o, x1, mod, seg_of_tile)

    out = _final_norm(xs, final_norm_g[None], n_x_rows)
    return out.reshape(n_b, seq, d)
```

```python
import functools

import jax
import jax.numpy as jnp
import numpy as np
from jax import lax
from jax.experimental import pallas as pl
from jax.experimental.pallas import tpu as pltpu

F32 = jnp.float32
BF16 = jnp.bfloat16
I32 = jnp.int32

D_MODEL = 2048
GRID_W = 64
HEAD_DIM = 64
CONV_W = 512
GQA_HEADS = 8
GQA_KV_HEADS = 2
SC_W = 512
NA_HEADS = 8
GQA_Q_W = GQA_HEADS * HEAD_DIM
GQA_KV_W = GQA_KV_HEADS * HEAD_DIM
NA_W = NA_HEADS * HEAD_DIM
MIX_W = CONV_W + GQA_Q_W + SC_W + NA_W
IN_COLS = 2 * CONV_W + GQA_Q_W + 2 * GQA_KV_W + 3 * SC_W + 3 * NA_W
CONV_K = 31
SC_K = 3
WIN_ROWS = 8
WIN_COLS = 16
ROPE_THETA = 10000.0
ATTN_SCALE = HEAD_DIM ** -0.5
TOP_K = 4
SWIGLU_LIMIT = 7.0
SWIGLU_ALPHA = 1.702
N_MOD = 6
EPS = 1e-6
MASK_VALUE = -1e30
LOG2E = 1.4426950408889634
Q_SCALE = ATTN_SCALE * LOG2E

C_AV, C_GQ = 0, 2 * CONV_W
C_GK = C_GQ + GQA_Q_W
C_GV = C_GK + GQA_KV_W
C_SH = C_GV + GQA_KV_W
C_NQ = C_SH + 3 * SC_W
C_NK = C_NQ + NA_W
C_NV = C_NK + NA_W

VMEM_LIMIT = 56 * 1024 * 1024
LANE = 128

TM_PROJ = 512
TM_CONV = 256
HALO = 16
TQ_GQA = 256
KC_GQA = 512
NA_RB = 8
TR_ROUTE = 256
TD_MOE = 256
BLK_MOE = 256


def _cparams(sem):
    return pltpu.CompilerParams(dimension_semantics=sem, vmem_limit_bytes=VMEM_LIMIT)


def _resident(shape, index_map):
    return pl.BlockSpec(shape, index_map, pipeline_mode=pl.Buffered(1))


def _split_bf16(a):
    hi = a.astype(BF16)
    lo = (a - hi.astype(F32)).astype(BF16)
    return hi, lo


PREP_COLS = 256
PREP_ROWS = 1024


def _deint_kernel(w_ref, p_ref, o_ref):
    n = w_ref.shape[1]
    half = PREP_COLS // 2
    for b in range(n // PREP_COLS):
        t = jnp.dot(w_ref[:, b * PREP_COLS:(b + 1) * PREP_COLS].astype(BF16), p_ref[...],
                    preferred_element_type=F32).astype(BF16)
        o_ref[:, b * half:(b + 1) * half] = t[:, :half]
        o_ref[:, n // 2 + b * half:n // 2 + (b + 1) * half] = t[:, half:]


def _deinterleave_cast(w):
    shape = w.shape
    n = shape[-1]
    rows = int(np.prod(shape[:-1]))
    assert rows % PREP_ROWS == 0 and n % PREP_COLS == 0
    perm = np.zeros((PREP_COLS, PREP_COLS), np.float32)
    j = np.arange(PREP_COLS // 2)
    perm[2 * j, j] = 1.0
    perm[2 * j + 1, PREP_COLS // 2 + j] = 1.0
    out = pl.pallas_call(
        _deint_kernel,
        out_shape=jax.ShapeDtypeStruct((rows, n), BF16),
        grid=(rows // PREP_ROWS,),
        in_specs=[pl.BlockSpec((PREP_ROWS, n), lambda i: (i, 0)),
                  pl.BlockSpec((PREP_COLS, PREP_COLS), lambda i: (0, 0))],
        out_specs=pl.BlockSpec((PREP_ROWS, n), lambda i: (i, 0)),
        compiler_params=_cparams(("parallel",)),
        name="wgu_prep",
    )(w.reshape(rows, n), jnp.asarray(perm, BF16))
    return out.reshape(shape)


def _ada_kernel(c_ref, w_ref, b_ref, o_ref):
    c = c_ref[...]
    s = c * jax.nn.sigmoid(c)
    hi, lo = _split_bf16(s)
    lhs = jnp.concatenate([hi, lo], axis=0)
    r = jnp.dot(lhs, w_ref[0].astype(BF16), preferred_element_type=F32)
    o_ref[0] = r[:8] + r[8:] + b_ref[0]


def _ada_all(cvec, w_ada, b_ada):
    n_l, d, n6 = w_ada.shape
    tn = 1024
    return pl.pallas_call(
        _ada_kernel,
        out_shape=jax.ShapeDtypeStruct((n_l, 8, n6), F32),
        grid=(n_l, n6 // tn),
        in_specs=[
            pl.BlockSpec((8, d), lambda l, j: (0, 0)),
            pl.BlockSpec((1, d, tn), lambda l, j: (l, 0, j)),
            pl.BlockSpec((1, 1, tn), lambda l, j: (l, 0, j)),
        ],
        out_specs=pl.BlockSpec((1, 8, tn), lambda l, j: (l, 0, j)),
        compiler_params=_cparams(("parallel", "parallel")),
        name="ada_mod",
    )(cvec, w_ada, b_ada.reshape(n_l, 1, n6))


def _rope(x, cos, sin_signed):
    n = x.shape[1]
    lane = lax.broadcasted_iota(I32, x.shape, 1)
    first = (lane % 32) < 16
    partner = jnp.where(first, pltpu.roll(x, n - 16, 1), pltpu.roll(x, 16, 1))
    return x * cos + partner * sin_signed


def _head_mean_sq(y, bd):
    hi, lo = _split_bf16(y * y)
    return (jnp.dot(hi, bd, preferred_element_type=F32)
            + jnp.dot(lo, bd, preferred_element_type=F32))


def _in_kernel(x_ref, mod_ref, g_ref, w_ref, cos_ref, sin_ref, qg_ref, kg_ref, bd_ref,
               pa_ref, q_ref, k_ref, v_ref, ps_ref, nq_ref, nk_ref, nv_ref):
    x = x_ref[...]
    ms = jnp.mean(x * x, axis=-1, keepdims=True)
    y = x * lax.rsqrt(ms + EPS) * g_ref[...]
    shift = mod_ref[0, 0:1, :]
    scale = mod_ref[0, 1:2, :]
    h = (y * (1 + scale) + shift).astype(BF16)

    def proj(lo, hi):
        return jnp.dot(h, w_ref[:, lo:hi], preferred_element_type=F32)

    pa_ref[...] = proj(C_AV, C_GQ).astype(BF16)

    cos = cos_ref[...]
    sin = sin_ref[...]
    gq = proj(C_GQ, C_GK)
    qn = gq * lax.rsqrt(_head_mean_sq(gq, bd_ref[...]) + EPS) * qg_ref[...]
    cos4 = jnp.concatenate([cos] * 4, axis=1)
    sin4 = jnp.concatenate([sin] * 4, axis=1)
    q_ref[...] = (_rope(qn, cos4, sin4) * Q_SCALE).astype(BF16)

    gk = proj(C_GK, C_GV)
    kn = gk * lax.rsqrt(_head_mean_sq(gk, bd_ref[0:GQA_KV_W, 0:GQA_KV_W]) + EPS) * kg_ref[...]
    kr = _rope(kn, cos, sin).astype(BF16)
    hd = HEAD_DIM
    k_ref[...] = jnp.concatenate([kr[:, :hd], kr[:, :hd], kr[:, hd:], kr[:, hd:]], axis=1)
    vv = proj(C_GV, C_SH).astype(BF16)
    ones = jnp.ones((vv.shape[0], hd), BF16)
    v_ref[...] = jnp.concatenate([vv[:, :hd], ones, vv[:, hd:], ones], axis=1)
    ps_ref[...] = proj(C_SH, C_NQ).astype(BF16)
    nq_ref[...] = (proj(C_NQ, C_NK) * Q_SCALE).astype(BF16)
    nk_ref[...] = proj(C_NK, C_NV).astype(BF16)
    nv_ref[...] = proj(C_NV, IN_COLS).astype(BF16)


def _in_proj(xs, mod, g1, w_in, layer, cos_t, sin_t, qg, kg, bd, seg_of_tile):
    nt, d = xs.shape
    tm = TM_PROJ
    row = lambda i: (i, 0)
    const = lambda i: (0, 0)
    widths = (2 * CONV_W, GQA_Q_W, 2 * GQA_KV_W, 2 * GQA_KV_W, 3 * SC_W, NA_W, NA_W, NA_W)
    return pl.pallas_call(
        _in_kernel,
        out_shape=[jax.ShapeDtypeStruct((nt, w), BF16) for w in widths],
        grid=(nt // tm,),
        in_specs=[
            pl.BlockSpec((tm, d), row),
            pl.BlockSpec((1, N_MOD, d), lambda i: (seg_of_tile(i), 0, 0)),
            pl.BlockSpec((1, d), const),
            _resident((None, d, IN_COLS), lambda i: (layer, 0, 0)),
            pl.BlockSpec((tm, LANE), row),
            pl.BlockSpec((tm, LANE), row),
            pl.BlockSpec((1, GQA_Q_W), const),
            pl.BlockSpec((1, GQA_KV_W), const),
            pl.BlockSpec((GQA_Q_W, GQA_Q_W), const),
        ],
        out_specs=[pl.BlockSpec((tm, w), row) for w in widths],
        compiler_params=_cparams(("parallel",)),
        name="in_proj",
    )(xs, mod, g1, w_in, cos_t, sin_t, qg, kg, bd)


def _conv_kernel(seq, n_x_rows, ctx_len,
                 pa_c, pa_p, pa_n, ps_c, ps_p, ps_n, dww_ref, dwb_ref, lng_ref, lnb_ref, scw_ref,
                 oa_ref, oc_ref, ext_ref, ext2_ref, acc_ref):
    tc = TM_CONV
    row0 = pl.program_id(0) * tc
    in_x = row0 < n_x_rows
    rel = jnp.where(in_x, row0, row0 - n_x_rows)
    seg = jnp.where(in_x, seq, ctx_len)
    keep_p = jnp.where(rel % seg == 0, 0.0, 1.0).astype(F32)
    keep_n = jnp.where((rel + tc) % seg == 0, 0.0, 1.0).astype(F32)

    def glu(ref):
        a = ref[...].astype(F32)
        return a[:, :CONV_W] * jax.nn.sigmoid(a[:, CONV_W:])

    ext_ref[0:HALO, :] = glu(pa_p) * keep_p
    ext_ref[HALO:HALO + tc, :] = glu(pa_c)
    ext_ref[HALO + tc:, :] = glu(pa_n) * keep_n

    pad = HALO - CONV_K // 2
    rc = 64
    for c in range(CONV_W // LANE):
        cs = slice(c * LANE, (c + 1) * LANE)
        for r in range(tc // rc):
            acc = jnp.zeros((rc, LANE), F32)
            for k in range(CONV_K):
                lo = r * rc + k + pad
                acc = acc + dww_ref[k:k + 1, cs] * ext_ref[lo:lo + rc, cs]
            acc_ref[r * rc:(r + 1) * rc, cs] = acc + dwb_ref[:, cs]

    u = acc_ref[...]
    mu = jnp.mean(u, axis=-1, keepdims=True)
    var = jnp.mean(jnp.square(u - mu), axis=-1, keepdims=True)
    yn = (u - mu) * lax.rsqrt(var + EPS) * lng_ref[...] + lnb_ref[...]
    oa_ref[...] = (yn * jax.nn.sigmoid(yn)).astype(BF16)

    def gated(ref, lo, hi):
        a = ref[lo:hi, :].astype(F32)
        return a[:, 2 * SC_W:] * a[:, :SC_W]

    ext2_ref[0:8, :] = gated(ps_p, HALO - 8, HALO) * keep_p
    ext2_ref[8:8 + tc, :] = gated(ps_c, 0, tc)
    ext2_ref[8 + tc:, :] = gated(ps_n, 0, 8) * keep_n
    conv = (scw_ref[0:1, :] * ext2_ref[7:7 + tc, :]
            + scw_ref[1:2, :] * ext2_ref[8:8 + tc, :]
            + scw_ref[2:3, :] * ext2_ref[9:9 + tc, :])
    oc_ref[...] = (ps_c[:, SC_W:2 * SC_W].astype(F32) * conv).astype(BF16)


def _conv_mix(pa, ps, dww, dwb, lng, lnb, scw, seq, n_x_rows, ctx_len):
    nt = pa.shape[0]
    tc = TM_CONV
    hb = tc // HALO
    n_hb = nt // HALO
    row = lambda i: (i, 0)
    prev = lambda i: (jnp.maximum(i * hb - 1, 0), 0)
    nxt = lambda i: (jnp.minimum((i + 1) * hb, n_hb - 1), 0)
    const = lambda i: (0, 0)
    wa, ws = pa.shape[1], ps.shape[1]
    return pl.pallas_call(
        functools.partial(_conv_kernel, seq, n_x_rows, ctx_len),
        out_shape=[jax.ShapeDtypeStruct((nt, CONV_W), BF16), jax.ShapeDtypeStruct((nt, SC_W), BF16)],
        grid=(nt // tc,),
        in_specs=[
            pl.BlockSpec((tc, wa), row), pl.BlockSpec((HALO, wa), prev), pl.BlockSpec((HALO, wa), nxt),
            pl.BlockSpec((tc, ws), row), pl.BlockSpec((HALO, ws), prev), pl.BlockSpec((HALO, ws), nxt),
            pl.BlockSpec((CONV_K, CONV_W), const), pl.BlockSpec((1, CONV_W), const),
            pl.BlockSpec((1, CONV_W), const), pl.BlockSpec((1, CONV_W), const),
            pl.BlockSpec((SC_K, SC_W), const),
        ],
        out_specs=[pl.BlockSpec((tc, CONV_W), row), pl.BlockSpec((tc, SC_W), row)],
        scratch_shapes=[pltpu.VMEM((tc + 2 * HALO, CONV_W), F32),
                        pltpu.VMEM((tc + 16, SC_W), F32),
                        pltpu.VMEM((tc, CONV_W), F32)],
        compiler_params=_cparams(("parallel",)),
        name="conv_mix",
    )(pa, pa, pa, ps, ps, ps, dww, dwb, lng, lnb, scw)


def _qk(q, k):
    return lax.dot_general(q, k, (((1,), (1,)), ((), ())), preferred_element_type=F32)


def _softmax_pv(s, v):
    m = jnp.max(s, axis=-1, keepdims=True)
    p = jnp.exp2(s - m)
    l = jnp.sum(p, axis=-1, keepdims=True)
    return jnp.dot(p.astype(BF16), v, preferred_element_type=F32) / l


def _half_masks(rows):
    lane = lax.broadcasted_iota(I32, (rows, LANE), 1)
    lo = lane < HEAD_DIM
    return lo, jnp.logical_not(lo)


def _ctx_attn_kernel(kv_rep, kv_step, q_ref, k_ref, v_ref, o_in_ref, o_ref):
    del o_in_ref
    n_h = q_ref.shape[1] // HEAD_DIM
    outs = []
    for h in range(n_h):
        off = (h // kv_rep) * kv_step
        q = q_ref[:, h * HEAD_DIM:(h + 1) * HEAD_DIM]
        k = k_ref[:, off:off + HEAD_DIM]
        v = v_ref[:, off:off + HEAD_DIM]
        outs.append(_softmax_pv(_qk(q, k), v))
    o_ref[...] = jnp.concatenate(outs, axis=1).astype(BF16)


def _ctx_attn(q, k, v, o_all, n_b, ctx_len, blk0, kv_rep, kv_step):
    kw = k.shape[1]
    qw = q.shape[1]
    rows = lambda b: (blk0 + b, 0)
    return pl.pallas_call(
        functools.partial(_ctx_attn_kernel, kv_rep, kv_step),
        out_shape=jax.ShapeDtypeStruct(o_all.shape, o_all.dtype),
        grid=(n_b,),
        in_specs=[pl.BlockSpec((ctx_len, qw), rows), pl.BlockSpec((ctx_len, kw), rows),
                  pl.BlockSpec((ctx_len, kw), rows), pl.BlockSpec(memory_space=pl.ANY)],
        out_specs=pl.BlockSpec((ctx_len, qw), rows),
        input_output_aliases={3: 0},
        compiler_params=_cparams(("parallel",)),
        name="ctx_attn",
    )(q, k, v, o_all)


def _gqa_kernel(n_chunks, q_ref, kx_ref, vx_ref, kc_ref, vc_ref, o_ref):
    tq = q_ref.shape[0]
    grp = GQA_HEADS // GQA_KV_HEADS
    lo, hi = _half_masks(tq)

    def q_group(g):
        parts = []
        for j in range(grp):
            h = g * grp + j
            blk = q_ref[:, (h // 2) * LANE:(h // 2 + 1) * LANE].astype(F32)
            parts.append(jnp.where(lo if h % 2 == 0 else hi, blk, 0.0).astype(BF16))
        return jnp.concatenate(parts, axis=0)

    def update(q, k, v, m, acc):
        s = _qk(q, k)
        m_new = jnp.maximum(m, jnp.max(s, axis=-1, keepdims=True))
        p = jnp.exp2(s - m_new)
        acc = jnp.exp2(m - m_new) * acc + jnp.dot(p.astype(BF16), v, preferred_element_type=F32)
        return m_new, acc

    qs = [q_group(g) for g in range(GQA_KV_HEADS)]
    gsl = [slice(g * LANE, (g + 1) * LANE) for g in range(GQA_KV_HEADS)]
    state = []
    for g in range(GQA_KV_HEADS):
        s = _qk(qs[g], kc_ref[:, gsl[g]])
        m = jnp.max(s, axis=-1, keepdims=True)
        p = jnp.exp2(s - m)
        state += [m, jnp.dot(p.astype(BF16), vc_ref[:, gsl[g]], preferred_element_type=F32)]

    def body(c, carry):
        rows = pl.ds(pl.multiple_of(c * KC_GQA, KC_GQA), KC_GQA)
        out = []
        for g in range(GQA_KV_HEADS):
            out += update(qs[g], kx_ref[rows, gsl[g]], vx_ref[rows, gsl[g]], carry[2 * g], carry[2 * g + 1])
        return tuple(out)

    state = lax.fori_loop(0, n_chunks, body, tuple(state), unroll=4)

    lo4, _ = _half_masks(grp * tq)
    heads = []
    for g in range(GQA_KV_HEADS):
        acc = state[2 * g + 1]
        inv = 1.0 / jnp.where(lo4, 1.0, acc)
        o = acc * pltpu.roll(inv, HEAD_DIM, 1)
        heads.extend(o[j * tq:(j + 1) * tq] for j in range(grp))
    pairs = [jnp.where(lo, heads[2 * i], pltpu.roll(heads[2 * i + 1], HEAD_DIM, 1))
             for i in range(GQA_HEADS // 2)]
    o_ref[...] = jnp.concatenate(pairs, axis=1).astype(BF16)


def _gqa(q, k, v, n_b, seq, ctx_len):
    nt = q.shape[0]
    tq = TQ_GQA
    nq = seq // tq
    cblk0 = n_b * seq // ctx_len
    kw = k.shape[1]
    return pl.pallas_call(
        functools.partial(_gqa_kernel, seq // KC_GQA),
        out_shape=jax.ShapeDtypeStruct((nt, GQA_Q_W), BF16),
        grid=(n_b, nq),
        in_specs=[
            pl.BlockSpec((tq, GQA_Q_W), lambda b, j: (b * nq + j, 0)),
            pl.BlockSpec((seq, kw), lambda b, j: (b, 0)),
            pl.BlockSpec((seq, kw), lambda b, j: (b, 0)),
            pl.BlockSpec((ctx_len, kw), lambda b, j: (cblk0 + b, 0)),
            pl.BlockSpec((ctx_len, kw), lambda b, j: (cblk0 + b, 0)),
        ],
        out_specs=pl.BlockSpec((tq, GQA_Q_W), lambda b, j: (b * nq + j, 0)),
        compiler_params=_cparams(("parallel", "parallel")),
        name="gqa_flash",
    )(q, k, v, k, v)


def _na_kernel(n_rows, q_ref, k_ref, v_ref, kc_ref, vc_ref, t_ref, o_ref):
    rb = pl.program_id(2)
    win = WIN_ROWS * GRID_W
    tq = q_ref.shape[0]
    lo, hi = _half_masks(tq)
    qf = q_ref[...].astype(F32)
    kc = kc_ref[...]
    vc = vc_ref[...]
    halves = []
    for j in range(LANE // HEAD_DIM):
        qa = jnp.where(lo if j == 0 else hi, qf, 0.0).astype(BF16)
        s_ctx = _qk(qa, kc)
        m_ctx = jnp.max(s_ctx, axis=-1, keepdims=True)
        s_wins, ms, krows_l = [], [], []
        for i in range(NA_RB):
            r = rb * NA_RB + i
            r0 = jnp.clip(r - WIN_ROWS // 2, 0, n_rows - WIN_ROWS)
            shift = r0 - r + (WIN_ROWS - 1)
            krows = pl.ds(pl.multiple_of(r0 * GRID_W, GRID_W), win)
            rs = slice(i * GRID_W, (i + 1) * GRID_W)
            bias = jnp.concatenate([t_ref[j, 2 * wp + shift] for wp in range(WIN_ROWS // 2)], axis=1)
            s_win = _qk(qa[rs], k_ref[krows, :]) + bias
            s_wins.append(s_win)
            ms.append(jnp.maximum(jnp.max(s_win, axis=-1, keepdims=True), m_ctx[rs]))
            krows_l.append(krows)
        m_all = jnp.concatenate(ms, axis=0)
        p_ctx = jnp.exp2(s_ctx - m_all)
        l_ctx = jnp.sum(p_ctx, axis=-1, keepdims=True)
        o_ctx = jnp.dot(p_ctx.astype(BF16), vc, preferred_element_type=F32)
        outs = []
        for i in range(NA_RB):
            rs = slice(i * GRID_W, (i + 1) * GRID_W)
            p = jnp.exp2(s_wins[i] - ms[i])
            l = jnp.sum(p, axis=-1, keepdims=True) + l_ctx[rs]
            o = jnp.dot(p.astype(BF16), v_ref[krows_l[i], :], preferred_element_type=F32) + o_ctx[rs]
            outs.append(o / l)
        halves.append(jnp.concatenate(outs, axis=0))
    o_ref[...] = jnp.where(lo, halves[0], halves[1]).astype(BF16)


def _na(nq, nk, nv, table, n_b, seq, ctx_len):
    nt = nq.shape[0]
    n_rows = seq // GRID_W
    n_rb = n_rows // NA_RB
    tq = NA_RB * GRID_W
    cblk0 = n_b * seq // ctx_len
    hp = LANE // HEAD_DIM
    return pl.pallas_call(
        functools.partial(_na_kernel, n_rows),
        out_shape=jax.ShapeDtypeStruct((nt, NA_W), BF16),
        grid=(n_b, NA_W // LANE, n_rb),
        in_specs=[
            pl.BlockSpec((tq, LANE), lambda b, h, r: (b * n_rb + r, h)),
            pl.BlockSpec((seq, LANE), lambda b, h, r: (b, h)),
            pl.BlockSpec((seq, LANE), lambda b, h, r: (b, h)),
            pl.BlockSpec((ctx_len, LANE), lambda b, h, r: (cblk0 + b, h)),
            pl.BlockSpec((ctx_len, LANE), lambda b, h, r: (cblk0 + b, h)),
            pl.BlockSpec((hp, 2 * WIN_ROWS - 2, GRID_W, LANE), lambda b, h, r: (h, 0, 0, 0)),
        ],
        out_specs=pl.BlockSpec((tq, LANE), lambda b, h, r: (b * n_rb + r, h)),
        compiler_params=_cparams(("parallel", "parallel", "parallel")),
        name="na_attn",
    )(nq, nk, nv, nk, nv, table)


def _na_bias_table(rpb):
    cols = np.arange(GRID_W)
    c0 = np.clip(cols - WIN_COLS // 2, 0, GRID_W - WIN_COLS)
    kc = np.arange(GRID_W)
    rel = kc[None, :] - cols[:, None] + (WIN_COLS - 1)
    valid = (kc[None, :] >= c0[:, None]) & (kc[None, :] < c0[:, None] + WIN_COLS)
    rel = np.clip(rel, 0, 2 * WIN_COLS - 2)
    t = rpb[:, :, rel]
    t = jnp.where(jnp.asarray(valid)[None, None], t * LOG2E, MASK_VALUE).astype(F32)
    return jnp.concatenate([t[:, :-1], t[:, 1:]], axis=-1)


def _out_kernel(oa_ref, ob_ref, oc_ref, od_ref, x_ref, mod_ref, g_ref, w_ref, wr_cat_ref, wr_hi_ref, br_ref,
                x1_ref, h2_ref, lg_ref):
    acc = jnp.dot(oa_ref[...], w_ref[0:CONV_W, :], preferred_element_type=F32)
    acc = acc + jnp.dot(ob_ref[...], w_ref[CONV_W:CONV_W + GQA_Q_W, :], preferred_element_type=F32)
    acc = acc + jnp.dot(oc_ref[...], w_ref[CONV_W + GQA_Q_W:MIX_W - NA_W, :], preferred_element_type=F32)
    acc = acc + jnp.dot(od_ref[...], w_ref[MIX_W - NA_W:MIX_W, :], preferred_element_type=F32)
    x1 = x_ref[...] + mod_ref[0, 2:3, :] * acc
    x1_ref[...] = x1
    ms = jnp.mean(x1 * x1, axis=-1, keepdims=True)
    y = x1 * lax.rsqrt(ms + EPS) * g_ref[...]
    h2 = y * (1 + mod_ref[0, 4:5, :]) + mod_ref[0, 3:4, :]
    h2_ref[...] = h2
    hi, lo = _split_bf16(h2)
    n_e = lg_ref.shape[1]
    a = jnp.dot(hi, wr_cat_ref[...], preferred_element_type=F32)
    b = jnp.dot(lo, wr_hi_ref[...], preferred_element_type=F32)
    lg_ref[...] = a[:, :n_e] + a[:, n_e:] + b + br_ref[...]


def _out_proj(oa, ob, oc, od, xs, mod, g2, w_out, layer, wr_cat, wr_hi, br, seg_of_tile):
    nt, d = xs.shape
    tm = TM_PROJ
    n_e = wr_hi.shape[1]
    row = lambda i: (i, 0)
    const = lambda i: (0, 0)
    return pl.pallas_call(
        _out_kernel,
        out_shape=[jax.ShapeDtypeStruct((nt, d), F32), jax.ShapeDtypeStruct((nt, d), F32),
                   jax.ShapeDtypeStruct((nt, n_e), F32)],
        grid=(nt // tm,),
        in_specs=[
            pl.BlockSpec((tm, CONV_W), row), pl.BlockSpec((tm, GQA_Q_W), row),
            pl.BlockSpec((tm, SC_W), row), pl.BlockSpec((tm, NA_W), row),
            pl.BlockSpec((tm, d), row),
            pl.BlockSpec((1, N_MOD, d), lambda i: (seg_of_tile(i), 0, 0)),
            pl.BlockSpec((1, d), const),
            _resident((None, MIX_W, d), lambda i: (layer, 0, 0)),
            pl.BlockSpec((d, 2 * n_e), const), pl.BlockSpec((d, n_e), const), pl.BlockSpec((1, n_e), const),
        ],
        out_specs=[pl.BlockSpec((tm, d), row), pl.BlockSpec((tm, d), row), pl.BlockSpec((tm, n_e), row)],
        compiler_params=_cparams(("parallel",)),
        name="out_proj",
    )(oa, ob, oc, od, xs, mod, g2, w_out, wr_cat, wr_hi, br)


def _route_kernel(lg_ref, e_ref, pos_ref, gate_ref, cnt_ref, carry_ref):
    i = pl.program_id(0)

    @pl.when(i == 0)
    def _():
        carry_ref[...] = jnp.zeros_like(carry_ref)

    lg = lg_ref[...]
    tr, n_e = lg.shape
    lane = lax.broadcasted_iota(I32, (tr, n_e), 1)
    work = lg
    vals, idxs = [], []
    for _ in range(TOP_K):
        m = jnp.max(work, axis=-1, keepdims=True)
        idx = jnp.min(jnp.where(work == m, lane, n_e), axis=-1, keepdims=True)
        vals.append(m)
        idxs.append(idx)
        work = jnp.where(lane == idx, -jnp.inf, work)
    exps = [jnp.exp(v - vals[0]) for v in vals]
    den = exps[0] + exps[1] + exps[2] + exps[3]

    mask = jnp.zeros((tr, n_e), F32)
    for idx in idxs:
        mask = mask + jnp.where(lane == idx, 1.0, 0.0)
    r_i = lax.broadcasted_iota(I32, (tr, tr), 0)
    c_i = lax.broadcasted_iota(I32, (tr, tr), 1)
    tri = jnp.where(c_i <= r_i, 1.0, 0.0).astype(BF16)
    incl = jnp.dot(tri, mask.astype(BF16), preferred_element_type=F32)
    before = carry_ref[...] + incl - mask

    lane_o = lax.broadcasted_iota(I32, (tr, LANE), 1)
    e_out = jnp.zeros((tr, LANE), I32)
    pos_out = jnp.zeros((tr, LANE), I32)
    gate_out = jnp.zeros((tr, LANE), F32)
    for k in range(TOP_K):
        pos_k = jnp.sum(jnp.where(lane == idxs[k], before, 0.0), axis=-1, keepdims=True)
        e_out = jnp.where(lane_o == k, idxs[k], e_out)
        pos_out = jnp.where(lane_o == k, pos_k.astype(I32), pos_out)
        gate_out = jnp.where(lane_o == k, exps[k] / den, gate_out)
    e_ref[...] = e_out
    pos_ref[...] = pos_out
    gate_ref[...] = gate_out
    total = carry_ref[...] + incl[tr - 1:tr, :]
    carry_ref[...] = total
    cnt_ref[...] = total.astype(I32)


def _route(logits):
    nt, n_e = logits.shape
    tr = TR_ROUTE
    row = lambda i: (i, 0)
    return pl.pallas_call(
        _route_kernel,
        out_shape=[jax.ShapeDtypeStruct((nt, LANE), I32), jax.ShapeDtypeStruct((nt, LANE), I32),
                   jax.ShapeDtypeStruct((nt, LANE), F32), jax.ShapeDtypeStruct((1, n_e), I32)],
        grid=(nt // tr,),
        in_specs=[pl.BlockSpec((tr, n_e), row)],
        out_specs=[pl.BlockSpec((tr, LANE), row), pl.BlockSpec((tr, LANE), row),
                   pl.BlockSpec((tr, LANE), row), pl.BlockSpec((1, n_e), lambda i: (0, 0))],
        scratch_shapes=[pltpu.VMEM((1, n_e), F32)],
        compiler_params=_cparams(("arbitrary",)),
        name="route",
    )(logits)


def _row_copy(src_ref, src_row, dst_ref, dst_row, sem):
    return pltpu.make_async_copy(src_ref.at[pl.ds(src_row, 1)], dst_ref.at[pl.ds(dst_row, 1)], sem)


def _dispatch_kernel(dest_ref, pad_end_ref, h_ref, xs_ref, zbuf_ref, sem, zsem):
    td = h_ref.shape[0]
    blk = zbuf_ref.shape[0]
    n_e = pad_end_ref.shape[0]
    base = pl.program_id(0) * td * TOP_K

    @pl.when(pl.program_id(0) == 0)
    def _():
        zbuf_ref[...] = jnp.zeros_like(zbuf_ref)

        def fill(e):
            start = pl.multiple_of(pad_end_ref[e] - blk, blk)
            return pltpu.make_async_copy(zbuf_ref, xs_ref.at[pl.ds(start, blk)], zsem)

        def nonempty(e):
            prev = pad_end_ref[jnp.maximum(e - 1, 0)]
            return pad_end_ref[e] > jnp.where(e == 0, 0, prev)

        def start(e, _):
            @pl.when(nonempty(e))
            def _():
                fill(e).start()
            return 0

        def wait(e, _):
            @pl.when(nonempty(e))
            def _():
                fill(e).wait()
            return 0

        lax.fori_loop(0, n_e, start, 0)
        lax.fori_loop(0, n_e, wait, 0)

    def issue(t, _):
        for k in range(TOP_K):
            _row_copy(h_ref, t, xs_ref, dest_ref[base + t * TOP_K + k], sem).start()
        return 0

    lax.fori_loop(0, td, issue, 0)
    n_rows = td * TOP_K
    pltpu.make_async_copy(xs_ref.at[pl.ds(0, n_rows)], xs_ref.at[pl.ds(0, n_rows)], sem).wait()


def _dispatch(dest, pad_end, h2, n_slots):
    nt, d = h2.shape
    td = TD_MOE
    return pl.pallas_call(
        _dispatch_kernel,
        out_shape=jax.ShapeDtypeStruct((n_slots, d), F32),
        grid_spec=pltpu.PrefetchScalarGridSpec(
            num_scalar_prefetch=2,
            grid=(nt // td,),
            in_specs=[pl.BlockSpec((td, d), lambda i, dest, pe: (i, 0))],
            out_specs=pl.BlockSpec(memory_space=pl.ANY),
            scratch_shapes=[pltpu.VMEM((BLK_MOE, d), F32), pltpu.SemaphoreType.DMA, pltpu.SemaphoreType.DMA],
        ),
        compiler_params=_cparams(("arbitrary",)),
        name="moe_dispatch",
    )(dest, pad_end, h2)


def _ffn_kernel(blk_e_ref, n_act_ref, x_ref, wgu_ref, bgu_ref, wd_ref, bd_ref, y_ref):
    del blk_e_ref

    @pl.when(pl.program_id(0) < n_act_ref[0])
    def _():
        d_e = wd_ref.shape[1]
        xb = x_ref[...].astype(BF16)
        h = jnp.dot(xb, wgu_ref[0], preferred_element_type=F32) + bgu_ref[0]
        glu = jnp.minimum(h[:, :d_e], SWIGLU_LIMIT)
        lin = jnp.clip(h[:, d_e:], -SWIGLU_LIMIT, SWIGLU_LIMIT)
        act = glu * jax.nn.sigmoid(SWIGLU_ALPHA * glu) * (lin + 1)
        y_ref[...] = jnp.dot(act.astype(BF16), wd_ref[0], preferred_element_type=F32) + bd_ref[0]


def _ffn(blk_e, n_act, xs, wgu, bgu, wd, bd, layer):
    ns, d = xs.shape
    blk = BLK_MOE
    d_e2 = wgu.shape[-1]
    d_e = d_e2 // 2

    def rows(b, be, na):
        return (jnp.minimum(b, na[0] - 1), 0)

    def expert4(b, be, na):
        return (layer, be[jnp.minimum(b, na[0] - 1)], 0, 0)

    return pl.pallas_call(
        _ffn_kernel,
        out_shape=jax.ShapeDtypeStruct((ns, d), F32),
        grid_spec=pltpu.PrefetchScalarGridSpec(
            num_scalar_prefetch=2,
            grid=(ns // blk,),
            in_specs=[
                pl.BlockSpec((blk, d), rows),
                pl.BlockSpec((None, 1, d, d_e2), expert4), pl.BlockSpec((None, 1, 1, d_e2), expert4),
                pl.BlockSpec((None, 1, d_e, d), expert4), pl.BlockSpec((None, 1, 1, d), expert4),
            ],
            out_specs=pl.BlockSpec((blk, d), rows),
        ),
        compiler_params=_cparams(("arbitrary",)),
        name="moe_ffn",
    )(blk_e, n_act, xs, wgu, bgu, wd, bd)


def _combine_kernel(dest_ref, ys_ref, gate_ref, x1_ref, mod_ref, o_ref, buf_ref, sems):
    td = x1_ref.shape[0]
    i = pl.program_id(0)
    slot = i % 2

    def gather(tile, dst_slot):
        base = tile * td * TOP_K

        def issue(t, _):
            for k in range(TOP_K):
                _row_copy(ys_ref, dest_ref[base + t * TOP_K + k], buf_ref.at[dst_slot, k], t,
                          sems.at[dst_slot]).start()
            return 0

        lax.fori_loop(0, td, issue, 0)

    @pl.when(i == 0)
    def _():
        gather(0, 0)

    @pl.when(i + 1 < pl.num_programs(0))
    def _():
        gather(i + 1, 1 - slot)

    pltpu.make_async_copy(buf_ref.at[slot], buf_ref.at[slot], sems.at[slot]).wait()

    g = gate_ref[...]
    y = g[:, 0:1] * buf_ref[slot, 0]
    for k in range(1, TOP_K):
        y = y + g[:, k:k + 1] * buf_ref[slot, k]
    o_ref[...] = x1_ref[...] + mod_ref[0, 5:6, :] * y


def _combine(dest, ys, gate, x1, mod, seg_of_tile):
    nt, d = x1.shape
    td = TD_MOE
    ratio = TM_PROJ // td
    return pl.pallas_call(
        _combine_kernel,
        out_shape=jax.ShapeDtypeStruct((nt, d), F32),
        grid_spec=pltpu.PrefetchScalarGridSpec(
            num_scalar_prefetch=1,
            grid=(nt // td,),
            in_specs=[
                pl.BlockSpec(memory_space=pl.ANY),
                pl.BlockSpec((td, LANE), lambda i, dest: (i, 0)),
                pl.BlockSpec((td, d), lambda i, dest: (i, 0)),
                pl.BlockSpec((1, N_MOD, d), lambda i, dest: (seg_of_tile(i // ratio), 0, 0)),
            ],
            out_specs=pl.BlockSpec((td, d), lambda i, dest: (i, 0)),
            scratch_shapes=[pltpu.VMEM((2, TOP_K, td, d), F32), pltpu.SemaphoreType.DMA((2,))],
        ),
        compiler_params=_cparams(("arbitrary",)),
        name="moe_combine",
    )(dest, ys, gate, x1, mod)


def _final_kernel(x_ref, g_ref, o_ref):
    x = x_ref[...]
    ms = jnp.mean(x * x, axis=-1, keepdims=True)
    o_ref[...] = x * lax.rsqrt(ms + EPS) * g_ref[...]


def _final_norm(xs, g, n_rows):
    d = xs.shape[1]
    tm = TM_PROJ
    return pl.pallas_call(
        _final_kernel,
        out_shape=jax.ShapeDtypeStruct((n_rows, d), F32),
        grid=(n_rows // tm,),
        in_specs=[pl.BlockSpec((tm, d), lambda i: (i, 0)), pl.BlockSpec((1, d), lambda i: (0, 0))],
        out_specs=pl.BlockSpec((tm, d), lambda i: (i, 0)),
        compiler_params=_cparams(("parallel",)),
        name="final_norm",
    )(xs, g)


def _rope_tables(n_b, seq, ctx_len):
    t = np.arange(seq)
    axis_dim = HEAD_DIM // 2
    inv = (ROPE_THETA ** (-np.arange(0, axis_dim, 2, dtype=np.float32) / axis_dim)).astype(np.float32)
    pos = np.stack([t // GRID_W, t % GRID_W], axis=-1).astype(np.float32)
    ang = jnp.asarray(pos[:, :, None] * inv[None, None, :])
    cos, sin = jnp.cos(ang), jnp.sin(ang)
    cos_h = jnp.concatenate([cos, cos], axis=-1).reshape(seq, HEAD_DIM)
    sin_h = jnp.concatenate([-sin, sin], axis=-1).reshape(seq, HEAD_DIM)
    cos_x = jnp.tile(cos_h, (n_b, LANE // HEAD_DIM))
    sin_x = jnp.tile(sin_h, (n_b, LANE // HEAD_DIM))
    ones = jnp.ones((n_b * ctx_len, LANE), F32)
    return (jnp.concatenate([cos_x, ones], axis=0),
            jnp.concatenate([sin_x, jnp.zeros_like(ones)], axis=0))


def kernel(x, c, ctx, c_ctx, w_ada, b_ada, norm1_g, norm2_g, w_in, w_out, conv_dw_w, conv_dw_b, conv_ln_g,
           conv_ln_b, q_norm_g, k_norm_g, sc_conv_w, na_rpb, w_router, b_router, w_gate_up, b_gate_up, w_down,
           b_down, final_norm_g):
    n_b, seq, d = x.shape
    ctx_len = ctx.shape[1]
    depth = w_ada.shape[0]
    n_e = w_router.shape[2]
    d_e = w_down.shape[2]
    n_x_rows = n_b * seq
    nt = n_x_rows + n_b * ctx_len
    assert d == D_MODEL and seq % GRID_W == 0 and (seq // GRID_W) % NA_RB == 0
    assert seq % TM_PROJ == 0 and (n_b * ctx_len) % TM_PROJ == 0
    assert seq % TM_CONV == 0 and ctx_len % TM_CONV == 0 and seq % ctx_len == 0
    assert seq % KC_GQA == 0 and nt % TD_MOE == 0 and nt % TR_ROUTE == 0 and n_b + 1 <= 8

    tiles_x = n_x_rows // TM_PROJ
    tiles_per_b = seq // TM_PROJ

    def seg_of_tile(i):
        return jnp.where(i < tiles_x, 1 + i // tiles_per_b, 0)

    w_in_b = w_in.astype(BF16)
    w_out_b = w_out.astype(BF16)
    wgu_b = _deinterleave_cast(w_gate_up)
    bgu_p = jnp.concatenate([b_gate_up[..., 0::2], b_gate_up[..., 1::2]], axis=-1)[:, :, None, :]
    wd_b = w_down.astype(BF16)
    bd_p = b_down[:, :, None, :]
    wr_hi = w_router.astype(BF16)
    wr_lo = (w_router - wr_hi.astype(F32)).astype(BF16)
    wr_cat = jnp.concatenate([wr_hi, wr_lo], axis=-1)
    bd = jnp.asarray(np.kron(np.eye(GQA_HEADS, dtype=np.float32),
                             np.full((HEAD_DIM, HEAD_DIM), 1.0 / HEAD_DIM, np.float32))).astype(BF16)
    cos_t, sin_t = _rope_tables(n_b, seq, ctx_len)

    cvec = jnp.zeros((8, d), F32).at[0].set(c_ctx).at[1:1 + n_b].set(c)
    mods = _ada_all(cvec, w_ada, b_ada).reshape(depth, 8, N_MOD, d)

    n_assign = nt * TOP_K
    n_blk = -(-n_assign // BLK_MOE) + n_e

    xs = jnp.concatenate([x.reshape(n_x_rows, d), ctx.reshape(n_b * ctx_len, d)], axis=0)
    cblk0 = n_x_rows // ctx_len

    for l in range(depth):
        mod = mods[l]
        pa, q, k, v, ps, nq, nk, nv = _in_proj(
            xs, mod, norm1_g[l][None], w_in_b, l, cos_t, sin_t,
            jnp.tile(q_norm_g[l], GQA_HEADS)[None], jnp.tile(k_norm_g[l], GQA_KV_HEADS)[None], bd, seg_of_tile)
        oa, oc = _conv_mix(pa, ps, conv_dw_w[l], conv_dw_b[l][None], conv_ln_g[l][None], conv_ln_b[l][None],
                           sc_conv_w[l], seq, n_x_rows, ctx_len)
        ob = _gqa(q, k, v, n_b, seq, ctx_len)
        ob = _ctx_attn(q, k, v, ob, n_b, ctx_len, cblk0, GQA_HEADS // GQA_KV_HEADS, LANE)
        od = _na(nq, nk, nv, _na_bias_table(na_rpb[l]), n_b, seq, ctx_len)
        od = _ctx_attn(nq, nk, nv, od, n_b, ctx_len, cblk0, 1, HEAD_DIM)
        x1, h2, logits = _out_proj(oa, ob, oc, od, xs, mod, norm2_g[l][None], w_out_b, l,
                                   wr_cat[l], wr_hi[l], b_router[l][None], seg_of_tile)

        e_o, pos_o, gate_o, cnt = _route(logits)
        counts = cnt[0]
        padded = (counts + BLK_MOE - 1) // BLK_MOE * BLK_MOE
        pad_end = jnp.cumsum(padded)
        pad_start = pad_end - padded
        dest = (pad_start[e_o[:, :TOP_K]] + pos_o[:, :TOP_K]).reshape(-1).astype(I32)
        blk_lo = jnp.arange(n_blk, dtype=I32) * BLK_MOE
        blk_e = jnp.minimum(jnp.sum((pad_end[None, :] <= blk_lo[:, None]).astype(I32), axis=1), n_e - 1)
        n_act = (pad_end[-1:] // BLK_MOE).astype(I32)

        x_sorted = _dispatch(dest, pad_end.astype(I32), h2, n_blk * BLK_MOE)
        ys = _ffn(blk_e, n_act, x_sorted, wgu_b, bgu_p, wd_b, bd_p, l)
        xs = _combine(dest, ys, gate_o, x1, mod, seg_of_tile)

    out = _final_norm(xs, final_norm_g[None], n_x_rows)
    return out.reshape(n_b, seq, d)
```

```python
import functools

import jax
import jax.numpy as jnp
import numpy as np
from jax import lax
from jax.experimental import pallas as pl
from jax.experimental.pallas import tpu as pltpu

F32 = jnp.float32
BF16 = jnp.bfloat16
I32 = jnp.int32

D_MODEL = 2048
GRID_W = 64
HEAD_DIM = 64
CONV_W = 512
GQA_HEADS = 8
GQA_KV_HEADS = 2
SC_W = 512
NA_HEADS = 8
GQA_Q_W = GQA_HEADS * HEAD_DIM
GQA_KV_W = GQA_KV_HEADS * HEAD_DIM
NA_W = NA_HEADS * HEAD_DIM
MIX_W = CONV_W + GQA_Q_W + SC_W + NA_W
IN_COLS = 2 * CONV_W + GQA_Q_W + 2 * GQA_KV_W + 3 * SC_W + 3 * NA_W
CONV_K = 31
SC_K = 3
WIN_ROWS = 8
WIN_COLS = 16
ROPE_THETA = 10000.0
ATTN_SCALE = HEAD_DIM ** -0.5
TOP_K = 4
SWIGLU_LIMIT = 7.0
SWIGLU_ALPHA = 1.702
N_MOD = 6
EPS = 1e-6
MASK_VALUE = -1e30
LOG2E = 1.4426950408889634
Q_SCALE = ATTN_SCALE * LOG2E

C_AV, C_GQ = 0, 2 * CONV_W
C_GK = C_GQ + GQA_Q_W
C_GV = C_GK + GQA_KV_W
C_SH = C_GV + GQA_KV_W
C_NQ = C_SH + 3 * SC_W
C_NK = C_NQ + NA_W
C_NV = C_NK + NA_W

VMEM_LIMIT = 56 * 1024 * 1024
LANE = 128

TM_PROJ = 512
TM_CONV = 256
HALO = 16
TQ_GQA = 256
KC_GQA = 512
NA_RB = 8
TD_MOE = 256
BLK_MOE = 256
DMA_UNROLL = 4


def _cparams(sem):
    return pltpu.CompilerParams(dimension_semantics=sem, vmem_limit_bytes=VMEM_LIMIT)


def _resident(shape, index_map):
    return pl.BlockSpec(shape, index_map, pipeline_mode=pl.Buffered(1))


def _split_bf16(a):
    hi = a.astype(BF16)
    lo = (a - hi.astype(F32)).astype(BF16)
    return hi, lo


PREP_COLS = 256
PREP_ROWS = 1024


def _deint_kernel(w_ref, p_ref, o_ref):
    n = w_ref.shape[1]
    half = PREP_COLS // 2
    for b in range(n // PREP_COLS):
        t = jnp.dot(w_ref[:, b * PREP_COLS:(b + 1) * PREP_COLS].astype(BF16), p_ref[...],
                    preferred_element_type=F32).astype(BF16)
        o_ref[:, b * half:(b + 1) * half] = t[:, :half]
        o_ref[:, n // 2 + b * half:n // 2 + (b + 1) * half] = t[:, half:]


def _deinterleave_cast(w):
    shape = w.shape
    n = shape[-1]
    rows = int(np.prod(shape[:-1]))
    assert rows % PREP_ROWS == 0 and n % PREP_COLS == 0
    perm = np.zeros((PREP_COLS, PREP_COLS), np.float32)
    j = np.arange(PREP_COLS // 2)
    perm[2 * j, j] = 1.0
    perm[2 * j + 1, PREP_COLS // 2 + j] = 1.0
    out = pl.pallas_call(
        _deint_kernel,
        out_shape=jax.ShapeDtypeStruct((rows, n), BF16),
        grid=(rows // PREP_ROWS,),
        in_specs=[pl.BlockSpec((PREP_ROWS, n), lambda i: (i, 0)),
                  pl.BlockSpec((PREP_COLS, PREP_COLS), lambda i: (0, 0))],
        out_specs=pl.BlockSpec((PREP_ROWS, n), lambda i: (i, 0)),
        compiler_params=_cparams(("parallel",)),
        name="wgu_prep",
    )(w.reshape(rows, n), jnp.asarray(perm, BF16))
    return out.reshape(shape)


def _ada_kernel(c_ref, w_ref, b_ref, o_ref):
    c = c_ref[...]
    s = c * jax.nn.sigmoid(c)
    hi, lo = _split_bf16(s)
    lhs = jnp.concatenate([hi, lo], axis=0)
    r = jnp.dot(lhs, w_ref[0].astype(BF16), preferred_element_type=F32)
    o_ref[0] = r[:8] + r[8:] + b_ref[0]


def _ada_all(cvec, w_ada, b_ada):
    n_l, d, n6 = w_ada.shape
    tn = 1024
    return pl.pallas_call(
        _ada_kernel,
        out_shape=jax.ShapeDtypeStruct((n_l, 8, n6), F32),
        grid=(n_l, n6 // tn),
        in_specs=[
            pl.BlockSpec((8, d), lambda l, j: (0, 0)),
            pl.BlockSpec((1, d, tn), lambda l, j: (l, 0, j)),
            pl.BlockSpec((1, 1, tn), lambda l, j: (l, 0, j)),
        ],
        out_specs=pl.BlockSpec((1, 8, tn), lambda l, j: (l, 0, j)),
        compiler_params=_cparams(("parallel", "parallel")),
        name="ada_mod",
    )(cvec, w_ada, b_ada.reshape(n_l, 1, n6))


def _rope(x, cos, sin_signed):
    n = x.shape[1]
    lane = lax.broadcasted_iota(I32, x.shape, 1)
    first = (lane % 32) < 16
    partner = jnp.where(first, pltpu.roll(x, n - 16, 1), pltpu.roll(x, 16, 1))
    return x * cos + partner * sin_signed


def _head_mean_sq(y, bd):
    hi, lo = _split_bf16(y * y)
    return (jnp.dot(hi, bd, preferred_element_type=F32)
            + jnp.dot(lo, bd, preferred_element_type=F32))


def _in_kernel(x_ref, mod_ref, g_ref, w_ref, cos_ref, sin_ref, qg_ref, kg_ref, bd_ref,
               pa_ref, q_ref, k_ref, v_ref, ps_ref, nq_ref, nk_ref, nv_ref):
    x = x_ref[...]
    ms = jnp.mean(x * x, axis=-1, keepdims=True)
    y = x * lax.rsqrt(ms + EPS) * g_ref[...]
    shift = mod_ref[0, 0:1, :]
    scale = mod_ref[0, 1:2, :]
    h = (y * (1 + scale) + shift).astype(BF16)

    def proj(lo, hi):
        return jnp.dot(h, w_ref[:, lo:hi], preferred_element_type=F32)

    pa_ref[...] = proj(C_AV, C_GQ).astype(BF16)

    cos = cos_ref[...]
    sin = sin_ref[...]
    gq = proj(C_GQ, C_GK)
    qn = gq * lax.rsqrt(_head_mean_sq(gq, bd_ref[...]) + EPS) * qg_ref[...]
    cos4 = jnp.concatenate([cos] * 4, axis=1)
    sin4 = jnp.concatenate([sin] * 4, axis=1)
    q_ref[...] = (_rope(qn, cos4, sin4) * Q_SCALE).astype(BF16)

    gk = proj(C_GK, C_GV)
    kn = gk * lax.rsqrt(_head_mean_sq(gk, bd_ref[0:GQA_KV_W, 0:GQA_KV_W]) + EPS) * kg_ref[...]
    kr = _rope(kn, cos, sin).astype(BF16)
    hd = HEAD_DIM
    k_ref[...] = jnp.concatenate([kr[:, :hd], kr[:, :hd], kr[:, hd:], kr[:, hd:]], axis=1)
    vv = proj(C_GV, C_SH).astype(BF16)
    ones = jnp.ones((vv.shape[0], hd), BF16)
    v_ref[...] = jnp.concatenate([vv[:, :hd], ones, vv[:, hd:], ones], axis=1)
    ps_ref[...] = proj(C_SH, C_NQ).astype(BF16)
    nq_ref[...] = (proj(C_NQ, C_NK) * Q_SCALE).astype(BF16)
    nk_ref[...] = proj(C_NK, C_NV).astype(BF16)
    nv_ref[...] = proj(C_NV, IN_COLS).astype(BF16)


def _in_proj(xs, mod, g1, w_in, layer, cos_t, sin_t, qg, kg, bd, seg_of_tile):
    nt, d = xs.shape
    tm = TM_PROJ
    row = lambda i: (i, 0)
    const = lambda i: (0, 0)
    widths = (2 * CONV_W, GQA_Q_W, 2 * GQA_KV_W, 2 * GQA_KV_W, 3 * SC_W, NA_W, NA_W, NA_W)
    return pl.pallas_call(
        _in_kernel,
        out_shape=[jax.ShapeDtypeStruct((nt, w), BF16) for w in widths],
        grid=(nt // tm,),
        in_specs=[
            pl.BlockSpec((tm, d), row),
            pl.BlockSpec((1, N_MOD, d), lambda i: (seg_of_tile(i), 0, 0)),
            pl.BlockSpec((1, d), const),
            _resident((None, d, IN_COLS), lambda i: (layer, 0, 0)),
            pl.BlockSpec((tm, LANE), row),
            pl.BlockSpec((tm, LANE), row),
            pl.BlockSpec((1, GQA_Q_W), const),
            pl.BlockSpec((1, GQA_KV_W), const),
            pl.BlockSpec((GQA_Q_W, GQA_Q_W), const),
        ],
        out_specs=[pl.BlockSpec((tm, w), row) for w in widths],
        compiler_params=_cparams(("parallel",)),
        name="in_proj",
    )(xs, mod, g1, w_in, cos_t, sin_t, qg, kg, bd)


def _conv_kernel(seq, n_x_rows, ctx_len,
                 pa_c, pa_p, pa_n, ps_c, ps_p, ps_n, dww_ref, dwb_ref, lng_ref, lnb_ref, scw_ref,
                 oa_ref, oc_ref, ext_ref, ext2_ref, acc_ref, sh_ref):
    tc = TM_CONV
    row0 = pl.program_id(0) * tc
    in_x = row0 < n_x_rows
    rel = jnp.where(in_x, row0, row0 - n_x_rows)
    seg = jnp.where(in_x, seq, ctx_len)
    keep_p = jnp.where(rel % seg == 0, 0.0, 1.0).astype(F32)
    keep_n = jnp.where((rel + tc) % seg == 0, 0.0, 1.0).astype(F32)

    def glu(ref):
        a = ref[...].astype(F32)
        return a[:, :CONV_W] * jax.nn.sigmoid(a[:, CONV_W:])

    ext_ref[0:HALO, :] = glu(pa_p) * keep_p
    ext_ref[HALO:HALO + tc, :] = glu(pa_c)
    ext_ref[HALO + tc:, :] = glu(pa_n) * keep_n

    pad = HALO - CONV_K // 2
    n_sh = sh_ref.shape[1]
    for b in range(8):
        sh_ref[b] = ext_ref[b:b + n_sh, :]
    rc = 64
    for c in range(CONV_W // LANE):
        cs = slice(c * LANE, (c + 1) * LANE)
        for r in range(tc // rc):
            acc = jnp.zeros((rc, LANE), F32)
            for k in range(CONV_K):
                a, b = divmod(k + pad, 8)
                lo = r * rc + 8 * a
                acc = acc + dww_ref[k:k + 1, cs] * sh_ref[b, lo:lo + rc, cs]
            acc_ref[r * rc:(r + 1) * rc, cs] = acc + dwb_ref[:, cs]

    u = acc_ref[...]
    mu = jnp.mean(u, axis=-1, keepdims=True)
    var = jnp.mean(jnp.square(u - mu), axis=-1, keepdims=True)
    yn = (u - mu) * lax.rsqrt(var + EPS) * lng_ref[...] + lnb_ref[...]
    oa_ref[...] = (yn * jax.nn.sigmoid(yn)).astype(BF16)

    def gated(ref, lo, hi):
        a = ref[lo:hi, :].astype(F32)
        return a[:, 2 * SC_W:] * a[:, :SC_W]

    ext2_ref[0:8, :] = gated(ps_p, HALO - 8, HALO) * keep_p
    ext2_ref[8:8 + tc, :] = gated(ps_c, 0, tc)
    ext2_ref[8 + tc:, :] = gated(ps_n, 0, 8) * keep_n
    conv = (scw_ref[0:1, :] * ext2_ref[7:7 + tc, :]
            + scw_ref[1:2, :] * ext2_ref[8:8 + tc, :]
            + scw_ref[2:3, :] * ext2_ref[9:9 + tc, :])
    oc_ref[...] = (ps_c[:, SC_W:2 * SC_W].astype(F32) * conv).astype(BF16)


def _conv_mix(pa, ps, dww, dwb, lng, lnb, scw, seq, n_x_rows, ctx_len):
    nt = pa.shape[0]
    tc = TM_CONV
    hb = tc // HALO
    n_hb = nt // HALO
    row = lambda i: (i, 0)
    prev = lambda i: (jnp.maximum(i * hb - 1, 0), 0)
    nxt = lambda i: (jnp.minimum((i + 1) * hb, n_hb - 1), 0)
    const = lambda i: (0, 0)
    wa, ws = pa.shape[1], ps.shape[1]
    return pl.pallas_call(
        functools.partial(_conv_kernel, seq, n_x_rows, ctx_len),
        out_shape=[jax.ShapeDtypeStruct((nt, CONV_W), BF16), jax.ShapeDtypeStruct((nt, SC_W), BF16)],
        grid=(nt // tc,),
        in_specs=[
            pl.BlockSpec((tc, wa), row), pl.BlockSpec((HALO, wa), prev), pl.BlockSpec((HALO, wa), nxt),
            pl.BlockSpec((tc, ws), row), pl.BlockSpec((HALO, ws), prev), pl.BlockSpec((HALO, ws), nxt),
            pl.BlockSpec((CONV_K, CONV_W), const), pl.BlockSpec((1, CONV_W), const),
            pl.BlockSpec((1, CONV_W), const), pl.BlockSpec((1, CONV_W), const),
            pl.BlockSpec((SC_K, SC_W), const),
        ],
        out_specs=[pl.BlockSpec((tc, CONV_W), row), pl.BlockSpec((tc, SC_W), row)],
        scratch_shapes=[pltpu.VMEM((tc + 2 * HALO, CONV_W), F32),
                        pltpu.VMEM((tc + 16, SC_W), F32),
                        pltpu.VMEM((tc, CONV_W), F32),
                        pltpu.VMEM((8, tc + 2 * HALO - 8, CONV_W), F32)],
        compiler_params=_cparams(("parallel",)),
        name="conv_mix",
    )(pa, pa, pa, ps, ps, ps, dww, dwb, lng, lnb, scw)


def _qk(q, k):
    return lax.dot_general(q, k, (((1,), (1,)), ((), ())), preferred_element_type=F32)


def _softmax_pv(s, v):
    m = jnp.max(s, axis=-1, keepdims=True)
    p = jnp.exp2(s - m)
    l = jnp.sum(p, axis=-1, keepdims=True)
    return jnp.dot(p.astype(BF16), v, preferred_element_type=F32) / l


def _half_masks(rows):
    lane = lax.broadcasted_iota(I32, (rows, LANE), 1)
    lo = lane < HEAD_DIM
    return lo, jnp.logical_not(lo)


def _ctx_attn_kernel(kv_rep, kv_step, q_ref, k_ref, v_ref, o_in_ref, o_ref):
    del o_in_ref
    n_h = q_ref.shape[1] // HEAD_DIM
    outs = []
    for h in range(n_h):
        off = (h // kv_rep) * kv_step
        q = q_ref[:, h * HEAD_DIM:(h + 1) * HEAD_DIM]
        k = k_ref[:, off:off + HEAD_DIM]
        v = v_ref[:, off:off + HEAD_DIM]
        outs.append(_softmax_pv(_qk(q, k), v))
    o_ref[...] = jnp.concatenate(outs, axis=1).astype(BF16)


def _ctx_attn(q, k, v, o_all, n_b, ctx_len, blk0, kv_rep, kv_step):
    kw = k.shape[1]
    qw = q.shape[1]
    rows = lambda b: (blk0 + b, 0)
    return pl.pallas_call(
        functools.partial(_ctx_attn_kernel, kv_rep, kv_step),
        out_shape=jax.ShapeDtypeStruct(o_all.shape, o_all.dtype),
        grid=(n_b,),
        in_specs=[pl.BlockSpec((ctx_len, qw), rows), pl.BlockSpec((ctx_len, kw), rows),
                  pl.BlockSpec((ctx_len, kw), rows), pl.BlockSpec(memory_space=pl.ANY)],
        out_specs=pl.BlockSpec((ctx_len, qw), rows),
        input_output_aliases={3: 0},
        compiler_params=_cparams(("parallel",)),
        name="ctx_attn",
    )(q, k, v, o_all)


def _gqa_kernel(n_chunks, q_ref, kx_ref, vx_ref, kc_ref, vc_ref, o_ref):
    tq = q_ref.shape[0]
    grp = GQA_HEADS // GQA_KV_HEADS
    lo, hi = _half_masks(tq)

    def q_group(g):
        parts = []
        for j in range(grp):
            h = g * grp + j
            blk = q_ref[:, (h // 2) * LANE:(h // 2 + 1) * LANE].astype(F32)
            parts.append(jnp.where(lo if h % 2 == 0 else hi, blk, 0.0).astype(BF16))
        return jnp.concatenate(parts, axis=0)

    def update(q, k, v, m, acc):
        s = _qk(q, k)
        m_new = jnp.maximum(m, jnp.max(s, axis=-1, keepdims=True))
        p = jnp.exp2(s - m_new)
        acc = jnp.exp2(m - m_new) * acc + jnp.dot(p.astype(BF16), v, preferred_element_type=F32)
        return m_new, acc

    qs = [q_group(g) for g in range(GQA_KV_HEADS)]
    gsl = [slice(g * LANE, (g + 1) * LANE) for g in range(GQA_KV_HEADS)]
    state = []
    for g in range(GQA_KV_HEADS):
        s = _qk(qs[g], kc_ref[:, gsl[g]])
        m = jnp.max(s, axis=-1, keepdims=True)
        p = jnp.exp2(s - m)
        state += [m, jnp.dot(p.astype(BF16), vc_ref[:, gsl[g]], preferred_element_type=F32)]

    def body(c, carry):
        rows = pl.ds(pl.multiple_of(c * KC_GQA, KC_GQA), KC_GQA)
        out = []
        for g in range(GQA_KV_HEADS):
            out += update(qs[g], kx_ref[rows, gsl[g]], vx_ref[rows, gsl[g]], carry[2 * g], carry[2 * g + 1])
        return tuple(out)

    state = lax.fori_loop(0, n_chunks, body, tuple(state), unroll=4)

    lo4, _ = _half_masks(grp * tq)
    heads = []
    for g in range(GQA_KV_HEADS):
        acc = state[2 * g + 1]
        inv = 1.0 / jnp.where(lo4, 1.0, acc)
        o = acc * pltpu.roll(inv, HEAD_DIM, 1)
        heads.extend(o[j * tq:(j + 1) * tq] for j in range(grp))
    pairs = [jnp.where(lo, heads[2 * i], pltpu.roll(heads[2 * i + 1], HEAD_DIM, 1))
             for i in range(GQA_HEADS // 2)]
    o_ref[...] = jnp.concatenate(pairs, axis=1).astype(BF16)


def _gqa(q, k, v, n_b, seq, ctx_len):
    nt = q.shape[0]
    tq = TQ_GQA
    nq = seq // tq
    cblk0 = n_b * seq // ctx_len
    kw = k.shape[1]
    return pl.pallas_call(
        functools.partial(_gqa_kernel, seq // KC_GQA),
        out_shape=jax.ShapeDtypeStruct((nt, GQA_Q_W), BF16),
        grid=(n_b, nq),
        in_specs=[
            pl.BlockSpec((tq, GQA_Q_W), lambda b, j: (b * nq + j, 0)),
            pl.BlockSpec((seq, kw), lambda b, j: (b, 0)),
            pl.BlockSpec((seq, kw), lambda b, j: (b, 0)),
            pl.BlockSpec((ctx_len, kw), lambda b, j: (cblk0 + b, 0)),
            pl.BlockSpec((ctx_len, kw), lambda b, j: (cblk0 + b, 0)),
        ],
        out_specs=pl.BlockSpec((tq, GQA_Q_W), lambda b, j: (b * nq + j, 0)),
        compiler_params=_cparams(("parallel", "parallel")),
        name="gqa_flash",
    )(q, k, v, k, v)


def _na_kernel(n_rows, q_ref, k_ref, v_ref, kc_ref, vc_ref, t_ref, o_ref):
    rb = pl.program_id(2)
    win = WIN_ROWS * GRID_W
    tq = q_ref.shape[0]
    lo, hi = _half_masks(tq)
    qf = q_ref[...].astype(F32)
    kc = kc_ref[...]
    vc = vc_ref[...]
    halves = []
    for j in range(LANE // HEAD_DIM):
        qa = jnp.where(lo if j == 0 else hi, qf, 0.0).astype(BF16)
        s_ctx = _qk(qa, kc)
        m_ctx = jnp.max(s_ctx, axis=-1, keepdims=True)
        s_wins, ms, krows_l = [], [], []
        for i in range(NA_RB):
            r = rb * NA_RB + i
            r0 = jnp.clip(r - WIN_ROWS // 2, 0, n_rows - WIN_ROWS)
            shift = r0 - r + (WIN_ROWS - 1)
            krows = pl.ds(pl.multiple_of(r0 * GRID_W, GRID_W), win)
            rs = slice(i * GRID_W, (i + 1) * GRID_W)
            bias = jnp.concatenate([t_ref[j, 2 * wp + shift] for wp in range(WIN_ROWS // 2)], axis=1)
            s_win = _qk(qa[rs], k_ref[krows, :]) + bias
            s_wins.append(s_win)
            ms.append(jnp.maximum(jnp.max(s_win, axis=-1, keepdims=True), m_ctx[rs]))
            krows_l.append(krows)
        m_all = jnp.concatenate(ms, axis=0)
        p_ctx = jnp.exp2(s_ctx - m_all)
        l_ctx = jnp.sum(p_ctx, axis=-1, keepdims=True)
        o_ctx = jnp.dot(p_ctx.astype(BF16), vc, preferred_element_type=F32)
        outs = []
        for i in range(NA_RB):
            rs = slice(i * GRID_W, (i + 1) * GRID_W)
            p = jnp.exp2(s_wins[i] - ms[i])
            l = jnp.sum(p, axis=-1, keepdims=True) + l_ctx[rs]
            o = jnp.dot(p.astype(BF16), v_ref[krows_l[i], :], preferred_element_type=F32) + o_ctx[rs]
            outs.append(o / l)
        halves.append(jnp.concatenate(outs, axis=0))
    o_ref[...] = jnp.where(lo, halves[0], halves[1]).astype(BF16)


def _na(nq, nk, nv, table, n_b, seq, ctx_len):
    nt = nq.shape[0]
    n_rows = seq // GRID_W
    n_rb = n_rows // NA_RB
    tq = NA_RB * GRID_W
    cblk0 = n_b * seq // ctx_len
    hp = LANE // HEAD_DIM
    return pl.pallas_call(
        functools.partial(_na_kernel, n_rows),
        out_shape=jax.ShapeDtypeStruct((nt, NA_W), BF16),
        grid=(n_b, NA_W // LANE, n_rb),
        in_specs=[
            pl.BlockSpec((tq, LANE), lambda b, h, r: (b * n_rb + r, h)),
            pl.BlockSpec((seq, LANE), lambda b, h, r: (b, h)),
            pl.BlockSpec((seq, LANE), lambda b, h, r: (b, h)),
            pl.BlockSpec((ctx_len, LANE), lambda b, h, r: (cblk0 + b, h)),
            pl.BlockSpec((ctx_len, LANE), lambda b, h, r: (cblk0 + b, h)),
            pl.BlockSpec((hp, 2 * WIN_ROWS - 2, GRID_W, LANE), lambda b, h, r: (h, 0, 0, 0)),
        ],
        out_specs=pl.BlockSpec((tq, LANE), lambda b, h, r: (b * n_rb + r, h)),
        compiler_params=_cparams(("parallel", "parallel", "parallel")),
        name="na_attn",
    )(nq, nk, nv, nk, nv, table)


def _na_bias_table(rpb):
    cols = np.arange(GRID_W)
    c0 = np.clip(cols - WIN_COLS // 2, 0, GRID_W - WIN_COLS)
    kc = np.arange(GRID_W)
    rel = kc[None, :] - cols[:, None] + (WIN_COLS - 1)
    valid = (kc[None, :] >= c0[:, None]) & (kc[None, :] < c0[:, None] + WIN_COLS)
    rel = np.clip(rel, 0, 2 * WIN_COLS - 2)
    t = rpb[:, :, rel]
    t = jnp.where(jnp.asarray(valid)[None, None], t * LOG2E, MASK_VALUE).astype(F32)
    return jnp.concatenate([t[:, :-1], t[:, 1:]], axis=-1)


def _out_kernel(oa_ref, ob_ref, oc_ref, od_ref, x_ref, mod_ref, g_ref, w_ref, wr_cat_ref, wr_hi_ref, br_ref,
                x1_ref, h2_ref, e_ref, pos_ref, gate_ref, cnt_ref, carry_ref):
    acc = jnp.dot(oa_ref[...], w_ref[0:CONV_W, :], preferred_element_type=F32)
    acc = acc + jnp.dot(ob_ref[...], w_ref[CONV_W:CONV_W + GQA_Q_W, :], preferred_element_type=F32)
    acc = acc + jnp.dot(oc_ref[...], w_ref[CONV_W + GQA_Q_W:MIX_W - NA_W, :], preferred_element_type=F32)
    acc = acc + jnp.dot(od_ref[...], w_ref[MIX_W - NA_W:MIX_W, :], preferred_element_type=F32)
    x1 = x_ref[...] + mod_ref[0, 2:3, :] * acc
    x1_ref[...] = x1
    ms = jnp.mean(x1 * x1, axis=-1, keepdims=True)
    y = x1 * lax.rsqrt(ms + EPS) * g_ref[...]
    h2 = y * (1 + mod_ref[0, 4:5, :]) + mod_ref[0, 3:4, :]
    h2_ref[...] = h2
    hi, lo = _split_bf16(h2)
    n_e = br_ref.shape[1]
    a = jnp.dot(hi, wr_cat_ref[...], preferred_element_type=F32)
    b = jnp.dot(lo, wr_hi_ref[...], preferred_element_type=F32)
    logits = a[:, :n_e] + a[:, n_e:] + b + br_ref[...]
    _route_tile(logits, e_ref, pos_ref, gate_ref, cnt_ref, carry_ref)


def _out_proj(oa, ob, oc, od, xs, mod, g2, w_out, layer, wr_cat, wr_hi, br, seg_of_tile):
    nt, d = xs.shape
    tm = TM_PROJ
    n_e = wr_hi.shape[1]
    row = lambda i: (i, 0)
    const = lambda i: (0, 0)
    return pl.pallas_call(
        _out_kernel,
        out_shape=[jax.ShapeDtypeStruct((nt, d), F32), jax.ShapeDtypeStruct((nt, d), F32),
                   jax.ShapeDtypeStruct((nt, LANE), I32), jax.ShapeDtypeStruct((nt, LANE), I32),
                   jax.ShapeDtypeStruct((nt, LANE), F32), jax.ShapeDtypeStruct((1, n_e), I32)],
        grid=(nt // tm,),
        in_specs=[
            pl.BlockSpec((tm, CONV_W), row), pl.BlockSpec((tm, GQA_Q_W), row),
            pl.BlockSpec((tm, SC_W), row), pl.BlockSpec((tm, NA_W), row),
            pl.BlockSpec((tm, d), row),
            pl.BlockSpec((1, N_MOD, d), lambda i: (seg_of_tile(i), 0, 0)),
            pl.BlockSpec((1, d), const),
            _resident((None, MIX_W, d), lambda i: (layer, 0, 0)),
            pl.BlockSpec((d, 2 * n_e), const), pl.BlockSpec((d, n_e), const), pl.BlockSpec((1, n_e), const),
        ],
        out_specs=[pl.BlockSpec((tm, d), row), pl.BlockSpec((tm, d), row),
                   pl.BlockSpec((tm, LANE), row), pl.BlockSpec((tm, LANE), row), pl.BlockSpec((tm, LANE), row),
                   pl.BlockSpec((1, n_e), const)],
        scratch_shapes=[pltpu.VMEM((1, n_e), F32)],
        compiler_params=_cparams(("arbitrary",)),
        name="out_proj",
    )(oa, ob, oc, od, xs, mod, g2, w_out, wr_cat, wr_hi, br)


def _route_tile(lg, e_ref, pos_ref, gate_ref, cnt_ref, carry_ref):
    i = pl.program_id(0)

    @pl.when(i == 0)
    def _():
        carry_ref[...] = jnp.zeros_like(carry_ref)

    tr, n_e = lg.shape
    lane = lax.broadcasted_iota(I32, (tr, n_e), 1)
    work = lg
    vals, idxs = [], []
    for _ in range(TOP_K):
        m = jnp.max(work, axis=-1, keepdims=True)
        idx = jnp.min(jnp.where(work == m, lane, n_e), axis=-1, keepdims=True)
        vals.append(m)
        idxs.append(idx)
        work = jnp.where(lane == idx, -jnp.inf, work)
    exps = [jnp.exp(v - vals[0]) for v in vals]
    den = exps[0] + exps[1] + exps[2] + exps[3]

    mask = jnp.zeros((tr, n_e), F32)
    for idx in idxs:
        mask = mask + jnp.where(lane == idx, 1.0, 0.0)
    r_i = lax.broadcasted_iota(I32, (tr, tr), 0)
    c_i = lax.broadcasted_iota(I32, (tr, tr), 1)
    tri = jnp.where(c_i <= r_i, 1.0, 0.0).astype(BF16)
    incl = jnp.dot(tri, mask.astype(BF16), preferred_element_type=F32)
    before = carry_ref[...] + incl - mask

    lane_o = lax.broadcasted_iota(I32, (tr, LANE), 1)
    e_out = jnp.zeros((tr, LANE), I32)
    pos_out = jnp.zeros((tr, LANE), I32)
    gate_out = jnp.zeros((tr, LANE), F32)
    for k in range(TOP_K):
        pos_k = jnp.sum(jnp.where(lane == idxs[k], before, 0.0), axis=-1, keepdims=True)
        e_out = jnp.where(lane_o == k, idxs[k], e_out)
        pos_out = jnp.where(lane_o == k, pos_k.astype(I32), pos_out)
        gate_out = jnp.where(lane_o == k, exps[k] / den, gate_out)
    e_ref[...] = e_out
    pos_ref[...] = pos_out
    gate_ref[...] = gate_out
    total = carry_ref[...] + incl[tr - 1:tr, :]
    carry_ref[...] = total
    cnt_ref[...] = total.astype(I32)


def _row_copy(src_ref, src_row, dst_ref, dst_row, sem):
    return pltpu.make_async_copy(src_ref.at[pl.ds(src_row, 1)], dst_ref.at[pl.ds(dst_row, 1)], sem)


def _dispatch_kernel(dest_ref, pad_end_ref, h_ref, xs_ref, zbuf_ref, sem, zsem):
    td = h_ref.shape[0]
    blk = zbuf_ref.shape[0]
    n_e = pad_end_ref.shape[0]
    base = pl.program_id(0) * td * TOP_K

    @pl.when(pl.program_id(0) == 0)
    def _():
        zbuf_ref[...] = jnp.zeros_like(zbuf_ref)

        def fill(e):
            start = pl.multiple_of(pad_end_ref[e] - blk, blk)
            return pltpu.make_async_copy(zbuf_ref, xs_ref.at[pl.ds(start, blk)], zsem)

        def nonempty(e):
            prev = pad_end_ref[jnp.maximum(e - 1, 0)]
            return pad_end_ref[e] > jnp.where(e == 0, 0, prev)

        def start(e, _):
            @pl.when(nonempty(e))
            def _():
                fill(e).start()
            return 0

        def wait(e, _):
            @pl.when(nonempty(e))
            def _():
                fill(e).wait()
            return 0

        lax.fori_loop(0, n_e, start, 0)
        lax.fori_loop(0, n_e, wait, 0)

    def issue(t, _):
        for k in range(TOP_K):
            _row_copy(h_ref, t, xs_ref, dest_ref[base + t * TOP_K + k], sem).start()
        return 0

    lax.fori_loop(0, td, issue, 0, unroll=DMA_UNROLL)
    n_rows = td * TOP_K
    pltpu.make_async_copy(xs_ref.at[pl.ds(0, n_rows)], xs_ref.at[pl.ds(0, n_rows)], sem).wait()


def _dispatch(dest, pad_end, h2, n_slots):
    nt, d = h2.shape
    td = TD_MOE
    return pl.pallas_call(
        _dispatch_kernel,
        out_shape=jax.ShapeDtypeStruct((n_slots, d), F32),
        grid_spec=pltpu.PrefetchScalarGridSpec(
            num_scalar_prefetch=2,
            grid=(nt // td,),
            in_specs=[pl.BlockSpec((td, d), lambda i, dest, pe: (i, 0))],
            out_specs=pl.BlockSpec(memory_space=pl.ANY),
            scratch_shapes=[pltpu.VMEM((BLK_MOE, d), F32), pltpu.SemaphoreType.DMA, pltpu.SemaphoreType.DMA],
        ),
        compiler_params=_cparams(("arbitrary",)),
        name="moe_dispatch",
    )(dest, pad_end, h2)


def _ffn_kernel(blk_e_ref, n_act_ref, x_ref, wgu_ref, bgu_ref, wd_ref, bd_ref, y_ref, wd_bf_ref):
    b = pl.program_id(0)

    @pl.when(b < n_act_ref[0])
    def _():
        @pl.when(jnp.logical_or(b == 0, blk_e_ref[b] != blk_e_ref[jnp.maximum(b - 1, 0)]))
        def _():
            wd_bf_ref[...] = wd_ref[0].astype(BF16)

        d_e = wd_ref.shape[1]
        xb = x_ref[...].astype(BF16)
        h = jnp.dot(xb, wgu_ref[0], preferred_element_type=F32) + bgu_ref[0]
        glu = jnp.minimum(h[:, :d_e], SWIGLU_LIMIT)
        lin = jnp.clip(h[:, d_e:], -SWIGLU_LIMIT, SWIGLU_LIMIT)
        act = glu * jax.nn.sigmoid(SWIGLU_ALPHA * glu) * (lin + 1)
        y_ref[...] = jnp.dot(act.astype(BF16), wd_bf_ref[...], preferred_element_type=F32) + bd_ref[0]


def _ffn(blk_e, n_act, xs, wgu, bgu, wd, bd, layer):
    ns, d = xs.shape
    blk = BLK_MOE
    d_e2 = wgu.shape[-1]
    d_e = d_e2 // 2

    def rows(b, be, na):
        return (jnp.minimum(b, na[0] - 1), 0)

    def expert4(b, be, na):
        return (layer, be[jnp.minimum(b, na[0] - 1)], 0, 0)

    return pl.pallas_call(
        _ffn_kernel,
        out_shape=jax.ShapeDtypeStruct((ns, d), F32),
        grid_spec=pltpu.PrefetchScalarGridSpec(
            num_scalar_prefetch=2,
            grid=(ns // blk,),
            in_specs=[
                pl.BlockSpec((blk, d), rows),
                pl.BlockSpec((None, 1, d, d_e2), expert4), pl.BlockSpec((None, 1, 1, d_e2), expert4),
                pl.BlockSpec((None, 1, d_e, d), expert4), pl.BlockSpec((None, 1, 1, d), expert4),
            ],
            out_specs=pl.BlockSpec((blk, d), rows),
            scratch_shapes=[pltpu.VMEM((d_e, d), BF16)],
        ),
        compiler_params=_cparams(("arbitrary",)),
        name="moe_ffn",
    )(blk_e, n_act, xs, wgu, bgu, wd, bd)


def _combine_kernel(final, dest_ref, ys_ref, gate_ref, x1_ref, mod_ref, fg_ref, o_ref, buf_ref, sems):
    td = x1_ref.shape[0]
    i = pl.program_id(0)
    slot = i % 2

    def gather(tile, dst_slot):
        base = tile * td * TOP_K

        def issue(t, _):
            for k in range(TOP_K):
                _row_copy(ys_ref, dest_ref[base + t * TOP_K + k], buf_ref.at[dst_slot, k], t,
                          sems.at[dst_slot]).start()
            return 0

        lax.fori_loop(0, td, issue, 0, unroll=DMA_UNROLL)

    @pl.when(i == 0)
    def _():
        gather(0, 0)

    @pl.when(i + 1 < pl.num_programs(0))
    def _():
        gather(i + 1, 1 - slot)

    pltpu.make_async_copy(buf_ref.at[slot], buf_ref.at[slot], sems.at[slot]).wait()

    g = gate_ref[...]
    y = g[:, 0:1] * buf_ref[slot, 0]
    for k in range(1, TOP_K):
        y = y + g[:, k:k + 1] * buf_ref[slot, k]
    x = x1_ref[...] + mod_ref[0, 5:6, :] * y
    if final:
        ms = jnp.mean(x * x, axis=-1, keepdims=True)
        x = x * lax.rsqrt(ms + EPS) * fg_ref[...]
    o_ref[...] = x


def _combine(dest, ys, gate, x1, mod, seg_of_tile, final_g, n_rows):
    d = x1.shape[1]
    td = TD_MOE
    ratio = TM_PROJ // td
    final = final_g is not None
    fg = final_g if final else jnp.ones((1, d), F32)
    return pl.pallas_call(
        functools.partial(_combine_kernel, final),
        out_shape=jax.ShapeDtypeStruct((n_rows, d), F32),
        grid_spec=pltpu.PrefetchScalarGridSpec(
            num_scalar_prefetch=1,
            grid=(n_rows // td,),
            in_specs=[
                pl.BlockSpec(memory_space=pl.ANY),
                pl.BlockSpec((td, LANE), lambda i, dest: (i, 0)),
                pl.BlockSpec((td, d), lambda i, dest: (i, 0)),
                pl.BlockSpec((1, N_MOD, d), lambda i, dest: (seg_of_tile(i // ratio), 0, 0)),
                pl.BlockSpec((1, d), lambda i, dest: (0, 0)),
            ],
            out_specs=pl.BlockSpec((td, d), lambda i, dest: (i, 0)),
            scratch_shapes=[pltpu.VMEM((2, TOP_K, td, d), F32), pltpu.SemaphoreType.DMA((2,))],
        ),
        compiler_params=_cparams(("arbitrary",)),
        name="moe_combine",
    )(dest, ys, gate, x1, mod, fg)


def _rope_tables(n_b, seq, ctx_len):
    t = np.arange(seq)
    axis_dim = HEAD_DIM // 2
    inv = (ROPE_THETA ** (-np.arange(0, axis_dim, 2, dtype=np.float32) / axis_dim)).astype(np.float32)
    pos = np.stack([t // GRID_W, t % GRID_W], axis=-1).astype(np.float32)
    ang = jnp.asarray(pos[:, :, None] * inv[None, None, :])
    cos, sin = jnp.cos(ang), jnp.sin(ang)
    cos_h = jnp.concatenate([cos, cos], axis=-1).reshape(seq, HEAD_DIM)
    sin_h = jnp.concatenate([-sin, sin], axis=-1).reshape(seq, HEAD_DIM)
    cos_x = jnp.tile(cos_h, (n_b, LANE // HEAD_DIM))
    sin_x = jnp.tile(sin_h, (n_b, LANE // HEAD_DIM))
    ones = jnp.ones((n_b * ctx_len, LANE), F32)
    return (jnp.concatenate([cos_x, ones], axis=0),
            jnp.concatenate([sin_x, jnp.zeros_like(ones)], axis=0))


def kernel(x, c, ctx, c_ctx, w_ada, b_ada, norm1_g, norm2_g, w_in, w_out, conv_dw_w, conv_dw_b, conv_ln_g,
           conv_ln_b, q_norm_g, k_norm_g, sc_conv_w, na_rpb, w_router, b_router, w_gate_up, b_gate_up, w_down,
           b_down, final_norm_g):
    n_b, seq, d = x.shape
    ctx_len = ctx.shape[1]
    depth = w_ada.shape[0]
    n_e = w_router.shape[2]
    d_e = w_down.shape[2]
    n_x_rows = n_b * seq
    nt = n_x_rows + n_b * ctx_len
    assert d == D_MODEL and seq % GRID_W == 0 and (seq // GRID_W) % NA_RB == 0
    assert seq % TM_PROJ == 0 and (n_b * ctx_len) % TM_PROJ == 0
    assert seq % TM_CONV == 0 and ctx_len % TM_CONV == 0 and seq % ctx_len == 0
    assert seq % KC_GQA == 0 and nt % TD_MOE == 0 and n_b + 1 <= 8

    tiles_x = n_x_rows // TM_PROJ
    tiles_per_b = seq // TM_PROJ

    def seg_of_tile(i):
        return jnp.where(i < tiles_x, 1 + i // tiles_per_b, 0)

    w_in_b = w_in.astype(BF16)
    w_out_b = w_out.astype(BF16)
    wgu_b = _deinterleave_cast(w_gate_up)
    bgu_p = jnp.concatenate([b_gate_up[..., 0::2], b_gate_up[..., 1::2]], axis=-1)[:, :, None, :]
    bd_p = b_down[:, :, None, :]
    wr_hi = w_router.astype(BF16)
    wr_lo = (w_router - wr_hi.astype(F32)).astype(BF16)
    wr_cat = jnp.concatenate([wr_hi, wr_lo], axis=-1)
    bd = jnp.asarray(np.kron(np.eye(GQA_HEADS, dtype=np.float32),
                             np.full((HEAD_DIM, HEAD_DIM), 1.0 / HEAD_DIM, np.float32))).astype(BF16)
    cos_t, sin_t = _rope_tables(n_b, seq, ctx_len)

    cvec = jnp.zeros((8, d), F32).at[0].set(c_ctx).at[1:1 + n_b].set(c)
    mods = _ada_all(cvec, w_ada, b_ada).reshape(depth, 8, N_MOD, d)

    n_assign = nt * TOP_K
    n_blk = -(-n_assign // BLK_MOE) + n_e

    xs = jnp.concatenate([x.reshape(n_x_rows, d), ctx.reshape(n_b * ctx_len, d)], axis=0)
    cblk0 = n_x_rows // ctx_len

    for l in range(depth):
        mod = mods[l]
        pa, q, k, v, ps, nq, nk, nv = _in_proj(
            xs, mod, norm1_g[l][None], w_in_b, l, cos_t, sin_t,
            jnp.tile(q_norm_g[l], GQA_HEADS)[None], jnp.tile(k_norm_g[l], GQA_KV_HEADS)[None], bd, seg_of_tile)
        oa, oc = _conv_mix(pa, ps, conv_dw_w[l], conv_dw_b[l][None], conv_ln_g[l][None], conv_ln_b[l][None],
                           sc_conv_w[l], seq, n_x_rows, ctx_len)
        ob = _gqa(q, k, v, n_b, seq, ctx_len)
        ob = _ctx_attn(q, k, v, ob, n_b, ctx_len, cblk0, GQA_HEADS // GQA_KV_HEADS, LANE)
        od = _na(nq, nk, nv, _na_bias_table(na_rpb[l]), n_b, seq, ctx_len)
        od = _ctx_attn(nq, nk, nv, od, n_b, ctx_len, cblk0, 1, HEAD_DIM)
        x1, h2, e_o, pos_o, gate_o, cnt = _out_proj(oa, ob, oc, od, xs, mod, norm2_g[l][None], w_out_b, l,
                                                    wr_cat[l], wr_hi[l], b_router[l][None], seg_of_tile)
        counts = cnt[0]
        padded = (counts + BLK_MOE - 1) // BLK_MOE * BLK_MOE
        pad_end = jnp.cumsum(padded)
        pad_start = pad_end - padded
        dest = (pad_start[e_o[:, :TOP_K]] + pos_o[:, :TOP_K]).reshape(-1).astype(I32)
        blk_lo = jnp.arange(n_blk, dtype=I32) * BLK_MOE
        blk_e = jnp.minimum(jnp.sum((pad_end[None, :] <= blk_lo[:, None]).astype(I32), axis=1), n_e - 1)
        n_act = (pad_end[-1:] // BLK_MOE).astype(I32)

        x_sorted = _dispatch(dest, pad_end.astype(I32), h2, n_blk * BLK_MOE)
        ys = _ffn(blk_e, n_act, x_sorted, wgu_b, bgu_p, w_down, bd_p, l)
        if l < depth - 1:
            xs = _combine(dest, ys, gate_o, x1, mod, seg_of_tile, None, nt)
        else:
            xs = _combine(dest, ys, gate_o, x1, mod, seg_of_tile, final_norm_g[None], n_x_rows)

    return xs.reshape(n_b, seq, d)
```

```python
import functools

import jax
import jax.numpy as jnp
import numpy as np
from jax import lax
from jax.experimental import pallas as pl
from jax.experimental.pallas import tpu as pltpu

F32 = jnp.float32
BF16 = jnp.bfloat16
I32 = jnp.int32

D_MODEL = 2048
GRID_W = 64
HEAD_DIM = 64
CONV_W = 512
GQA_HEADS = 8
GQA_KV_HEADS = 2
SC_W = 512
NA_HEADS = 8
GQA_Q_W = GQA_HEADS * HEAD_DIM
GQA_KV_W = GQA_KV_HEADS * HEAD_DIM
NA_W = NA_HEADS * HEAD_DIM
MIX_W = CONV_W + GQA_Q_W + SC_W + NA_W
IN_COLS = 2 * CONV_W + GQA_Q_W + 2 * GQA_KV_W + 3 * SC_W + 3 * NA_W
CONV_K = 31
SC_K = 3
WIN_ROWS = 8
WIN_COLS = 16
ROPE_THETA = 10000.0
ATTN_SCALE = HEAD_DIM ** -0.5
TOP_K = 4
SWIGLU_LIMIT = 7.0
SWIGLU_ALPHA = 1.702
N_MOD = 6
EPS = 1e-6
MASK_VALUE = -1e30
LOG2E = 1.4426950408889634
Q_SCALE = ATTN_SCALE * LOG2E

C_AV, C_GQ = 0, 2 * CONV_W
C_GK = C_GQ + GQA_Q_W
C_GV = C_GK + GQA_KV_W
C_SH = C_GV + GQA_KV_W
C_NQ = C_SH + 3 * SC_W
C_NK = C_NQ + NA_W
C_NV = C_NK + NA_W

VMEM_LIMIT = 56 * 1024 * 1024
LANE = 128

TM_PROJ = 512
TM_COMB = 256
TM_CONV = 256
HALO = 16
TQ_GQA = 256
KC_GQA = 512
NA_RB = 8
TD_MOE = 256
BLK_MOE = 256
DMA_UNROLL = 4


def _cparams(sem):
    return pltpu.CompilerParams(dimension_semantics=sem, vmem_limit_bytes=VMEM_LIMIT)


def _resident(shape, index_map):
    return pl.BlockSpec(shape, index_map, pipeline_mode=pl.Buffered(1))


def _split_bf16(a):
    hi = a.astype(BF16)
    lo = (a - hi.astype(F32)).astype(BF16)
    return hi, lo


PREP_COLS = 256
PREP_ROWS = 1024


def _deint_kernel(w_ref, p_ref, o_ref):
    n = w_ref.shape[1]
    half = PREP_COLS // 2
    for b in range(n // PREP_COLS):
        t = jnp.dot(w_ref[:, b * PREP_COLS:(b + 1) * PREP_COLS].astype(BF16), p_ref[...],
                    preferred_element_type=F32).astype(BF16)
        o_ref[:, b * half:(b + 1) * half] = t[:, :half]
        o_ref[:, n // 2 + b * half:n // 2 + (b + 1) * half] = t[:, half:]


def _deinterleave_cast(w):
    shape = w.shape
    n = shape[-1]
    rows = int(np.prod(shape[:-1]))
    assert rows % PREP_ROWS == 0 and n % PREP_COLS == 0
    perm = np.zeros((PREP_COLS, PREP_COLS), np.float32)
    j = np.arange(PREP_COLS // 2)
    perm[2 * j, j] = 1.0
    perm[2 * j + 1, PREP_COLS // 2 + j] = 1.0
    out = pl.pallas_call(
        _deint_kernel,
        out_shape=jax.ShapeDtypeStruct((rows, n), BF16),
        grid=(rows // PREP_ROWS,),
        in_specs=[pl.BlockSpec((PREP_ROWS, n), lambda i: (i, 0)),
                  pl.BlockSpec((PREP_COLS, PREP_COLS), lambda i: (0, 0))],
        out_specs=pl.BlockSpec((PREP_ROWS, n), lambda i: (i, 0)),
        compiler_params=_cparams(("parallel",)),
        name="wgu_prep",
    )(w.reshape(rows, n), jnp.asarray(perm, BF16))
    return out.reshape(shape)


def _ada_kernel(c_ref, w_ref, b_ref, o_ref):
    c = c_ref[...]
    s = c * jax.nn.sigmoid(c)
    hi, lo = _split_bf16(s)
    lhs = jnp.concatenate([hi, lo], axis=0)
    r = jnp.dot(lhs, w_ref[0].astype(BF16), preferred_element_type=F32)
    o_ref[0] = r[:8] + r[8:] + b_ref[0]


def _ada_all(cvec, w_ada, b_ada):
    n_l, d, n6 = w_ada.shape
    tn = 1024
    return pl.pallas_call(
        _ada_kernel,
        out_shape=jax.ShapeDtypeStruct((n_l, 8, n6), F32),
        grid=(n_l, n6 // tn),
        in_specs=[
            pl.BlockSpec((8, d), lambda l, j: (0, 0)),
            pl.BlockSpec((1, d, tn), lambda l, j: (l, 0, j)),
            pl.BlockSpec((1, 1, tn), lambda l, j: (l, 0, j)),
        ],
        out_specs=pl.BlockSpec((1, 8, tn), lambda l, j: (l, 0, j)),
        compiler_params=_cparams(("parallel", "parallel")),
        name="ada_mod",
    )(cvec, w_ada, b_ada.reshape(n_l, 1, n6))


def _rope(x, cos, sin_signed):
    n = x.shape[1]
    lane = lax.broadcasted_iota(I32, x.shape, 1)
    first = (lane % 32) < 16
    partner = jnp.where(first, pltpu.roll(x, n - 16, 1), pltpu.roll(x, 16, 1))
    return x * cos + partner * sin_signed


def _head_mean_sq(y, bd):
    hi, lo = _split_bf16(y * y)
    return (jnp.dot(hi, bd, preferred_element_type=F32)
            + jnp.dot(lo, bd, preferred_element_type=F32))


def _in_kernel(x_ref, *refs):
    _in_body(x_ref[...], *refs)


def _in_body(x, mod_ref, g_ref, w_ref, cos_ref, sin_ref, qg_ref, kg_ref, bd_ref,
             pa_ref, q_ref, k_ref, v_ref, ps_ref, nq_ref, nk_ref, nv_ref):
    ms = jnp.mean(x * x, axis=-1, keepdims=True)
    y = x * lax.rsqrt(ms + EPS) * g_ref[...]
    shift = mod_ref[0, 0:1, :]
    scale = mod_ref[0, 1:2, :]
    h = (y * (1 + scale) + shift).astype(BF16)

    def proj(lo, hi):
        return jnp.dot(h, w_ref[:, lo:hi], preferred_element_type=F32)

    pa_ref[...] = proj(C_AV, C_GQ).astype(BF16)

    cos = cos_ref[...]
    sin = sin_ref[...]
    gq = proj(C_GQ, C_GK)
    qn = gq * lax.rsqrt(_head_mean_sq(gq, bd_ref[...]) + EPS) * qg_ref[...]
    cos4 = jnp.concatenate([cos] * 4, axis=1)
    sin4 = jnp.concatenate([sin] * 4, axis=1)
    q_ref[...] = (_rope(qn, cos4, sin4) * Q_SCALE).astype(BF16)

    gk = proj(C_GK, C_GV)
    kn = gk * lax.rsqrt(_head_mean_sq(gk, bd_ref[0:GQA_KV_W, 0:GQA_KV_W]) + EPS) * kg_ref[...]
    kr = _rope(kn, cos, sin).astype(BF16)
    hd = HEAD_DIM
    k_ref[...] = jnp.concatenate([kr[:, :hd], kr[:, :hd], kr[:, hd:], kr[:, hd:]], axis=1)
    vv = proj(C_GV, C_SH).astype(BF16)
    ones = jnp.ones((vv.shape[0], hd), BF16)
    v_ref[...] = jnp.concatenate([vv[:, :hd], ones, vv[:, hd:], ones], axis=1)
    ps_ref[...] = proj(C_SH, C_NQ).astype(BF16)
    nq_ref[...] = (proj(C_NQ, C_NK) * Q_SCALE).astype(BF16)
    nk_ref[...] = proj(C_NK, C_NV).astype(BF16)
    nv_ref[...] = proj(C_NV, IN_COLS).astype(BF16)


def _in_proj(xs, mod, g1, w_in, layer, cos_t, sin_t, qg, kg, bd, seg_of_tile):
    nt, d = xs.shape
    tm = TM_PROJ
    row = lambda i: (i, 0)
    const = lambda i: (0, 0)
    widths = (2 * CONV_W, GQA_Q_W, 2 * GQA_KV_W, 2 * GQA_KV_W, 3 * SC_W, NA_W, NA_W, NA_W)
    return pl.pallas_call(
        _in_kernel,
        out_shape=[jax.ShapeDtypeStruct((nt, w), BF16) for w in widths],
        grid=(nt // tm,),
        in_specs=[
            pl.BlockSpec((tm, d), row),
            pl.BlockSpec((1, N_MOD, d), lambda i: (seg_of_tile(i), 0, 0)),
            pl.BlockSpec((1, d), const),
            _resident((None, d, IN_COLS), lambda i: (layer, 0, 0)),
            pl.BlockSpec((tm, LANE), row),
            pl.BlockSpec((tm, LANE), row),
            pl.BlockSpec((1, GQA_Q_W), const),
            pl.BlockSpec((1, GQA_KV_W), const),
            pl.BlockSpec((GQA_Q_W, GQA_Q_W), const),
        ],
        out_specs=[pl.BlockSpec((tm, w), row) for w in widths],
        compiler_params=_cparams(("parallel",)),
        name="in_proj",
    )(xs, mod, g1, w_in, cos_t, sin_t, qg, kg, bd)


def _in_comb_kernel(dest_ref, ys_ref, gate_ref, x1_ref, modp_ref, *rest):
    *body_refs, buf_ref, sem = rest
    xs_ref = body_refs[8]
    body_refs = body_refs[:8] + body_refs[9:]
    tm = x1_ref.shape[0]
    i = pl.program_id(0)
    n = pl.num_programs(0)

    @pl.when(i == 0)
    def _():
        def issue(t, _):
            for k in range(TOP_K):
                _row_copy(ys_ref, dest_ref[t * TOP_K + k], buf_ref.at[k], t, sem).start()
            return 0

        lax.fori_loop(0, tm, issue, 0, unroll=DMA_UNROLL)

    pltpu.make_async_copy(buf_ref, buf_ref, sem).wait()
    g = gate_ref[...]
    y = g[:, 0:1] * buf_ref[0]
    for k in range(1, TOP_K):
        y = y + g[:, k:k + 1] * buf_ref[k]
    x = x1_ref[...] + modp_ref[0, 5:6, :] * y
    xs_ref[...] = x

    base = jnp.minimum(i + 1, n - 1) * (tm * TOP_K)
    for t in range(tm):
        for k in range(TOP_K):
            _row_copy(ys_ref, dest_ref[base + (t * TOP_K + k)], buf_ref.at[k], t, sem).start()

    _in_body(x, *body_refs)

    @pl.when(i == n - 1)
    def _():
        pltpu.make_async_copy(buf_ref, buf_ref, sem).wait()


def _in_proj_combine(dest, ys, gate, x1, mod_prev, mod, g1, w_in, layer, cos_t, sin_t, qg, kg, bd, seg_of_tile):
    nt, d = x1.shape
    tm = TM_COMB
    ratio = TM_PROJ // tm
    row = lambda i, dest: (i, 0)
    const = lambda i, dest: (0, 0)
    seg = lambda i, dest: (seg_of_tile(i // ratio), 0, 0)
    widths = (2 * CONV_W, GQA_Q_W, 2 * GQA_KV_W, 2 * GQA_KV_W, 3 * SC_W, NA_W, NA_W, NA_W)
    return pl.pallas_call(
        _in_comb_kernel,
        out_shape=([jax.ShapeDtypeStruct((nt, d), F32)]
                   + [jax.ShapeDtypeStruct((nt, w), BF16) for w in widths]),
        grid_spec=pltpu.PrefetchScalarGridSpec(
            num_scalar_prefetch=1,
            grid=(nt // tm,),
            in_specs=[
                pl.BlockSpec(memory_space=pl.ANY),
                pl.BlockSpec((tm, LANE), row),
                pl.BlockSpec((tm, d), row),
                pl.BlockSpec((1, N_MOD, d), seg),
                pl.BlockSpec((1, N_MOD, d), seg),
                pl.BlockSpec((1, d), const),
                _resident((None, d, IN_COLS), lambda i, dest: (layer, 0, 0)),
                pl.BlockSpec((tm, LANE), row),
                pl.BlockSpec((tm, LANE), row),
                pl.BlockSpec((1, GQA_Q_W), const),
                pl.BlockSpec((1, GQA_KV_W), const),
                pl.BlockSpec((GQA_Q_W, GQA_Q_W), const),
            ],
            out_specs=([pl.BlockSpec((tm, d), row)] + [pl.BlockSpec((tm, w), row) for w in widths]),
            scratch_shapes=[pltpu.VMEM((TOP_K, tm, d), F32), pltpu.SemaphoreType.DMA],
        ),
        compiler_params=_cparams(("arbitrary",)),
        name="in_proj_combine",
    )(dest, ys, gate, x1, mod_prev, mod, g1, w_in, cos_t, sin_t, qg, kg, bd)


def _conv_kernel(seq, n_x_rows, ctx_len,
                 pa_c, pa_p, pa_n, ps_c, ps_p, ps_n, dww_ref, dwb_ref, lng_ref, lnb_ref, scw_ref,
                 oa_ref, oc_ref, ext_ref, ext2_ref, acc_ref, sh_ref):
    tc = TM_CONV
    row0 = pl.program_id(0) * tc
    in_x = row0 < n_x_rows
    rel = jnp.where(in_x, row0, row0 - n_x_rows)
    seg = jnp.where(in_x, seq, ctx_len)
    keep_p = jnp.where(rel % seg == 0, 0.0, 1.0).astype(F32)
    keep_n = jnp.where((rel + tc) % seg == 0, 0.0, 1.0).astype(F32)

    def glu(ref):
        a = ref[...].astype(F32)
        return a[:, :CONV_W] * jax.nn.sigmoid(a[:, CONV_W:])

    ext_ref[0:HALO, :] = glu(pa_p) * keep_p
    ext_ref[HALO:HALO + tc, :] = glu(pa_c)
    ext_ref[HALO + tc:, :] = glu(pa_n) * keep_n

    pad = HALO - CONV_K // 2
    n_sh = sh_ref.shape[1]
    for b in range(8):
        sh_ref[b] = ext_ref[b:b + n_sh, :]
    rc = 64
    for c in range(CONV_W // LANE):
        cs = slice(c * LANE, (c + 1) * LANE)
        for r in range(tc // rc):
            acc = jnp.zeros((rc, LANE), F32)
            for k in range(CONV_K):
                a, b = divmod(k + pad, 8)
                lo = r * rc + 8 * a
                acc = acc + dww_ref[k:k + 1, cs] * sh_ref[b, lo:lo + rc, cs]
            acc_ref[r * rc:(r + 1) * rc, cs] = acc + dwb_ref[:, cs]

    u = acc_ref[...]
    mu = jnp.mean(u, axis=-1, keepdims=True)
    var = jnp.mean(jnp.square(u - mu), axis=-1, keepdims=True)
    yn = (u - mu) * lax.rsqrt(var + EPS) * lng_ref[...] + lnb_ref[...]
    oa_ref[...] = (yn * jax.nn.sigmoid(yn)).astype(BF16)

    def gated(ref, lo, hi):
        a = ref[lo:hi, :].astype(F32)
        return a[:, 2 * SC_W:] * a[:, :SC_W]

    ext2_ref[0:8, :] = gated(ps_p, HALO - 8, HALO) * keep_p
    ext2_ref[8:8 + tc, :] = gated(ps_c, 0, tc)
    ext2_ref[8 + tc:, :] = gated(ps_n, 0, 8) * keep_n
    conv = (scw_ref[0:1, :] * ext2_ref[7:7 + tc, :]
            + scw_ref[1:2, :] * ext2_ref[8:8 + tc, :]
            + scw_ref[2:3, :] * ext2_ref[9:9 + tc, :])
    oc_ref[...] = (ps_c[:, SC_W:2 * SC_W].astype(F32) * conv).astype(BF16)


def _conv_mix(pa, ps, dww, dwb, lng, lnb, scw, seq, n_x_rows, ctx_len):
    nt = pa.shape[0]
    tc = TM_CONV
    hb = tc // HALO
    n_hb = nt // HALO
    row = lambda i: (i, 0)
    prev = lambda i: (jnp.maximum(i * hb - 1, 0), 0)
    nxt = lambda i: (jnp.minimum((i + 1) * hb, n_hb - 1), 0)
    const = lambda i: (0, 0)
    wa, ws = pa.shape[1], ps.shape[1]
    return pl.pallas_call(
        functools.partial(_conv_kernel, seq, n_x_rows, ctx_len),
        out_shape=[jax.ShapeDtypeStruct((nt, CONV_W), BF16), jax.ShapeDtypeStruct((nt, SC_W), BF16)],
        grid=(nt // tc,),
        in_specs=[
            pl.BlockSpec((tc, wa), row), pl.BlockSpec((HALO, wa), prev), pl.BlockSpec((HALO, wa), nxt),
            pl.BlockSpec((tc, ws), row), pl.BlockSpec((HALO, ws), prev), pl.BlockSpec((HALO, ws), nxt),
            pl.BlockSpec((CONV_K, CONV_W), const), pl.BlockSpec((1, CONV_W), const),
            pl.BlockSpec((1, CONV_W), const), pl.BlockSpec((1, CONV_W), const),
            pl.BlockSpec((SC_K, SC_W), const),
        ],
        out_specs=[pl.BlockSpec((tc, CONV_W), row), pl.BlockSpec((tc, SC_W), row)],
        scratch_shapes=[pltpu.VMEM((tc + 2 * HALO, CONV_W), F32),
                        pltpu.VMEM((tc + 16, SC_W), F32),
                        pltpu.VMEM((tc, CONV_W), F32),
                        pltpu.VMEM((8, tc + 2 * HALO - 8, CONV_W), F32)],
        compiler_params=_cparams(("parallel",)),
        name="conv_mix",
    )(pa, pa, pa, ps, ps, ps, dww, dwb, lng, lnb, scw)


def _qk(q, k):
    return lax.dot_general(q, k, (((1,), (1,)), ((), ())), preferred_element_type=F32)


def _softmax_pv(s, v):
    m = jnp.max(s, axis=-1, keepdims=True)
    p = jnp.exp2(s - m)
    l = jnp.sum(p, axis=-1, keepdims=True)
    return jnp.dot(p.astype(BF16), v, preferred_element_type=F32) / l


def _half_masks(rows):
    lane = lax.broadcasted_iota(I32, (rows, LANE), 1)
    lo = lane < HEAD_DIM
    return lo, jnp.logical_not(lo)


def _ctx_attn_kernel(kv_rep, kv_step, q_ref, k_ref, v_ref, o_in_ref, o_ref):
    del o_in_ref
    n_h = q_ref.shape[1] // HEAD_DIM
    outs = []
    for h in range(n_h):
        off = (h // kv_rep) * kv_step
        q = q_ref[:, h * HEAD_DIM:(h + 1) * HEAD_DIM]
        k = k_ref[:, off:off + HEAD_DIM]
        v = v_ref[:, off:off + HEAD_DIM]
        outs.append(_softmax_pv(_qk(q, k), v))
    o_ref[...] = jnp.concatenate(outs, axis=1).astype(BF16)


def _ctx_attn(q, k, v, o_all, n_b, ctx_len, blk0, kv_rep, kv_step):
    kw = k.shape[1]
    qw = q.shape[1]
    rows = lambda b: (blk0 + b, 0)
    return pl.pallas_call(
        functools.partial(_ctx_attn_kernel, kv_rep, kv_step),
        out_shape=jax.ShapeDtypeStruct(o_all.shape, o_all.dtype),
        grid=(n_b,),
        in_specs=[pl.BlockSpec((ctx_len, qw), rows), pl.BlockSpec((ctx_len, kw), rows),
                  pl.BlockSpec((ctx_len, kw), rows), pl.BlockSpec(memory_space=pl.ANY)],
        out_specs=pl.BlockSpec((ctx_len, qw), rows),
        input_output_aliases={3: 0},
        compiler_params=_cparams(("parallel",)),
        name="ctx_attn",
    )(q, k, v, o_all)


def _gqa_kernel(n_chunks, q_ref, kx_ref, vx_ref, kc_ref, vc_ref, o_ref):
    tq = q_ref.shape[0]
    grp = GQA_HEADS // GQA_KV_HEADS
    lo, hi = _half_masks(tq)

    def q_group(g):
        parts = []
        for j in range(grp):
            h = g * grp + j
            blk = q_ref[:, (h // 2) * LANE:(h // 2 + 1) * LANE].astype(F32)
            parts.append(jnp.where(lo if h % 2 == 0 else hi, blk, 0.0).astype(BF16))
        return jnp.concatenate(parts, axis=0)

    def update(q, k, v, m, acc):
        s = _qk(q, k)
        m_new = jnp.maximum(m, jnp.max(s, axis=-1, keepdims=True))
        p = jnp.exp2(s - m_new)
        acc = jnp.exp2(m - m_new) * acc + jnp.dot(p.astype(BF16), v, preferred_element_type=F32)
        return m_new, acc

    qs = [q_group(g) for g in range(GQA_KV_HEADS)]
    gsl = [slice(g * LANE, (g + 1) * LANE) for g in range(GQA_KV_HEADS)]
    state = []
    for g in range(GQA_KV_HEADS):
        s = _qk(qs[g], kc_ref[:, gsl[g]])
        m = jnp.max(s, axis=-1, keepdims=True)
        p = jnp.exp2(s - m)
        state += [m, jnp.dot(p.astype(BF16), vc_ref[:, gsl[g]], preferred_element_type=F32)]

    def body(c, carry):
        rows = pl.ds(pl.multiple_of(c * KC_GQA, KC_GQA), KC_GQA)
        out = []
        for g in range(GQA_KV_HEADS):
            out += update(qs[g], kx_ref[rows, gsl[g]], vx_ref[rows, gsl[g]], carry[2 * g], carry[2 * g + 1])
        return tuple(out)

    state = lax.fori_loop(0, n_chunks, body, tuple(state), unroll=4)

    lo4, _ = _half_masks(grp * tq)
    heads = []
    for g in range(GQA_KV_HEADS):
        acc = state[2 * g + 1]
        inv = 1.0 / jnp.where(lo4, 1.0, acc)
        o = acc * pltpu.roll(inv, HEAD_DIM, 1)
        heads.extend(o[j * tq:(j + 1) * tq] for j in range(grp))
    pairs = [jnp.where(lo, heads[2 * i], pltpu.roll(heads[2 * i + 1], HEAD_DIM, 1))
             for i in range(GQA_HEADS // 2)]
    o_ref[...] = jnp.concatenate(pairs, axis=1).astype(BF16)


def _gqa(q, k, v, n_b, seq, ctx_len):
    nt = q.shape[0]
    tq = TQ_GQA
    nq = seq // tq
    cblk0 = n_b * seq // ctx_len
    kw = k.shape[1]
    return pl.pallas_call(
        functools.partial(_gqa_kernel, seq // KC_GQA),
        out_shape=jax.ShapeDtypeStruct((nt, GQA_Q_W), BF16),
        grid=(n_b, nq),
        in_specs=[
            pl.BlockSpec((tq, GQA_Q_W), lambda b, j: (b * nq + j, 0)),
            pl.BlockSpec((seq, kw), lambda b, j: (b, 0)),
            pl.BlockSpec((seq, kw), lambda b, j: (b, 0)),
            pl.BlockSpec((ctx_len, kw), lambda b, j: (cblk0 + b, 0)),
            pl.BlockSpec((ctx_len, kw), lambda b, j: (cblk0 + b, 0)),
        ],
        out_specs=pl.BlockSpec((tq, GQA_Q_W), lambda b, j: (b * nq + j, 0)),
        compiler_params=_cparams(("parallel", "parallel")),
        name="gqa_flash",
    )(q, k, v, k, v)


def _na_kernel(n_rows, q_ref, k_ref, v_ref, kc_ref, vc_ref, t_ref, o_ref):
    rb = pl.program_id(2)
    win = WIN_ROWS * GRID_W
    tq = q_ref.shape[0]
    lo, hi = _half_masks(tq)
    qf = q_ref[...].astype(F32)
    kc = kc_ref[...]
    vc = vc_ref[...]
    halves = []
    for j in range(LANE // HEAD_DIM):
        qa = jnp.where(lo if j == 0 else hi, qf, 0.0).astype(BF16)
        s_ctx = _qk(qa, kc)
        m_ctx = jnp.max(s_ctx, axis=-1, keepdims=True)
        s_wins, ms, krows_l = [], [], []
        for i in range(NA_RB):
            r = rb * NA_RB + i
            r0 = jnp.clip(r - WIN_ROWS // 2, 0, n_rows - WIN_ROWS)
            shift = r0 - r + (WIN_ROWS - 1)
            krows = pl.ds(pl.multiple_of(r0 * GRID_W, GRID_W), win)
            rs = slice(i * GRID_W, (i + 1) * GRID_W)
            bias = jnp.concatenate([t_ref[j, 2 * wp + shift] for wp in range(WIN_ROWS // 2)], axis=1)
            s_win = _qk(qa[rs], k_ref[krows, :]) + bias
            s_wins.append(s_win)
            ms.append(jnp.maximum(jnp.max(s_win, axis=-1, keepdims=True), m_ctx[rs]))
            krows_l.append(krows)
        m_all = jnp.concatenate(ms, axis=0)
        p_ctx = jnp.exp2(s_ctx - m_all)
        l_ctx = jnp.sum(p_ctx, axis=-1, keepdims=True)
        o_ctx = jnp.dot(p_ctx.astype(BF16), vc, preferred_element_type=F32)
        outs = []
        for i in range(NA_RB):
            rs = slice(i * GRID_W, (i + 1) * GRID_W)
            p = jnp.exp2(s_wins[i] - ms[i])
            l = jnp.sum(p, axis=-1, keepdims=True) + l_ctx[rs]
            o = jnp.dot(p.astype(BF16), v_ref[krows_l[i], :], preferred_element_type=F32) + o_ctx[rs]
            outs.append(o / l)
        halves.append(jnp.concatenate(outs, axis=0))
    o_ref[...] = jnp.where(lo, halves[0], halves[1]).astype(BF16)


def _na(nq, nk, nv, table, n_b, seq, ctx_len):
    nt = nq.shape[0]
    n_rows = seq // GRID_W
    n_rb = n_rows // NA_RB
    tq = NA_RB * GRID_W
    cblk0 = n_b * seq // ctx_len
    hp = LANE // HEAD_DIM
    return pl.pallas_call(
        functools.partial(_na_kernel, n_rows),
        out_shape=jax.ShapeDtypeStruct((nt, NA_W), BF16),
        grid=(n_b, NA_W // LANE, n_rb),
        in_specs=[
            pl.BlockSpec((tq, LANE), lambda b, h, r: (b * n_rb + r, h)),
            pl.BlockSpec((seq, LANE), lambda b, h, r: (b, h)),
            pl.BlockSpec((seq, LANE), lambda b, h, r: (b, h)),
            pl.BlockSpec((ctx_len, LANE), lambda b, h, r: (cblk0 + b, h)),
            pl.BlockSpec((ctx_len, LANE), lambda b, h, r: (cblk0 + b, h)),
            pl.BlockSpec((hp, 2 * WIN_ROWS - 2, GRID_W, LANE), lambda b, h, r: (h, 0, 0, 0)),
        ],
        out_specs=pl.BlockSpec((tq, LANE), lambda b, h, r: (b * n_rb + r, h)),
        compiler_params=_cparams(("parallel", "parallel", "parallel")),
        name="na_attn",
    )(nq, nk, nv, nk, nv, table)


def _na_bias_table(rpb):
    cols = np.arange(GRID_W)
    c0 = np.clip(cols - WIN_COLS // 2, 0, GRID_W - WIN_COLS)
    kc = np.arange(GRID_W)
    rel = kc[None, :] - cols[:, None] + (WIN_COLS - 1)
    valid = (kc[None, :] >= c0[:, None]) & (kc[None, :] < c0[:, None] + WIN_COLS)
    rel = np.clip(rel, 0, 2 * WIN_COLS - 2)
    t = rpb[:, :, rel]
    t = jnp.where(jnp.asarray(valid)[None, None], t * LOG2E, MASK_VALUE).astype(F32)
    return jnp.concatenate([t[:, :-1], t[:, 1:]], axis=-1)


def _out_kernel(oa_ref, ob_ref, oc_ref, od_ref, x_ref, mod_ref, g_ref, w_ref, wr_cat_ref, wr_hi_ref, br_ref,
                x1_ref, h2_ref, e_ref, pos_ref, gate_ref, cnt_ref, carry_ref):
    acc = jnp.dot(oa_ref[...], w_ref[0:CONV_W, :], preferred_element_type=F32)
    acc = acc + jnp.dot(ob_ref[...], w_ref[CONV_W:CONV_W + GQA_Q_W, :], preferred_element_type=F32)
    acc = acc + jnp.dot(oc_ref[...], w_ref[CONV_W + GQA_Q_W:MIX_W - NA_W, :], preferred_element_type=F32)
    acc = acc + jnp.dot(od_ref[...], w_ref[MIX_W - NA_W:MIX_W, :], preferred_element_type=F32)
    x1 = x_ref[...] + mod_ref[0, 2:3, :] * acc
    x1_ref[...] = x1
    ms = jnp.mean(x1 * x1, axis=-1, keepdims=True)
    y = x1 * lax.rsqrt(ms + EPS) * g_ref[...]
    h2 = y * (1 + mod_ref[0, 4:5, :]) + mod_ref[0, 3:4, :]
    h2_ref[...] = h2
    hi, lo = _split_bf16(h2)
    n_e = br_ref.shape[1]
    a = jnp.dot(hi, wr_cat_ref[...], preferred_element_type=F32)
    b = jnp.dot(lo, wr_hi_ref[...], preferred_element_type=F32)
    logits = a[:, :n_e] + a[:, n_e:] + b + br_ref[...]
    _route_tile(logits, e_ref, pos_ref, gate_ref, cnt_ref, carry_ref)


def _out_proj(oa, ob, oc, od, xs, mod, g2, w_out, layer, wr_cat, wr_hi, br, seg_of_tile):
    nt, d = xs.shape
    tm = TM_PROJ
    n_e = wr_hi.shape[1]
    row = lambda i: (i, 0)
    const = lambda i: (0, 0)
    return pl.pallas_call(
        _out_kernel,
        out_shape=[jax.ShapeDtypeStruct((nt, d), F32), jax.ShapeDtypeStruct((nt, d), F32),
                   jax.ShapeDtypeStruct((nt, LANE), I32), jax.ShapeDtypeStruct((nt, LANE), I32),
                   jax.ShapeDtypeStruct((nt, LANE), F32), jax.ShapeDtypeStruct((1, n_e), I32)],
        grid=(nt // tm,),
        in_specs=[
            pl.BlockSpec((tm, CONV_W), row), pl.BlockSpec((tm, GQA_Q_W), row),
            pl.BlockSpec((tm, SC_W), row), pl.BlockSpec((tm, NA_W), row),
            pl.BlockSpec((tm, d), row),
            pl.BlockSpec((1, N_MOD, d), lambda i: (seg_of_tile(i), 0, 0)),
            pl.BlockSpec((1, d), const),
            _resident((None, MIX_W, d), lambda i: (layer, 0, 0)),
            pl.BlockSpec((d, 2 * n_e), const), pl.BlockSpec((d, n_e), const), pl.BlockSpec((1, n_e), const),
        ],
        out_specs=[pl.BlockSpec((tm, d), row), pl.BlockSpec((tm, d), row),
                   pl.BlockSpec((tm, LANE), row), pl.BlockSpec((tm, LANE), row), pl.BlockSpec((tm, LANE), row),
                   pl.BlockSpec((1, n_e), const)],
        scratch_shapes=[pltpu.VMEM((1, n_e), F32)],
        compiler_params=_cparams(("arbitrary",)),
        name="out_proj",
    )(oa, ob, oc, od, xs, mod, g2, w_out, wr_cat, wr_hi, br)


def _route_tile(lg, e_ref, pos_ref, gate_ref, cnt_ref, carry_ref):
    i = pl.program_id(0)

    @pl.when(i == 0)
    def _():
        carry_ref[...] = jnp.zeros_like(carry_ref)

    tr, n_e = lg.shape
    lane = lax.broadcasted_iota(I32, (tr, n_e), 1)
    work = lg
    vals, idxs = [], []
    for _ in range(TOP_K):
        m = jnp.max(work, axis=-1, keepdims=True)
        idx = jnp.min(jnp.where(work == m, lane, n_e), axis=-1, keepdims=True)
        vals.append(m)
        idxs.append(idx)
        work = jnp.where(lane == idx, -jnp.inf, work)
    exps = [jnp.exp(v - vals[0]) for v in vals]
    den = exps[0] + exps[1] + exps[2] + exps[3]

    mask = jnp.zeros((tr, n_e), F32)
    for idx in idxs:
        mask = mask + jnp.where(lane == idx, 1.0, 0.0)
    r_i = lax.broadcasted_iota(I32, (tr, tr), 0)
    c_i = lax.broadcasted_iota(I32, (tr, tr), 1)
    tri = jnp.where(c_i <= r_i, 1.0, 0.0).astype(BF16)
    incl = jnp.dot(tri, mask.astype(BF16), preferred_element_type=F32)
    before = carry_ref[...] + incl - mask

    lane_o = lax.broadcasted_iota(I32, (tr, LANE), 1)
    e_out = jnp.zeros((tr, LANE), I32)
    pos_out = jnp.zeros((tr, LANE), I32)
    gate_out = jnp.zeros((tr, LANE), F32)
    for k in range(TOP_K):
        pos_k = jnp.sum(jnp.where(lane == idxs[k], before, 0.0), axis=-1, keepdims=True)
        e_out = jnp.where(lane_o == k, idxs[k], e_out)
        pos_out = jnp.where(lane_o == k, pos_k.astype(I32), pos_out)
        gate_out = jnp.where(lane_o == k, exps[k] / den, gate_out)
    e_ref[...] = e_out
    pos_ref[...] = pos_out
    gate_ref[...] = gate_out
    total = carry_ref[...] + incl[tr - 1:tr, :]
    carry_ref[...] = total
    cnt_ref[...] = total.astype(I32)


def _row_copy(src_ref, src_row, dst_ref, dst_row, sem):
    return pltpu.make_async_copy(src_ref.at[pl.ds(src_row, 1)], dst_ref.at[pl.ds(dst_row, 1)], sem)


def _dispatch_kernel(dest_ref, pad_end_ref, h_ref, xs_ref, zbuf_ref, sem, zsem):
    td = h_ref.shape[0]
    blk = zbuf_ref.shape[0]
    n_e = pad_end_ref.shape[0]
    base = pl.program_id(0) * td * TOP_K

    @pl.when(pl.program_id(0) == 0)
    def _():
        zbuf_ref[...] = jnp.zeros_like(zbuf_ref)

        def fill(e):
            start = pl.multiple_of(pad_end_ref[e] - blk, blk)
            return pltpu.make_async_copy(zbuf_ref, xs_ref.at[pl.ds(start, blk)], zsem)

        def nonempty(e):
            prev = pad_end_ref[jnp.maximum(e - 1, 0)]
            return pad_end_ref[e] > jnp.where(e == 0, 0, prev)

        def start(e, _):
            @pl.when(nonempty(e))
            def _():
                fill(e).start()
            return 0

        def wait(e, _):
            @pl.when(nonempty(e))
            def _():
                fill(e).wait()
            return 0

        lax.fori_loop(0, n_e, start, 0)
        lax.fori_loop(0, n_e, wait, 0)

    def issue(t, _):
        for k in range(TOP_K):
            _row_copy(h_ref, t, xs_ref, dest_ref[base + t * TOP_K + k], sem).start()
        return 0

    lax.fori_loop(0, td, issue, 0, unroll=DMA_UNROLL)
    n_rows = td * TOP_K
    pltpu.make_async_copy(xs_ref.at[pl.ds(0, n_rows)], xs_ref.at[pl.ds(0, n_rows)], sem).wait()


def _dispatch(dest, pad_end, h2, n_slots):
    nt, d = h2.shape
    td = TD_MOE
    return pl.pallas_call(
        _dispatch_kernel,
        out_shape=jax.ShapeDtypeStruct((n_slots, d), F32),
        grid_spec=pltpu.PrefetchScalarGridSpec(
            num_scalar_prefetch=2,
            grid=(nt // td,),
            in_specs=[pl.BlockSpec((td, d), lambda i, dest, pe: (i, 0))],
            out_specs=pl.BlockSpec(memory_space=pl.ANY),
            scratch_shapes=[pltpu.VMEM((BLK_MOE, d), F32), pltpu.SemaphoreType.DMA, pltpu.SemaphoreType.DMA],
        ),
        compiler_params=_cparams(("arbitrary",)),
        name="moe_dispatch",
    )(dest, pad_end, h2)


def _ffn_kernel(blk_e_ref, n_act_ref, x_ref, wgu_ref, bgu_ref, wd_ref, bd_ref, y_ref, wd_bf_ref):
    b = pl.program_id(0)

    @pl.when(b < n_act_ref[0])
    def _():
        @pl.when(jnp.logical_or(b == 0, blk_e_ref[b] != blk_e_ref[jnp.maximum(b - 1, 0)]))
        def _():
            wd_bf_ref[...] = wd_ref[0].astype(BF16)

        d_e = wd_ref.shape[1]
        xb = x_ref[...].astype(BF16)
        h = jnp.dot(xb, wgu_ref[0], preferred_element_type=F32) + bgu_ref[0]
        glu = jnp.minimum(h[:, :d_e], SWIGLU_LIMIT)
        lin = jnp.clip(h[:, d_e:], -SWIGLU_LIMIT, SWIGLU_LIMIT)
        act = glu * jax.nn.sigmoid(SWIGLU_ALPHA * glu) * (lin + 1)
        y_ref[...] = jnp.dot(act.astype(BF16), wd_bf_ref[...], preferred_element_type=F32) + bd_ref[0]


def _ffn(blk_e, n_act, xs, wgu, bgu, wd, bd, layer):
    ns, d = xs.shape
    blk = BLK_MOE
    d_e2 = wgu.shape[-1]
    d_e = d_e2 // 2

    def rows(b, be, na):
        return (jnp.minimum(b, na[0] - 1), 0)

    def expert4(b, be, na):
        return (layer, be[jnp.minimum(b, na[0] - 1)], 0, 0)

    return pl.pallas_call(
        _ffn_kernel,
        out_shape=jax.ShapeDtypeStruct((ns, d), F32),
        grid_spec=pltpu.PrefetchScalarGridSpec(
            num_scalar_prefetch=2,
            grid=(ns // blk,),
            in_specs=[
                pl.BlockSpec((blk, d), rows),
                pl.BlockSpec((None, 1, d, d_e2), expert4), pl.BlockSpec((None, 1, 1, d_e2), expert4),
                pl.BlockSpec((None, 1, d_e, d), expert4), pl.BlockSpec((None, 1, 1, d), expert4),
            ],
            out_specs=pl.BlockSpec((blk, d), rows),
            scratch_shapes=[pltpu.VMEM((d_e, d), BF16)],
        ),
        compiler_params=_cparams(("arbitrary",)),
        name="moe_ffn",
    )(blk_e, n_act, xs, wgu, bgu, wd, bd)


def _combine_kernel(final, dest_ref, ys_ref, gate_ref, x1_ref, mod_ref, fg_ref, o_ref, buf_ref, sems):
    td = x1_ref.shape[0]
    i = pl.program_id(0)
    slot = i % 2

    def gather(tile, dst_slot):
        base = tile * td * TOP_K

        def issue(t, _):
            for k in range(TOP_K):
                _row_copy(ys_ref, dest_ref[base + t * TOP_K + k], buf_ref.at[dst_slot, k], t,
                          sems.at[dst_slot]).start()
            return 0

        lax.fori_loop(0, td, issue, 0, unroll=DMA_UNROLL)

    @pl.when(i == 0)
    def _():
        gather(0, 0)

    @pl.when(i + 1 < pl.num_programs(0))
    def _():
        gather(i + 1, 1 - slot)

    pltpu.make_async_copy(buf_ref.at[slot], buf_ref.at[slot], sems.at[slot]).wait()

    g = gate_ref[...]
    y = g[:, 0:1] * buf_ref[slot, 0]
    for k in range(1, TOP_K):
        y = y + g[:, k:k + 1] * buf_ref[slot, k]
    x = x1_ref[...] + mod_ref[0, 5:6, :] * y
    if final:
        ms = jnp.mean(x * x, axis=-1, keepdims=True)
        x = x * lax.rsqrt(ms + EPS) * fg_ref[...]
    o_ref[...] = x


def _combine(dest, ys, gate, x1, mod, seg_of_tile, final_g, n_rows):
    d = x1.shape[1]
    td = TD_MOE
    ratio = TM_PROJ // td
    final = final_g is not None
    fg = final_g if final else jnp.ones((1, d), F32)
    return pl.pallas_call(
        functools.partial(_combine_kernel, final),
        out_shape=jax.ShapeDtypeStruct((n_rows, d), F32),
        grid_spec=pltpu.PrefetchScalarGridSpec(
            num_scalar_prefetch=1,
            grid=(n_rows // td,),
            in_specs=[
                pl.BlockSpec(memory_space=pl.ANY),
                pl.BlockSpec((td, LANE), lambda i, dest: (i, 0)),
                pl.BlockSpec((td, d), lambda i, dest: (i, 0)),
                pl.BlockSpec((1, N_MOD, d), lambda i, dest: (seg_of_tile(i // ratio), 0, 0)),
                pl.BlockSpec((1, d), lambda i, dest: (0, 0)),
            ],
            out_specs=pl.BlockSpec((td, d), lambda i, dest: (i, 0)),
            scratch_shapes=[pltpu.VMEM((2, TOP_K, td, d), F32), pltpu.SemaphoreType.DMA((2,))],
        ),
        compiler_params=_cparams(("arbitrary",)),
        name="moe_combine",
    )(dest, ys, gate, x1, mod, fg)


def _rope_tables(n_b, seq, ctx_len):
    t = np.arange(seq)
    axis_dim = HEAD_DIM // 2
    inv = (ROPE_THETA ** (-np.arange(0, axis_dim, 2, dtype=np.float32) / axis_dim)).astype(np.float32)
    pos = np.stack([t // GRID_W, t % GRID_W], axis=-1).astype(np.float32)
    ang = jnp.asarray(pos[:, :, None] * inv[None, None, :])
    cos, sin = jnp.cos(ang), jnp.sin(ang)
    cos_h = jnp.concatenate([cos, cos], axis=-1).reshape(seq, HEAD_DIM)
    sin_h = jnp.concatenate([-sin, sin], axis=-1).reshape(seq, HEAD_DIM)
    cos_x = jnp.tile(cos_h, (n_b, LANE // HEAD_DIM))
    sin_x = jnp.tile(sin_h, (n_b, LANE // HEAD_DIM))
    ones = jnp.ones((n_b * ctx_len, LANE), F32)
    return (jnp.concatenate([cos_x, ones], axis=0),
            jnp.concatenate([sin_x, jnp.zeros_like(ones)], axis=0))


def kernel(x, c, ctx, c_ctx, w_ada, b_ada, norm1_g, norm2_g, w_in, w_out, conv_dw_w, conv_dw_b, conv_ln_g,
           conv_ln_b, q_norm_g, k_norm_g, sc_conv_w, na_rpb, w_router, b_router, w_gate_up, b_gate_up, w_down,
           b_down, final_norm_g):
    n_b, seq, d = x.shape
    ctx_len = ctx.shape[1]
    depth = w_ada.shape[0]
    n_e = w_router.shape[2]
    d_e = w_down.shape[2]
    n_x_rows = n_b * seq
    nt = n_x_rows + n_b * ctx_len
    assert d == D_MODEL and seq % GRID_W == 0 and (seq // GRID_W) % NA_RB == 0
    assert seq % TM_PROJ == 0 and (n_b * ctx_len) % TM_PROJ == 0
    assert seq % TM_CONV == 0 and ctx_len % TM_CONV == 0 and seq % ctx_len == 0
    assert seq % KC_GQA == 0 and nt % TD_MOE == 0 and n_b + 1 <= 8

    tiles_x = n_x_rows // TM_PROJ
    tiles_per_b = seq // TM_PROJ

    def seg_of_tile(i):
        return jnp.where(i < tiles_x, 1 + i // tiles_per_b, 0)

    w_in_b = w_in.astype(BF16)
    w_out_b = w_out.astype(BF16)
    wgu_b = _deinterleave_cast(w_gate_up)
    bgu_p = jnp.concatenate([b_gate_up[..., 0::2], b_gate_up[..., 1::2]], axis=-1)[:, :, None, :]
    bd_p = b_down[:, :, None, :]
    wr_hi = w_router.astype(BF16)
    wr_lo = (w_router - wr_hi.astype(F32)).astype(BF16)
    wr_cat = jnp.concatenate([wr_hi, wr_lo], axis=-1)
    bd = jnp.asarray(np.kron(np.eye(GQA_HEADS, dtype=np.float32),
                             np.full((HEAD_DIM, HEAD_DIM), 1.0 / HEAD_DIM, np.float32))).astype(BF16)
    cos_t, sin_t = _rope_tables(n_b, seq, ctx_len)

    cvec = jnp.zeros((8, d), F32).at[0].set(c_ctx).at[1:1 + n_b].set(c)
    mods = _ada_all(cvec, w_ada, b_ada).reshape(depth, 8, N_MOD, d)

    n_assign = nt * TOP_K
    n_blk = -(-n_assign // BLK_MOE) + n_e

    xs = jnp.concatenate([x.reshape(n_x_rows, d), ctx.reshape(n_b * ctx_len, d)], axis=0)
    cblk0 = n_x_rows // ctx_len

    pending = None
    for l in range(depth):
        mod = mods[l]
        in_args = (mod, norm1_g[l][None], w_in_b, l, cos_t, sin_t,
                   jnp.tile(q_norm_g[l], GQA_HEADS)[None], jnp.tile(k_norm_g[l], GQA_KV_HEADS)[None], bd,
                   seg_of_tile)
        if pending is None:
            pa, q, k, v, ps, nq, nk, nv = _in_proj(xs, *in_args)
        else:
            xs, pa, q, k, v, ps, nq, nk, nv = _in_proj_combine(*pending, *in_args)
        oa, oc = _conv_mix(pa, ps, conv_dw_w[l], conv_dw_b[l][None], conv_ln_g[l][None], conv_ln_b[l][None],
                           sc_conv_w[l], seq, n_x_rows, ctx_len)
        ob = _gqa(q, k, v, n_b, seq, ctx_len)
        ob = _ctx_attn(q, k, v, ob, n_b, ctx_len, cblk0, GQA_HEADS // GQA_KV_HEADS, LANE)
        od = _na(nq, nk, nv, _na_bias_table(na_rpb[l]), n_b, seq, ctx_len)
        od = _ctx_attn(nq, nk, nv, od, n_b, ctx_len, cblk0, 1, HEAD_DIM)
        x1, h2, e_o, pos_o, gate_o, cnt = _out_proj(oa, ob, oc, od, xs, mod, norm2_g[l][None], w_out_b, l,
                                                    wr_cat[l], wr_hi[l], b_router[l][None], seg_of_tile)
        counts = cnt[0]
        padded = (counts + BLK_MOE - 1) // BLK_MOE * BLK_MOE
        pad_end = jnp.cumsum(padded)
        pad_start = pad_end - padded
        dest = (pad_start[e_o[:, :TOP_K]] + pos_o[:, :TOP_K]).reshape(-1).astype(I32)
        blk_lo = jnp.arange(n_blk, dtype=I32) * BLK_MOE
        blk_e = jnp.minimum(jnp.sum((pad_end[None, :] <= blk_lo[:, None]).astype(I32), axis=1), n_e - 1)
        n_act = (pad_end[-1:] // BLK_MOE).astype(I32)

        x_sorted = _dispatch(dest, pad_end.astype(I32), h2, n_blk * BLK_MOE)
        ys = _ffn(blk_e, n_act, x_sorted, wgu_b, bgu_p, w_down, bd_p, l)
        pending = (dest, ys, gate_o, x1, mod)

    out = _combine(*pending, seg_of_tile, final_norm_g[None], n_x_rows)
    return out.reshape(n_b, seq, d)
```

```python
import functools

import jax
import jax.numpy as jnp
import numpy as np
from jax import lax
from jax.experimental import pallas as pl
from jax.experimental.pallas import tpu as pltpu

F32 = jnp.float32
BF16 = jnp.bfloat16
I32 = jnp.int32

D_MODEL = 2048
GRID_W = 64
HEAD_DIM = 64
CONV_W = 512
GQA_HEADS = 8
GQA_KV_HEADS = 2
SC_W = 512
NA_HEADS = 8
GQA_Q_W = GQA_HEADS * HEAD_DIM
GQA_KV_W = GQA_KV_HEADS * HEAD_DIM
NA_W = NA_HEADS * HEAD_DIM
MIX_W = CONV_W + GQA_Q_W + SC_W + NA_W
IN_COLS = 2 * CONV_W + GQA_Q_W + 2 * GQA_KV_W + 3 * SC_W + 3 * NA_W
CONV_K = 31
SC_K = 3
WIN_ROWS = 8
WIN_COLS = 16
ROPE_THETA = 10000.0
ATTN_SCALE = HEAD_DIM ** -0.5
TOP_K = 4
SWIGLU_LIMIT = 7.0
SWIGLU_ALPHA = 1.702
N_MOD = 6
EPS = 1e-6
MASK_VALUE = -1e30
LOG2E = 1.4426950408889634
Q_SCALE = ATTN_SCALE * LOG2E

C_AV, C_GQ = 0, 2 * CONV_W
C_GK = C_GQ + GQA_Q_W
C_GV = C_GK + GQA_KV_W
C_SH = C_GV + GQA_KV_W
C_NQ = C_SH + 3 * SC_W
C_NK = C_NQ + NA_W
C_NV = C_NK + NA_W

VMEM_LIMIT = 56 * 1024 * 1024
LANE = 128

TM_PROJ = 512
TM_CONV = 256
HALO = 16
TQ_GQA = 256
KC_GQA = 512
NA_RB = 8
TD_MOE = 256
BLK_MOE = 256
DMA_UNROLL = 4


def _cparams(sem):
    return pltpu.CompilerParams(dimension_semantics=sem, vmem_limit_bytes=VMEM_LIMIT)


def _resident(shape, index_map):
    return pl.BlockSpec(shape, index_map, pipeline_mode=pl.Buffered(1))


def _split_bf16(a):
    hi = a.astype(BF16)
    lo = (a - hi.astype(F32)).astype(BF16)
    return hi, lo


PREP_COLS = 256
PREP_ROWS = 1024


def _deint_kernel(w_ref, p_ref, o_ref):
    n = w_ref.shape[1]
    half = PREP_COLS // 2
    for b in range(n // PREP_COLS):
        t = jnp.dot(w_ref[:, b * PREP_COLS:(b + 1) * PREP_COLS].astype(BF16), p_ref[...],
                    preferred_element_type=F32).astype(BF16)
        o_ref[:, b * half:(b + 1) * half] = t[:, :half]
        o_ref[:, n // 2 + b * half:n // 2 + (b + 1) * half] = t[:, half:]


def _deinterleave_perm():
    perm = np.zeros((PREP_COLS, PREP_COLS), np.float32)
    j = np.arange(PREP_COLS // 2)
    perm[2 * j, j] = 1.0
    perm[2 * j + 1, PREP_COLS // 2 + j] = 1.0
    return jnp.asarray(perm, BF16)


def _ada_kernel(c_ref, w_ref, b_ref, o_ref):
    c = c_ref[...]
    s = c * jax.nn.sigmoid(c)
    hi, lo = _split_bf16(s)
    lhs = jnp.concatenate([hi, lo], axis=0)
    r = jnp.dot(lhs, w_ref[0].astype(BF16), preferred_element_type=F32)
    o_ref[0] = r[:8] + r[8:] + b_ref[0]


def _ada_all(cvec, w_ada, b_ada):
    n_l, d, n6 = w_ada.shape
    tn = 1024
    return pl.pallas_call(
        _ada_kernel,
        out_shape=jax.ShapeDtypeStruct((n_l, 8, n6), F32),
        grid=(n_l, n6 // tn),
        in_specs=[
            pl.BlockSpec((8, d), lambda l, j: (0, 0)),
            pl.BlockSpec((1, d, tn), lambda l, j: (l, 0, j)),
            pl.BlockSpec((1, 1, tn), lambda l, j: (l, 0, j)),
        ],
        out_specs=pl.BlockSpec((1, 8, tn), lambda l, j: (l, 0, j)),
        compiler_params=_cparams(("parallel", "parallel")),
        name="ada_mod",
    )(cvec, w_ada, b_ada.reshape(n_l, 1, n6))


def _rope(x, cos, sin_signed):
    n = x.shape[1]
    lane = lax.broadcasted_iota(I32, x.shape, 1)
    first = (lane % 32) < 16
    partner = jnp.where(first, pltpu.roll(x, n - 16, 1), pltpu.roll(x, 16, 1))
    return x * cos + partner * sin_signed


def _head_mean_sq(y, bd):
    hi, lo = _split_bf16(y * y)
    return (jnp.dot(hi, bd, preferred_element_type=F32)
            + jnp.dot(lo, bd, preferred_element_type=F32))


def _in_kernel(x_ref, *refs):
    _in_body(x_ref[...], *refs)


def _in_body(x, mod_ref, g_ref, w_ref, cos_ref, sin_ref, qg_ref, kg_ref, bd_ref,
             pa_ref, q_ref, k_ref, v_ref, ps_ref, nq_ref, nk_ref, nv_ref):
    ms = jnp.mean(x * x, axis=-1, keepdims=True)
    y = x * lax.rsqrt(ms + EPS) * g_ref[...]
    shift = mod_ref[0, 0:1, :]
    scale = mod_ref[0, 1:2, :]
    h = (y * (1 + scale) + shift).astype(BF16)

    def proj(lo, hi):
        return jnp.dot(h, w_ref[:, lo:hi], preferred_element_type=F32)

    pa_ref[...] = proj(C_AV, C_GQ).astype(BF16)

    cos = cos_ref[...]
    sin = sin_ref[...]
    gq = proj(C_GQ, C_GK)
    qn = gq * lax.rsqrt(_head_mean_sq(gq, bd_ref[...]) + EPS) * qg_ref[...]
    cos4 = jnp.concatenate([cos] * 4, axis=1)
    sin4 = jnp.concatenate([sin] * 4, axis=1)
    q_ref[...] = (_rope(qn, cos4, sin4) * Q_SCALE).astype(BF16)

    gk = proj(C_GK, C_GV)
    kn = gk * lax.rsqrt(_head_mean_sq(gk, bd_ref[0:GQA_KV_W, 0:GQA_KV_W]) + EPS) * kg_ref[...]
    kr = _rope(kn, cos, sin).astype(BF16)
    hd = HEAD_DIM
    k_ref[...] = jnp.concatenate([kr[:, :hd], kr[:, :hd], kr[:, hd:], kr[:, hd:]], axis=1)
    vv = proj(C_GV, C_SH).astype(BF16)
    ones = jnp.ones((vv.shape[0], hd), BF16)
    v_ref[...] = jnp.concatenate([vv[:, :hd], ones, vv[:, hd:], ones], axis=1)
    ps_ref[...] = proj(C_SH, C_NQ).astype(BF16)
    nq_ref[...] = (proj(C_NQ, C_NK) * Q_SCALE).astype(BF16)
    nk_ref[...] = proj(C_NK, C_NV).astype(BF16)
    nv_ref[...] = proj(C_NV, IN_COLS).astype(BF16)


def _in_proj(xs, mod, g1, w_in, layer, cos_t, sin_t, qg, kg, bd, seg_of_tile):
    nt, d = xs.shape
    tm = TM_PROJ
    row = lambda i: (i, 0)
    const = lambda i: (0, 0)
    widths = (2 * CONV_W, GQA_Q_W, 2 * GQA_KV_W, 2 * GQA_KV_W, 3 * SC_W, NA_W, NA_W, NA_W)
    return pl.pallas_call(
        _in_kernel,
        out_shape=[jax.ShapeDtypeStruct((nt, w), BF16) for w in widths],
        grid=(nt // tm,),
        in_specs=[
            pl.BlockSpec((tm, d), row),
            pl.BlockSpec((1, N_MOD, d), lambda i: (seg_of_tile(i), 0, 0)),
            pl.BlockSpec((1, d), const),
            _resident((None, d, IN_COLS), lambda i: (layer, 0, 0)),
            pl.BlockSpec((tm, LANE), row),
            pl.BlockSpec((tm, LANE), row),
            pl.BlockSpec((1, GQA_Q_W), const),
            pl.BlockSpec((1, GQA_KV_W), const),
            pl.BlockSpec((GQA_Q_W, GQA_Q_W), const),
        ],
        out_specs=[pl.BlockSpec((tm, w), row) for w in widths],
        compiler_params=_cparams(("parallel",)),
        name="in_proj",
    )(xs, mod, g1, w_in, cos_t, sin_t, qg, kg, bd)


def _conv_kernel(seq, n_x_rows, ctx_len,
                 pa_c, pa_p, pa_n, ps_c, ps_p, ps_n, dww_ref, dwb_ref, lng_ref, lnb_ref, scw_ref, wgu_ref, perm_ref,
                 oa_ref, oc_ref, wgu_o_ref, ext_ref, ext2_ref, acc_ref, sh_ref):
    _deint_kernel(wgu_ref, perm_ref, wgu_o_ref)
    tc = TM_CONV
    row0 = pl.program_id(0) * tc
    in_x = row0 < n_x_rows
    rel = jnp.where(in_x, row0, row0 - n_x_rows)
    seg = jnp.where(in_x, seq, ctx_len)
    keep_p = jnp.where(rel % seg == 0, 0.0, 1.0).astype(F32)
    keep_n = jnp.where((rel + tc) % seg == 0, 0.0, 1.0).astype(F32)

    def glu(ref):
        a = ref[...].astype(F32)
        return a[:, :CONV_W] * jax.nn.sigmoid(a[:, CONV_W:])

    ext_ref[0:HALO, :] = glu(pa_p) * keep_p
    ext_ref[HALO:HALO + tc, :] = glu(pa_c)
    ext_ref[HALO + tc:, :] = glu(pa_n) * keep_n

    pad = HALO - CONV_K // 2
    n_sh = sh_ref.shape[1]
    for b in range(8):
        sh_ref[b] = ext_ref[b:b + n_sh, :]
    rc = 64
    for c in range(CONV_W // LANE):
        cs = slice(c * LANE, (c + 1) * LANE)
        for r in range(tc // rc):
            acc = jnp.zeros((rc, LANE), F32)
            for k in range(CONV_K):
                a, b = divmod(k + pad, 8)
                lo = r * rc + 8 * a
                acc = acc + dww_ref[k:k + 1, cs] * sh_ref[b, lo:lo + rc, cs]
            acc_ref[r * rc:(r + 1) * rc, cs] = acc + dwb_ref[:, cs]

    u = acc_ref[...]
    mu = jnp.mean(u, axis=-1, keepdims=True)
    var = jnp.mean(jnp.square(u - mu), axis=-1, keepdims=True)
    yn = (u - mu) * lax.rsqrt(var + EPS) * lng_ref[...] + lnb_ref[...]
    oa_ref[...] = (yn * jax.nn.sigmoid(yn)).astype(BF16)

    def gated(ref, lo, hi):
        a = ref[lo:hi, :].astype(F32)
        return a[:, 2 * SC_W:] * a[:, :SC_W]

    ext2_ref[0:8, :] = gated(ps_p, HALO - 8, HALO) * keep_p
    ext2_ref[8:8 + tc, :] = gated(ps_c, 0, tc)
    ext2_ref[8 + tc:, :] = gated(ps_n, 0, 8) * keep_n
    conv = (scw_ref[0:1, :] * ext2_ref[7:7 + tc, :]
            + scw_ref[1:2, :] * ext2_ref[8:8 + tc, :]
            + scw_ref[2:3, :] * ext2_ref[9:9 + tc, :])
    oc_ref[...] = (ps_c[:, SC_W:2 * SC_W].astype(F32) * conv).astype(BF16)


def _conv_mix(pa, ps, dww, dwb, lng, lnb, scw, w_gate_up, layer, seq, n_x_rows, ctx_len):
    nt = pa.shape[0]
    tc = TM_CONV
    hb = tc // HALO
    n_hb = nt // HALO
    n_steps = nt // tc
    row = lambda i: (i, 0)
    prev = lambda i: (jnp.maximum(i * hb - 1, 0), 0)
    nxt = lambda i: (jnp.minimum((i + 1) * hb, n_hb - 1), 0)
    const = lambda i: (0, 0)
    wa, ws = pa.shape[1], ps.shape[1]
    n_l, n_e, d, n_gu = w_gate_up.shape
    rows_l = n_e * d
    w_tiles = min(rows_l // PREP_ROWS, 1 << (n_steps.bit_length() - 1))
    w_rows = rows_l // w_tiles
    assert rows_l % w_tiles == 0 and w_rows % 8 == 0
    wtile = lambda i: (layer * w_tiles + jnp.minimum(i, w_tiles - 1), 0)
    wtile_o = lambda i: (jnp.minimum(i, w_tiles - 1), 0)
    oa, oc, wgu_b = pl.pallas_call(
        functools.partial(_conv_kernel, seq, n_x_rows, ctx_len),
        out_shape=[jax.ShapeDtypeStruct((nt, CONV_W), BF16), jax.ShapeDtypeStruct((nt, SC_W), BF16),
                   jax.ShapeDtypeStruct((rows_l, n_gu), BF16)],
        grid=(n_steps,),
        in_specs=[
            pl.BlockSpec((tc, wa), row), pl.BlockSpec((HALO, wa), prev), pl.BlockSpec((HALO, wa), nxt),
            pl.BlockSpec((tc, ws), row), pl.BlockSpec((HALO, ws), prev), pl.BlockSpec((HALO, ws), nxt),
            pl.BlockSpec((CONV_K, CONV_W), const), pl.BlockSpec((1, CONV_W), const),
            pl.BlockSpec((1, CONV_W), const), pl.BlockSpec((1, CONV_W), const),
            pl.BlockSpec((SC_K, SC_W), const),
            pl.BlockSpec((w_rows, n_gu), wtile), pl.BlockSpec((PREP_COLS, PREP_COLS), const),
        ],
        out_specs=[pl.BlockSpec((tc, CONV_W), row), pl.BlockSpec((tc, SC_W), row),
                   pl.BlockSpec((w_rows, n_gu), wtile_o)],
        scratch_shapes=[pltpu.VMEM((tc + 2 * HALO, CONV_W), F32),
                        pltpu.VMEM((tc + 16, SC_W), F32),
                        pltpu.VMEM((tc, CONV_W), F32),
                        pltpu.VMEM((8, tc + 2 * HALO - 8, CONV_W), F32)],
        compiler_params=_cparams(("arbitrary",)),
        name="conv_mix",
    )(pa, pa, pa, ps, ps, ps, dww, dwb, lng, lnb, scw, w_gate_up.reshape(n_l * rows_l, n_gu), _deinterleave_perm())
    return oa, oc, wgu_b.reshape(1, n_e, d, n_gu)


def _qk(q, k):
    return lax.dot_general(q, k, (((1,), (1,)), ((), ())), preferred_element_type=F32)


def _softmax_pv(s, v):
    m = jnp.max(s, axis=-1, keepdims=True)
    p = jnp.exp2(s - m)
    l = jnp.sum(p, axis=-1, keepdims=True)
    return jnp.dot(p.astype(BF16), v, preferred_element_type=F32) / l


def _half_masks(rows):
    lane = lax.broadcasted_iota(I32, (rows, LANE), 1)
    lo = lane < HEAD_DIM
    return lo, jnp.logical_not(lo)


def _ctx_attn_kernel(kv_rep, kv_step, q_ref, k_ref, v_ref, o_in_ref, o_ref):
    del o_in_ref
    n_h = q_ref.shape[1] // HEAD_DIM
    outs = []
    for h in range(n_h):
        off = (h // kv_rep) * kv_step
        q = q_ref[:, h * HEAD_DIM:(h + 1) * HEAD_DIM]
        k = k_ref[:, off:off + HEAD_DIM]
        v = v_ref[:, off:off + HEAD_DIM]
        outs.append(_softmax_pv(_qk(q, k), v))
    o_ref[...] = jnp.concatenate(outs, axis=1).astype(BF16)


def _ctx_attn(q, k, v, o_all, n_b, ctx_len, blk0, kv_rep, kv_step):
    kw = k.shape[1]
    qw = q.shape[1]
    rows = lambda b: (blk0 + b, 0)
    return pl.pallas_call(
        functools.partial(_ctx_attn_kernel, kv_rep, kv_step),
        out_shape=jax.ShapeDtypeStruct(o_all.shape, o_all.dtype),
        grid=(n_b,),
        in_specs=[pl.BlockSpec((ctx_len, qw), rows), pl.BlockSpec((ctx_len, kw), rows),
                  pl.BlockSpec((ctx_len, kw), rows), pl.BlockSpec(memory_space=pl.ANY)],
        out_specs=pl.BlockSpec((ctx_len, qw), rows),
        input_output_aliases={3: 0},
        compiler_params=_cparams(("parallel",)),
        name="ctx_attn",
    )(q, k, v, o_all)


def _gqa_kernel(n_chunks, q_ref, kx_ref, vx_ref, kc_ref, vc_ref, o_ref):
    tq = q_ref.shape[0]
    grp = GQA_HEADS // GQA_KV_HEADS
    lo, hi = _half_masks(tq)

    def q_group(g):
        parts = []
        for j in range(grp):
            h = g * grp + j
            blk = q_ref[:, (h // 2) * LANE:(h // 2 + 1) * LANE].astype(F32)
            parts.append(jnp.where(lo if h % 2 == 0 else hi, blk, 0.0).astype(BF16))
        return jnp.concatenate(parts, axis=0)

    def update(q, k, v, m, acc):
        s = _qk(q, k)
        m_new = jnp.maximum(m, jnp.max(s, axis=-1, keepdims=True))
        p = jnp.exp2(s - m_new)
        acc = jnp.exp2(m - m_new) * acc + jnp.dot(p.astype(BF16), v, preferred_element_type=F32)
        return m_new, acc

    qs = [q_group(g) for g in range(GQA_KV_HEADS)]
    gsl = [slice(g * LANE, (g + 1) * LANE) for g in range(GQA_KV_HEADS)]
    state = []
    for g in range(GQA_KV_HEADS):
        s = _qk(qs[g], kc_ref[:, gsl[g]])
        m = jnp.max(s, axis=-1, keepdims=True)
        p = jnp.exp2(s - m)
        state += [m, jnp.dot(p.astype(BF16), vc_ref[:, gsl[g]], preferred_element_type=F32)]

    def body(c, carry):
        rows = pl.ds(pl.multiple_of(c * KC_GQA, KC_GQA), KC_GQA)
        out = []
        for g in range(GQA_KV_HEADS):
            out += update(qs[g], kx_ref[rows, gsl[g]], vx_ref[rows, gsl[g]], carry[2 * g], carry[2 * g + 1])
        return tuple(out)

    state = lax.fori_loop(0, n_chunks, body, tuple(state), unroll=4)

    lo4, _ = _half_masks(grp * tq)
    heads = []
    for g in range(GQA_KV_HEADS):
        acc = state[2 * g + 1]
        inv = 1.0 / jnp.where(lo4, 1.0, acc)
        o = acc * pltpu.roll(inv, HEAD_DIM, 1)
        heads.extend(o[j * tq:(j + 1) * tq] for j in range(grp))
    pairs = [jnp.where(lo, heads[2 * i], pltpu.roll(heads[2 * i + 1], HEAD_DIM, 1))
             for i in range(GQA_HEADS // 2)]
    o_ref[...] = jnp.concatenate(pairs, axis=1).astype(BF16)


def _gqa(q, k, v, n_b, seq, ctx_len):
    nt = q.shape[0]
    tq = TQ_GQA
    nq = seq // tq
    cblk0 = n_b * seq // ctx_len
    kw = k.shape[1]
    return pl.pallas_call(
        functools.partial(_gqa_kernel, seq // KC_GQA),
        out_shape=jax.ShapeDtypeStruct((nt, GQA_Q_W), BF16),
        grid=(n_b, nq),
        in_specs=[
            pl.BlockSpec((tq, GQA_Q_W), lambda b, j: (b * nq + j, 0)),
            pl.BlockSpec((seq, kw), lambda b, j: (b, 0)),
            pl.BlockSpec((seq, kw), lambda b, j: (b, 0)),
            pl.BlockSpec((ctx_len, kw), lambda b, j: (cblk0 + b, 0)),
            pl.BlockSpec((ctx_len, kw), lambda b, j: (cblk0 + b, 0)),
        ],
        out_specs=pl.BlockSpec((tq, GQA_Q_W), lambda b, j: (b * nq + j, 0)),
        compiler_params=_cparams(("parallel", "parallel")),
        name="gqa_flash",
    )(q, k, v, k, v)


def _na_kernel(n_rows, q_ref, k_ref, v_ref, kc_ref, vc_ref, t_ref, o_ref):
    rb = pl.program_id(2)
    win = WIN_ROWS * GRID_W
    tq = q_ref.shape[0]
    lo, hi = _half_masks(tq)
    qf = q_ref[...].astype(F32)
    kc = kc_ref[...]
    vc = vc_ref[...]
    halves = []
    for j in range(LANE // HEAD_DIM):
        qa = jnp.where(lo if j == 0 else hi, qf, 0.0).astype(BF16)
        s_ctx = _qk(qa, kc)
        m_ctx = jnp.max(s_ctx, axis=-1, keepdims=True)
        s_wins, ms, krows_l = [], [], []
        for i in range(NA_RB):
            r = rb * NA_RB + i
            r0 = jnp.clip(r - WIN_ROWS // 2, 0, n_rows - WIN_ROWS)
            shift = r0 - r + (WIN_ROWS - 1)
            krows = pl.ds(pl.multiple_of(r0 * GRID_W, GRID_W), win)
            rs = slice(i * GRID_W, (i + 1) * GRID_W)
            bias = jnp.concatenate([t_ref[j, 2 * wp + shift] for wp in range(WIN_ROWS // 2)], axis=1)
            s_win = _qk(qa[rs], k_ref[krows, :]) + bias
            s_wins.append(s_win)
            ms.append(jnp.maximum(jnp.max(s_win, axis=-1, keepdims=True), m_ctx[rs]))
            krows_l.append(krows)
        m_all = jnp.concatenate(ms, axis=0)
        p_ctx = jnp.exp2(s_ctx - m_all)
        l_ctx = jnp.sum(p_ctx, axis=-1, keepdims=True)
        o_ctx = jnp.dot(p_ctx.astype(BF16), vc, preferred_element_type=F32)
        outs = []
        for i in range(NA_RB):
            rs = slice(i * GRID_W, (i + 1) * GRID_W)
            p = jnp.exp2(s_wins[i] - ms[i])
            l = jnp.sum(p, axis=-1, keepdims=True) + l_ctx[rs]
            o = jnp.dot(p.astype(BF16), v_ref[krows_l[i], :], preferred_element_type=F32) + o_ctx[rs]
            outs.append(o / l)
        halves.append(jnp.concatenate(outs, axis=0))
    o_ref[...] = jnp.where(lo, halves[0], halves[1]).astype(BF16)


def _na(nq, nk, nv, table, n_b, seq, ctx_len):
    nt = nq.shape[0]
    n_rows = seq // GRID_W
    n_rb = n_rows // NA_RB
    tq = NA_RB * GRID_W
    cblk0 = n_b * seq // ctx_len
    hp = LANE // HEAD_DIM
    return pl.pallas_call(
        functools.partial(_na_kernel, n_rows),
        out_shape=jax.ShapeDtypeStruct((nt, NA_W), BF16),
        grid=(n_b, NA_W // LANE, n_rb),
        in_specs=[
            pl.BlockSpec((tq, LANE), lambda b, h, r: (b * n_rb + r, h)),
            pl.BlockSpec((seq, LANE), lambda b, h, r: (b, h)),
            pl.BlockSpec((seq, LANE), lambda b, h, r: (b, h)),
            pl.BlockSpec((ctx_len, LANE), lambda b, h, r: (cblk0 + b, h)),
            pl.BlockSpec((ctx_len, LANE), lambda b, h, r: (cblk0 + b, h)),
            pl.BlockSpec((hp, 2 * WIN_ROWS - 2, GRID_W, LANE), lambda b, h, r: (h, 0, 0, 0)),
        ],
        out_specs=pl.BlockSpec((tq, LANE), lambda b, h, r: (b * n_rb + r, h)),
        compiler_params=_cparams(("parallel", "parallel", "parallel")),
        name="na_attn",
    )(nq, nk, nv, nk, nv, table)


def _na_bias_table(rpb):
    cols = np.arange(GRID_W)
    c0 = np.clip(cols - WIN_COLS // 2, 0, GRID_W - WIN_COLS)
    kc = np.arange(GRID_W)
    rel = kc[None, :] - cols[:, None] + (WIN_COLS - 1)
    valid = (kc[None, :] >= c0[:, None]) & (kc[None, :] < c0[:, None] + WIN_COLS)
    rel = np.clip(rel, 0, 2 * WIN_COLS - 2)
    t = rpb[:, :, rel]
    t = jnp.where(jnp.asarray(valid)[None, None], t * LOG2E, MASK_VALUE).astype(F32)
    return jnp.concatenate([t[:, :-1], t[:, 1:]], axis=-1)


def _out_kernel(oa_ref, ob_ref, oc_ref, od_ref, x_ref, mod_ref, g_ref, w_ref, wr_cat_ref, wr_hi_ref, br_ref,
                x1_ref, h2_ref, e_ref, pos_ref, gate_ref, cnt_ref, carry_ref):
    acc = jnp.dot(oa_ref[...], w_ref[0:CONV_W, :], preferred_element_type=F32)
    acc = acc + jnp.dot(ob_ref[...], w_ref[CONV_W:CONV_W + GQA_Q_W, :], preferred_element_type=F32)
    acc = acc + jnp.dot(oc_ref[...], w_ref[CONV_W + GQA_Q_W:MIX_W - NA_W, :], preferred_element_type=F32)
    acc = acc + jnp.dot(od_ref[...], w_ref[MIX_W - NA_W:MIX_W, :], preferred_element_type=F32)
    x1 = x_ref[...] + mod_ref[0, 2:3, :] * acc
    x1_ref[...] = x1
    ms = jnp.mean(x1 * x1, axis=-1, keepdims=True)
    y = x1 * lax.rsqrt(ms + EPS) * g_ref[...]
    h2 = y * (1 + mod_ref[0, 4:5, :]) + mod_ref[0, 3:4, :]
    h2_ref[...] = h2
    hi, lo = _split_bf16(h2)
    n_e = br_ref.shape[1]
    a = jnp.dot(hi, wr_cat_ref[...], preferred_element_type=F32)
    b = jnp.dot(lo, wr_hi_ref[...], preferred_element_type=F32)
    logits = a[:, :n_e] + a[:, n_e:] + b + br_ref[...]
    _route_tile(logits, e_ref, pos_ref, gate_ref, cnt_ref, carry_ref)


def _out_proj(oa, ob, oc, od, xs, mod, g2, w_out, layer, wr_cat, wr_hi, br, seg_of_tile):
    nt, d = xs.shape
    tm = TM_PROJ
    n_e = wr_hi.shape[1]
    row = lambda i: (i, 0)
    const = lambda i: (0, 0)
    return pl.pallas_call(
        _out_kernel,
        out_shape=[jax.ShapeDtypeStruct((nt, d), F32), jax.ShapeDtypeStruct((nt, d), F32),
                   jax.ShapeDtypeStruct((nt, LANE), I32), jax.ShapeDtypeStruct((nt, LANE), I32),
                   jax.ShapeDtypeStruct((nt, LANE), F32), jax.ShapeDtypeStruct((1, n_e), I32)],
        grid=(nt // tm,),
        in_specs=[
            pl.BlockSpec((tm, CONV_W), row), pl.BlockSpec((tm, GQA_Q_W), row),
            pl.BlockSpec((tm, SC_W), row), pl.BlockSpec((tm, NA_W), row),
            pl.BlockSpec((tm, d), row),
            pl.BlockSpec((1, N_MOD, d), lambda i: (seg_of_tile(i), 0, 0)),
            pl.BlockSpec((1, d), const),
            _resident((None, MIX_W, d), lambda i: (layer, 0, 0)),
            pl.BlockSpec((d, 2 * n_e), const), pl.BlockSpec((d, n_e), const), pl.BlockSpec((1, n_e), const),
        ],
        out_specs=[pl.BlockSpec((tm, d), row), pl.BlockSpec((tm, d), row),
                   pl.BlockSpec((tm, LANE), row), pl.BlockSpec((tm, LANE), row), pl.BlockSpec((tm, LANE), row),
                   pl.BlockSpec((1, n_e), const)],
        scratch_shapes=[pltpu.VMEM((1, n_e), F32)],
        compiler_params=_cparams(("arbitrary",)),
        name="out_proj",
    )(oa, ob, oc, od, xs, mod, g2, w_out, wr_cat, wr_hi, br)


def _route_tile(lg, e_ref, pos_ref, gate_ref, cnt_ref, carry_ref):
    i = pl.program_id(0)

    @pl.when(i == 0)
    def _():
        carry_ref[...] = jnp.zeros_like(carry_ref)

    tr, n_e = lg.shape
    lane = lax.broadcasted_iota(I32, (tr, n_e), 1)
    work = lg
    vals, idxs = [], []
    for _ in range(TOP_K):
        m = jnp.max(work, axis=-1, keepdims=True)
        idx = jnp.min(jnp.where(work == m, lane, n_e), axis=-1, keepdims=True)
        vals.append(m)
        idxs.append(idx)
        work = jnp.where(lane == idx, -jnp.inf, work)
    exps = [jnp.exp(v - vals[0]) for v in vals]
    den = exps[0] + exps[1] + exps[2] + exps[3]

    mask = jnp.zeros((tr, n_e), F32)
    for idx in idxs:
        mask = mask + jnp.where(lane == idx, 1.0, 0.0)
    r_i = lax.broadcasted_iota(I32, (tr, tr), 0)
    c_i = lax.broadcasted_iota(I32, (tr, tr), 1)
    tri = jnp.where(c_i <= r_i, 1.0, 0.0).astype(BF16)
    incl = jnp.dot(tri, mask.astype(BF16), preferred_element_type=F32)
    before = carry_ref[...] + incl - mask

    lane_o = lax.broadcasted_iota(I32, (tr, LANE), 1)
    e_out = jnp.zeros((tr, LANE), I32)
    pos_out = jnp.zeros((tr, LANE), I32)
    gate_out = jnp.zeros((tr, LANE), F32)
    for k in range(TOP_K):
        pos_k = jnp.sum(jnp.where(lane == idxs[k], before, 0.0), axis=-1, keepdims=True)
        e_out = jnp.where(lane_o == k, idxs[k], e_out)
        pos_out = jnp.where(lane_o == k, pos_k.astype(I32), pos_out)
        gate_out = jnp.where(lane_o == k, exps[k] / den, gate_out)
    e_ref[...] = e_out
    pos_ref[...] = pos_out
    gate_ref[...] = gate_out
    total = carry_ref[...] + incl[tr - 1:tr, :]
    carry_ref[...] = total
    cnt_ref[...] = total.astype(I32)


def _row_copy(src_ref, src_row, dst_ref, dst_row, sem):
    return pltpu.make_async_copy(src_ref.at[pl.ds(src_row, 1)], dst_ref.at[pl.ds(dst_row, 1)], sem)


def _dispatch_kernel(dest_ref, pad_end_ref, h_ref, xs_ref, zbuf_ref, sem, zsem):
    td = h_ref.shape[0]
    blk = zbuf_ref.shape[0]
    n_e = pad_end_ref.shape[0]
    base = pl.program_id(0) * td * TOP_K

    @pl.when(pl.program_id(0) == 0)
    def _():
        zbuf_ref[...] = jnp.zeros_like(zbuf_ref)

        def fill(e):
            start = pl.multiple_of(pad_end_ref[e] - blk, blk)
            return pltpu.make_async_copy(zbuf_ref, xs_ref.at[pl.ds(start, blk)], zsem)

        def nonempty(e):
            prev = pad_end_ref[jnp.maximum(e - 1, 0)]
            return pad_end_ref[e] > jnp.where(e == 0, 0, prev)

        def start(e, _):
            @pl.when(nonempty(e))
            def _():
                fill(e).start()
            return 0

        def wait(e, _):
            @pl.when(nonempty(e))
            def _():
                fill(e).wait()
            return 0

        lax.fori_loop(0, n_e, start, 0)
        lax.fori_loop(0, n_e, wait, 0)

    def issue(t, _):
        for k in range(TOP_K):
            _row_copy(h_ref, t, xs_ref, dest_ref[base + t * TOP_K + k], sem).start()
        return 0

    lax.fori_loop(0, td, issue, 0, unroll=DMA_UNROLL)
    n_rows = td * TOP_K
    pltpu.make_async_copy(xs_ref.at[pl.ds(0, n_rows)], xs_ref.at[pl.ds(0, n_rows)], sem).wait()


def _dispatch(dest, pad_end, h2, n_slots):
    nt, d = h2.shape
    td = TD_MOE
    return pl.pallas_call(
        _dispatch_kernel,
        out_shape=jax.ShapeDtypeStruct((n_slots, d), F32),
        grid_spec=pltpu.PrefetchScalarGridSpec(
            num_scalar_prefetch=2,
            grid=(nt // td,),
            in_specs=[pl.BlockSpec((td, d), lambda i, dest, pe: (i, 0))],
            out_specs=pl.BlockSpec(memory_space=pl.ANY),
            scratch_shapes=[pltpu.VMEM((BLK_MOE, d), F32), pltpu.SemaphoreType.DMA, pltpu.SemaphoreType.DMA],
        ),
        compiler_params=_cparams(("arbitrary",)),
        name="moe_dispatch",
    )(dest, pad_end, h2)


def _ffn_kernel(blk_e_ref, n_act_ref, x_ref, wgu_ref, bgu_ref, wd_ref, bd_ref, y_ref, wd_bf_ref):
    b = pl.program_id(0)

    @pl.when(b < n_act_ref[0])
    def _():
        @pl.when(jnp.logical_or(b == 0, blk_e_ref[b] != blk_e_ref[jnp.maximum(b - 1, 0)]))
        def _():
            wd_bf_ref[...] = wd_ref[0].astype(BF16)

        d_e = wd_ref.shape[1]
        xb = x_ref[...].astype(BF16)
        h = jnp.dot(xb, wgu_ref[0], preferred_element_type=F32) + bgu_ref[0]
        glu = jnp.minimum(h[:, :d_e], SWIGLU_LIMIT)
        lin = jnp.clip(h[:, d_e:], -SWIGLU_LIMIT, SWIGLU_LIMIT)
        act = glu * jax.nn.sigmoid(SWIGLU_ALPHA * glu) * (lin + 1)
        y_ref[...] = jnp.dot(act.astype(BF16), wd_bf_ref[...], preferred_element_type=F32) + bd_ref[0]


def _ffn(blk_e, n_act, xs, wgu, bgu, wd, bd, layer):
    ns, d = xs.shape
    blk = BLK_MOE
    d_e2 = wgu.shape[-1]
    d_e = d_e2 // 2

    def rows(b, be, na):
        return (jnp.minimum(b, na[0] - 1), 0)

    def expert4(b, be, na):
        return (layer, be[jnp.minimum(b, na[0] - 1)], 0, 0)

    def expert4_gu(b, be, na):
        return (0, be[jnp.minimum(b, na[0] - 1)], 0, 0)

    return pl.pallas_call(
        _ffn_kernel,
        out_shape=jax.ShapeDtypeStruct((ns, d), F32),
        grid_spec=pltpu.PrefetchScalarGridSpec(
            num_scalar_prefetch=2,
            grid=(ns // blk,),
            in_specs=[
                pl.BlockSpec((blk, d), rows),
                pl.BlockSpec((None, 1, d, d_e2), expert4_gu), pl.BlockSpec((None, 1, 1, d_e2), expert4),
                pl.BlockSpec((None, 1, d_e, d), expert4), pl.BlockSpec((None, 1, 1, d), expert4),
            ],
            out_specs=pl.BlockSpec((blk, d), rows),
            scratch_shapes=[pltpu.VMEM((d_e, d), BF16)],
        ),
        compiler_params=_cparams(("arbitrary",)),
        name="moe_ffn",
    )(blk_e, n_act, xs, wgu, bgu, wd, bd)


def _combine_kernel(final, dest_ref, ys_ref, gate_ref, x1_ref, mod_ref, fg_ref, o_ref, buf_ref, sems):
    td = x1_ref.shape[0]
    i = pl.program_id(0)
    slot = i % 2

    def gather(tile, dst_slot):
        base = tile * td * TOP_K

        def issue(t, _):
            for k in range(TOP_K):
                _row_copy(ys_ref, dest_ref[base + t * TOP_K + k], buf_ref.at[dst_slot, k], t,
                          sems.at[dst_slot]).start()
            return 0

        lax.fori_loop(0, td, issue, 0, unroll=DMA_UNROLL)

    @pl.when(i == 0)
    def _():
        gather(0, 0)

    @pl.when(i + 1 < pl.num_programs(0))
    def _():
        gather(i + 1, 1 - slot)

    pltpu.make_async_copy(buf_ref.at[slot], buf_ref.at[slot], sems.at[slot]).wait()

    g = gate_ref[...]
    y = g[:, 0:1] * buf_ref[slot, 0]
    for k in range(1, TOP_K):
        y = y + g[:, k:k + 1] * buf_ref[slot, k]
    x = x1_ref[...] + mod_ref[0, 5:6, :] * y
    if final:
        ms = jnp.mean(x * x, axis=-1, keepdims=True)
        x = x * lax.rsqrt(ms + EPS) * fg_ref[...]
    o_ref[...] = x


def _combine(dest, ys, gate, x1, mod, seg_of_tile, final_g, n_rows):
    d = x1.shape[1]
    td = TD_MOE
    ratio = TM_PROJ // td
    final = final_g is not None
    fg = final_g if final else jnp.ones((1, d), F32)
    return pl.pallas_call(
        functools.partial(_combine_kernel, final),
        out_shape=jax.ShapeDtypeStruct((n_rows, d), F32),
        grid_spec=pltpu.PrefetchScalarGridSpec(
            num_scalar_prefetch=1,
            grid=(n_rows // td,),
            in_specs=[
                pl.BlockSpec(memory_space=pl.ANY),
                pl.BlockSpec((td, LANE), lambda i, dest: (i, 0)),
                pl.BlockSpec((td, d), lambda i, dest: (i, 0)),
                pl.BlockSpec((1, N_MOD, d), lambda i, dest: (seg_of_tile(i // ratio), 0, 0)),
                pl.BlockSpec((1, d), lambda i, dest: (0, 0)),
            ],
            out_specs=pl.BlockSpec((td, d), lambda i, dest: (i, 0)),
            scratch_shapes=[pltpu.VMEM((2, TOP_K, td, d), F32), pltpu.SemaphoreType.DMA((2,))],
        ),
        compiler_params=_cparams(("arbitrary",)),
        name="moe_combine",
    )(dest, ys, gate, x1, mod, fg)


def _rope_tables(n_b, seq, ctx_len):
    t = np.arange(seq)
    axis_dim = HEAD_DIM // 2
    inv = (ROPE_THETA ** (-np.arange(0, axis_dim, 2, dtype=np.float32) / axis_dim)).astype(np.float32)
    pos = np.stack([t // GRID_W, t % GRID_W], axis=-1).astype(np.float32)
    ang = jnp.asarray(pos[:, :, None] * inv[None, None, :])
    cos, sin = jnp.cos(ang), jnp.sin(ang)
    cos_h = jnp.concatenate([cos, cos], axis=-1).reshape(seq, HEAD_DIM)
    sin_h = jnp.concatenate([-sin, sin], axis=-1).reshape(seq, HEAD_DIM)
    cos_x = jnp.tile(cos_h, (n_b, LANE // HEAD_DIM))
    sin_x = jnp.tile(sin_h, (n_b, LANE // HEAD_DIM))
    ones = jnp.ones((n_b * ctx_len, LANE), F32)
    return (jnp.concatenate([cos_x, ones], axis=0),
            jnp.concatenate([sin_x, jnp.zeros_like(ones)], axis=0))


def kernel(x, c, ctx, c_ctx, w_ada, b_ada, norm1_g, norm2_g, w_in, w_out, conv_dw_w, conv_dw_b, conv_ln_g,
           conv_ln_b, q_norm_g, k_norm_g, sc_conv_w, na_rpb, w_router, b_router, w_gate_up, b_gate_up, w_down,
           b_down, final_norm_g):
    n_b, seq, d = x.shape
    ctx_len = ctx.shape[1]
    depth = w_ada.shape[0]
    n_e = w_router.shape[2]
    d_e = w_down.shape[2]
    n_x_rows = n_b * seq
    nt = n_x_rows + n_b * ctx_len
    assert d == D_MODEL and seq % GRID_W == 0 and (seq // GRID_W) % NA_RB == 0
    assert seq % TM_PROJ == 0 and (n_b * ctx_len) % TM_PROJ == 0
    assert seq % TM_CONV == 0 and ctx_len % TM_CONV == 0 and seq % ctx_len == 0
    assert seq % KC_GQA == 0 and nt % TD_MOE == 0 and n_b + 1 <= 8

    tiles_x = n_x_rows // TM_PROJ
    tiles_per_b = seq // TM_PROJ

    def seg_of_tile(i):
        return jnp.where(i < tiles_x, 1 + i // tiles_per_b, 0)

    w_in_b = w_in.astype(BF16)
    w_out_b = w_out.astype(BF16)
    bgu_p = jnp.concatenate([b_gate_up[..., 0::2], b_gate_up[..., 1::2]], axis=-1)[:, :, None, :]
    bd_p = b_down[:, :, None, :]
    wr_hi = w_router.astype(BF16)
    wr_lo = (w_router - wr_hi.astype(F32)).astype(BF16)
    wr_cat = jnp.concatenate([wr_hi, wr_lo], axis=-1)
    bd = jnp.asarray(np.kron(np.eye(GQA_HEADS, dtype=np.float32),
                             np.full((HEAD_DIM, HEAD_DIM), 1.0 / HEAD_DIM, np.float32))).astype(BF16)
    cos_t, sin_t = _rope_tables(n_b, seq, ctx_len)

    cvec = jnp.zeros((8, d), F32).at[0].set(c_ctx).at[1:1 + n_b].set(c)
    mods = _ada_all(cvec, w_ada, b_ada).reshape(depth, 8, N_MOD, d)

    n_assign = nt * TOP_K
    n_blk = -(-n_assign // BLK_MOE) + n_e

    xs = jnp.concatenate([x.reshape(n_x_rows, d), ctx.reshape(n_b * ctx_len, d)], axis=0)
    cblk0 = n_x_rows // ctx_len

    pending = None
    for l in range(depth):
        mod = mods[l]
        in_args = (mod, norm1_g[l][None], w_in_b, l, cos_t, sin_t,
                   jnp.tile(q_norm_g[l], GQA_HEADS)[None], jnp.tile(k_norm_g[l], GQA_KV_HEADS)[None], bd,
                   seg_of_tile)
        if pending is not None:
            xs = _combine(*pending, seg_of_tile, None, nt)
        pa, q, k, v, ps, nq, nk, nv = _in_proj(xs, *in_args)
        oa, oc, wgu_b = _conv_mix(pa, ps, conv_dw_w[l], conv_dw_b[l][None], conv_ln_g[l][None],
                                  conv_ln_b[l][None], sc_conv_w[l], w_gate_up, l, seq, n_x_rows, ctx_len)
        ob = _gqa(q, k, v, n_b, seq, ctx_len)
        ob = _ctx_attn(q, k, v, ob, n_b, ctx_len, cblk0, GQA_HEADS // GQA_KV_HEADS, LANE)
        od = _na(nq, nk, nv, _na_bias_table(na_rpb[l]), n_b, seq, ctx_len)
        od = _ctx_attn(nq, nk, nv, od, n_b, ctx_len, cblk0, 1, HEAD_DIM)
        x1, h2, e_o, pos_o, gate_o, cnt = _out_proj(oa, ob, oc, od, xs, mod, norm2_g[l][None], w_out_b, l,
                                                    wr_cat[l], wr_hi[l], b_router[l][None], seg_of_tile)
        counts = cnt[0]
        padded = (counts + BLK_MOE - 1) // BLK_MOE * BLK_MOE
        pad_end = jnp.cumsum(padded)
        pad_start = pad_end - padded
        dest = (pad_start[e_o[:, :TOP_K]] + pos_o[:, :TOP_K]).reshape(-1).astype(I32)
        blk_lo = jnp.arange(n_blk, dtype=I32) * BLK_MOE
        blk_e = jnp.minimum(jnp.sum((pad_end[None, :] <= blk_lo[:, None]).astype(I32), axis=1), n_e - 1)
        n_act = (pad_end[-1:] // BLK_MOE).astype(I32)

        x_sorted = _dispatch(dest, pad_end.astype(I32), h2, n_blk * BLK_MOE)
        ys = _ffn(blk_e, n_act, x_sorted, wgu_b, bgu_p, w_down, bd_p, l)
        pending = (dest, ys, gate_o, x1, mod)

    out = _combine(*pending, seg_of_tile, final_norm_g[None], n_x_rows)
    return out.reshape(n_b, seq, d)
```

```python
import functools

import jax
import jax.numpy as jnp
import numpy as np
from jax import lax
from jax.experimental import pallas as pl
from jax.experimental.pallas import tpu as pltpu

F32 = jnp.float32
BF16 = jnp.bfloat16
I32 = jnp.int32

D_MODEL = 2048
GRID_W = 64
HEAD_DIM = 64
CONV_W = 512
GQA_HEADS = 8
GQA_KV_HEADS = 2
SC_W = 512
NA_HEADS = 8
GQA_Q_W = GQA_HEADS * HEAD_DIM
GQA_KV_W = GQA_KV_HEADS * HEAD_DIM
NA_W = NA_HEADS * HEAD_DIM
MIX_W = CONV_W + GQA_Q_W + SC_W + NA_W
IN_COLS = 2 * CONV_W + GQA_Q_W + 2 * GQA_KV_W + 3 * SC_W + 3 * NA_W
CONV_K = 31
SC_K = 3
WIN_ROWS = 8
WIN_COLS = 16
ROPE_THETA = 10000.0
ATTN_SCALE = HEAD_DIM ** -0.5
TOP_K = 4
SWIGLU_LIMIT = 7.0
SWIGLU_ALPHA = 1.702
N_MOD = 6
EPS = 1e-6
MASK_VALUE = -1e30
LOG2E = 1.4426950408889634
Q_SCALE = ATTN_SCALE * LOG2E

C_AV, C_GQ = 0, 2 * CONV_W
C_GK = C_GQ + GQA_Q_W
C_GV = C_GK + GQA_KV_W
C_SH = C_GV + GQA_KV_W
C_NQ = C_SH + 3 * SC_W
C_NK = C_NQ + NA_W
C_NV = C_NK + NA_W

VMEM_LIMIT = 56 * 1024 * 1024
LANE = 128

TM_PROJ = 512
TM_CONV = 256
HALO = 16
TQ_GQA = 256
KC_GQA = 512
NA_RB = 8
TD_MOE = 256
BLK_MOE = 256
DMA_UNROLL = 4


def _cparams(sem):
    return pltpu.CompilerParams(dimension_semantics=sem, vmem_limit_bytes=VMEM_LIMIT)


def _resident(shape, index_map):
    return pl.BlockSpec(shape, index_map, pipeline_mode=pl.Buffered(1))


def _split_bf16(a):
    hi = a.astype(BF16)
    lo = (a - hi.astype(F32)).astype(BF16)
    return hi, lo


PREP_COLS = 256
PREP_ROWS = 1024


def _deint_kernel(w_ref, p_ref, o_ref):
    n = w_ref.shape[1]
    half = PREP_COLS // 2
    for b in range(n // PREP_COLS):
        t = jnp.dot(w_ref[:, b * PREP_COLS:(b + 1) * PREP_COLS].astype(BF16), p_ref[...],
                    preferred_element_type=F32).astype(BF16)
        o_ref[:, b * half:(b + 1) * half] = t[:, :half]
        o_ref[:, n // 2 + b * half:n // 2 + (b + 1) * half] = t[:, half:]


def _deinterleave_perm():
    perm = np.zeros((PREP_COLS, PREP_COLS), np.float32)
    j = np.arange(PREP_COLS // 2)
    perm[2 * j, j] = 1.0
    perm[2 * j + 1, PREP_COLS // 2 + j] = 1.0
    return jnp.asarray(perm, BF16)


def _ada_kernel(c_ref, w_ref, b_ref, o_ref):
    c = c_ref[...]
    s = c * jax.nn.sigmoid(c)
    hi, lo = _split_bf16(s)
    lhs = jnp.concatenate([hi, lo], axis=0)
    r = jnp.dot(lhs, w_ref[0].astype(BF16), preferred_element_type=F32)
    o_ref[0] = r[:8] + r[8:] + b_ref[0]


def _ada_all(cvec, w_ada, b_ada):
    n_l, d, n6 = w_ada.shape
    tn = 1024
    return pl.pallas_call(
        _ada_kernel,
        out_shape=jax.ShapeDtypeStruct((n_l, 8, n6), F32),
        grid=(n_l, n6 // tn),
        in_specs=[
            pl.BlockSpec((8, d), lambda l, j: (0, 0)),
            pl.BlockSpec((1, d, tn), lambda l, j: (l, 0, j)),
            pl.BlockSpec((1, 1, tn), lambda l, j: (l, 0, j)),
        ],
        out_specs=pl.BlockSpec((1, 8, tn), lambda l, j: (l, 0, j)),
        compiler_params=_cparams(("parallel", "parallel")),
        name="ada_mod",
    )(cvec, w_ada, b_ada.reshape(n_l, 1, n6))


def _rope(x, cos, sin_signed):
    n = x.shape[1]
    lane = lax.broadcasted_iota(I32, x.shape, 1)
    first = (lane % 32) < 16
    partner = jnp.where(first, pltpu.roll(x, n - 16, 1), pltpu.roll(x, 16, 1))
    return x * cos + partner * sin_signed


def _head_mean_sq(y, bd):
    hi, lo = _split_bf16(y * y)
    return (jnp.dot(hi, bd, preferred_element_type=F32)
            + jnp.dot(lo, bd, preferred_element_type=F32))


def _in_kernel(x_ref, *refs):
    _in_body(x_ref[...], *refs)


def _in_body(x, mod_ref, g_ref, w_ref, cos_ref, sin_ref, qg_ref, kg_ref, bd_ref,
             pa_ref, q_ref, k_ref, v_ref, ps_ref, nq_ref, nk_ref, nv_ref):
    ms = jnp.mean(x * x, axis=-1, keepdims=True)
    y = x * lax.rsqrt(ms + EPS) * g_ref[...]
    shift = mod_ref[0, 0:1, :]
    scale = mod_ref[0, 1:2, :]
    h = (y * (1 + scale) + shift).astype(BF16)

    def proj(lo, hi):
        return jnp.dot(h, w_ref[:, lo:hi], preferred_element_type=F32)

    pa_ref[...] = proj(C_AV, C_GQ).astype(BF16)

    cos = cos_ref[...]
    sin = sin_ref[...]
    gq = proj(C_GQ, C_GK)
    qn = gq * lax.rsqrt(_head_mean_sq(gq, bd_ref[...]) + EPS) * qg_ref[...]
    cos4 = jnp.concatenate([cos] * 4, axis=1)
    sin4 = jnp.concatenate([sin] * 4, axis=1)
    q_ref[...] = (_rope(qn, cos4, sin4) * Q_SCALE).astype(BF16)

    gk = proj(C_GK, C_GV)
    kn = gk * lax.rsqrt(_head_mean_sq(gk, bd_ref[0:GQA_KV_W, 0:GQA_KV_W]) + EPS) * kg_ref[...]
    kr = _rope(kn, cos, sin).astype(BF16)
    hd = HEAD_DIM
    k_ref[...] = jnp.concatenate([kr[:, :hd], kr[:, :hd], kr[:, hd:], kr[:, hd:]], axis=1)
    vv = proj(C_GV, C_SH).astype(BF16)
    ones = jnp.ones((vv.shape[0], hd), BF16)
    v_ref[...] = jnp.concatenate([vv[:, :hd], ones, vv[:, hd:], ones], axis=1)
    ps_ref[...] = proj(C_SH, C_NQ).astype(BF16)
    nq_ref[...] = (proj(C_NQ, C_NK) * Q_SCALE).astype(BF16)
    nk_ref[...] = proj(C_NK, C_NV).astype(BF16)
    nv_ref[...] = proj(C_NV, IN_COLS).astype(BF16)


def _in_proj(xs, mod, g1, w_in, layer, cos_t, sin_t, qg, kg, bd, seg_of_tile):
    nt, d = xs.shape
    tm = TM_PROJ
    row = lambda i: (i, 0)
    const = lambda i: (0, 0)
    widths = (2 * CONV_W, GQA_Q_W, 2 * GQA_KV_W, 2 * GQA_KV_W, 3 * SC_W, NA_W, NA_W, NA_W)
    return pl.pallas_call(
        _in_kernel,
        out_shape=[jax.ShapeDtypeStruct((nt, w), BF16) for w in widths],
        grid=(nt // tm,),
        in_specs=[
            pl.BlockSpec((tm, d), row),
            pl.BlockSpec((1, N_MOD, d), lambda i: (seg_of_tile(i), 0, 0)),
            pl.BlockSpec((1, d), const),
            _resident((None, d, IN_COLS), lambda i: (layer, 0, 0)),
            pl.BlockSpec((tm, LANE), row),
            pl.BlockSpec((tm, LANE), row),
            pl.BlockSpec((1, GQA_Q_W), const),
            pl.BlockSpec((1, GQA_KV_W), const),
            pl.BlockSpec((GQA_Q_W, GQA_Q_W), const),
        ],
        out_specs=[pl.BlockSpec((tm, w), row) for w in widths],
        compiler_params=_cparams(("parallel",)),
        name="in_proj",
    )(xs, mod, g1, w_in, cos_t, sin_t, qg, kg, bd)


def _conv_kernel(seq, n_x_rows, ctx_len,
                 pa_c, pa_p, pa_n, ps_c, ps_p, ps_n, dww_ref, dwb_ref, lng_ref, lnb_ref, scw_ref, wgu_ref, perm_ref,
                 oa_ref, oc_ref, wgu_o_ref, ext_ref, ext2_ref, acc_ref, sh_ref):
    _deint_kernel(wgu_ref, perm_ref, wgu_o_ref)
    tc = TM_CONV
    row0 = pl.program_id(0) * tc
    in_x = row0 < n_x_rows
    rel = jnp.where(in_x, row0, row0 - n_x_rows)
    seg = jnp.where(in_x, seq, ctx_len)
    keep_p = jnp.where(rel % seg == 0, 0.0, 1.0).astype(F32)
    keep_n = jnp.where((rel + tc) % seg == 0, 0.0, 1.0).astype(F32)

    def glu(ref):
        a = ref[...].astype(F32)
        return a[:, :CONV_W] * jax.nn.sigmoid(a[:, CONV_W:])

    ext_ref[0:HALO, :] = glu(pa_p) * keep_p
    ext_ref[HALO:HALO + tc, :] = glu(pa_c)
    ext_ref[HALO + tc:, :] = glu(pa_n) * keep_n

    pad = HALO - CONV_K // 2
    n_sh = sh_ref.shape[1]
    for b in range(8):
        sh_ref[b] = ext_ref[b:b + n_sh, :]
    rc = 64
    for c in range(CONV_W // LANE):
        cs = slice(c * LANE, (c + 1) * LANE)
        for r in range(tc // rc):
            acc = jnp.zeros((rc, LANE), F32)
            for k in range(CONV_K):
                a, b = divmod(k + pad, 8)
                lo = r * rc + 8 * a
                acc = acc + dww_ref[k:k + 1, cs] * sh_ref[b, lo:lo + rc, cs]
            acc_ref[r * rc:(r + 1) * rc, cs] = acc + dwb_ref[:, cs]

    u = acc_ref[...]
    mu = jnp.mean(u, axis=-1, keepdims=True)
    var = jnp.mean(jnp.square(u - mu), axis=-1, keepdims=True)
    yn = (u - mu) * lax.rsqrt(var + EPS) * lng_ref[...] + lnb_ref[...]
    oa_ref[...] = (yn * jax.nn.sigmoid(yn)).astype(BF16)

    def gated(ref, lo, hi):
        a = ref[lo:hi, :].astype(F32)
        return a[:, 2 * SC_W:] * a[:, :SC_W]

    ext2_ref[0:8, :] = gated(ps_p, HALO - 8, HALO) * keep_p
    ext2_ref[8:8 + tc, :] = gated(ps_c, 0, tc)
    ext2_ref[8 + tc:, :] = gated(ps_n, 0, 8) * keep_n
    conv = (scw_ref[0:1, :] * ext2_ref[7:7 + tc, :]
            + scw_ref[1:2, :] * ext2_ref[8:8 + tc, :]
            + scw_ref[2:3, :] * ext2_ref[9:9 + tc, :])
    oc_ref[...] = (ps_c[:, SC_W:2 * SC_W].astype(F32) * conv).astype(BF16)


def _conv_mix(pa, ps, dww, dwb, lng, lnb, scw, w_gate_up, layer, seq, n_x_rows, ctx_len):
    nt = pa.shape[0]
    tc = TM_CONV
    hb = tc // HALO
    n_hb = nt // HALO
    n_steps = nt // tc
    row = lambda i: (i, 0)
    prev = lambda i: (jnp.maximum(i * hb - 1, 0), 0)
    nxt = lambda i: (jnp.minimum((i + 1) * hb, n_hb - 1), 0)
    const = lambda i: (0, 0)
    wa, ws = pa.shape[1], ps.shape[1]
    n_l, n_e, d, n_gu = w_gate_up.shape
    rows_l = n_e * d
    w_tiles = min(rows_l // PREP_ROWS, 1 << (n_steps.bit_length() - 1))
    w_rows = rows_l // w_tiles
    assert rows_l % w_tiles == 0 and w_rows % 8 == 0
    wtile = lambda i: (layer * w_tiles + jnp.minimum(i, w_tiles - 1), 0)
    wtile_o = lambda i: (jnp.minimum(i, w_tiles - 1), 0)
    oa, oc, wgu_b = pl.pallas_call(
        functools.partial(_conv_kernel, seq, n_x_rows, ctx_len),
        out_shape=[jax.ShapeDtypeStruct((nt, CONV_W), BF16), jax.ShapeDtypeStruct((nt, SC_W), BF16),
                   jax.ShapeDtypeStruct((rows_l, n_gu), BF16)],
        grid=(n_steps,),
        in_specs=[
            pl.BlockSpec((tc, wa), row), pl.BlockSpec((HALO, wa), prev), pl.BlockSpec((HALO, wa), nxt),
            pl.BlockSpec((tc, ws), row), pl.BlockSpec((HALO, ws), prev), pl.BlockSpec((HALO, ws), nxt),
            pl.BlockSpec((CONV_K, CONV_W), const), pl.BlockSpec((1, CONV_W), const),
            pl.BlockSpec((1, CONV_W), const), pl.BlockSpec((1, CONV_W), const),
            pl.BlockSpec((SC_K, SC_W), const),
            pl.BlockSpec((w_rows, n_gu), wtile), pl.BlockSpec((PREP_COLS, PREP_COLS), const),
        ],
        out_specs=[pl.BlockSpec((tc, CONV_W), row), pl.BlockSpec((tc, SC_W), row),
                   pl.BlockSpec((w_rows, n_gu), wtile_o)],
        scratch_shapes=[pltpu.VMEM((tc + 2 * HALO, CONV_W), F32),
                        pltpu.VMEM((tc + 16, SC_W), F32),
                        pltpu.VMEM((tc, CONV_W), F32),
                        pltpu.VMEM((8, tc + 2 * HALO - 8, CONV_W), F32)],
        compiler_params=_cparams(("arbitrary",)),
        name="conv_mix",
    )(pa, pa, pa, ps, ps, ps, dww, dwb, lng, lnb, scw, w_gate_up.reshape(n_l * rows_l, n_gu), _deinterleave_perm())
    return oa, oc, wgu_b.reshape(1, n_e, d, n_gu)


def _qk(q, k):
    return lax.dot_general(q, k, (((1,), (1,)), ((), ())), preferred_element_type=F32)


def _softmax_pv(s, v):
    m = jnp.max(s, axis=-1, keepdims=True)
    p = jnp.exp2(s - m)
    l = jnp.sum(p, axis=-1, keepdims=True)
    return jnp.dot(p.astype(BF16), v, preferred_element_type=F32) / l


def _half_masks(rows):
    lane = lax.broadcasted_iota(I32, (rows, LANE), 1)
    lo = lane < HEAD_DIM
    return lo, jnp.logical_not(lo)


def _ctx_attn_kernel(kv_rep, kv_step, q_ref, k_ref, v_ref, o_in_ref, o_ref):
    del o_in_ref
    n_h = q_ref.shape[1] // HEAD_DIM
    outs = []
    for h in range(n_h):
        off = (h // kv_rep) * kv_step
        q = q_ref[:, h * HEAD_DIM:(h + 1) * HEAD_DIM]
        k = k_ref[:, off:off + HEAD_DIM]
        v = v_ref[:, off:off + HEAD_DIM]
        outs.append(_softmax_pv(_qk(q, k), v))
    o_ref[...] = jnp.concatenate(outs, axis=1).astype(BF16)


def _ctx_attn(q, k, v, o_all, n_b, ctx_len, blk0, kv_rep, kv_step):
    kw = k.shape[1]
    qw = q.shape[1]
    rows = lambda b: (blk0 + b, 0)
    return pl.pallas_call(
        functools.partial(_ctx_attn_kernel, kv_rep, kv_step),
        out_shape=jax.ShapeDtypeStruct(o_all.shape, o_all.dtype),
        grid=(n_b,),
        in_specs=[pl.BlockSpec((ctx_len, qw), rows), pl.BlockSpec((ctx_len, kw), rows),
                  pl.BlockSpec((ctx_len, kw), rows), pl.BlockSpec(memory_space=pl.ANY)],
        out_specs=pl.BlockSpec((ctx_len, qw), rows),
        input_output_aliases={3: 0},
        compiler_params=_cparams(("parallel",)),
        name="ctx_attn",
    )(q, k, v, o_all)


def _gqa_kernel(n_chunks, q_ref, kx_ref, vx_ref, kc_ref, vc_ref, o_ref):
    tq = q_ref.shape[0]
    grp = GQA_HEADS // GQA_KV_HEADS
    lo, hi = _half_masks(tq)

    def q_group(g):
        parts = []
        for j in range(grp):
            h = g * grp + j
            blk = q_ref[:, (h // 2) * LANE:(h // 2 + 1) * LANE].astype(F32)
            parts.append(jnp.where(lo if h % 2 == 0 else hi, blk, 0.0).astype(BF16))
        return jnp.concatenate(parts, axis=0)

    def update(q, k, v, m, acc):
        s = _qk(q, k)
        m_new = jnp.maximum(m, jnp.max(s, axis=-1, keepdims=True))
        p = jnp.exp2(s - m_new)
        acc = jnp.exp2(m - m_new) * acc + jnp.dot(p.astype(BF16), v, preferred_element_type=F32)
        return m_new, acc

    qs = [q_group(g) for g in range(GQA_KV_HEADS)]
    gsl = [slice(g * LANE, (g + 1) * LANE) for g in range(GQA_KV_HEADS)]
    state = []
    for g in range(GQA_KV_HEADS):
        s = _qk(qs[g], kc_ref[:, gsl[g]])
        m = jnp.max(s, axis=-1, keepdims=True)
        p = jnp.exp2(s - m)
        state += [m, jnp.dot(p.astype(BF16), vc_ref[:, gsl[g]], preferred_element_type=F32)]

    def body(c, carry):
        rows = pl.ds(pl.multiple_of(c * KC_GQA, KC_GQA), KC_GQA)
        out = []
        for g in range(GQA_KV_HEADS):
            out += update(qs[g], kx_ref[rows, gsl[g]], vx_ref[rows, gsl[g]], carry[2 * g], carry[2 * g + 1])
        return tuple(out)

    state = lax.fori_loop(0, n_chunks, body, tuple(state), unroll=8)

    lo4, _ = _half_masks(grp * tq)
    heads = []
    for g in range(GQA_KV_HEADS):
        acc = state[2 * g + 1]
        inv = 1.0 / jnp.where(lo4, 1.0, acc)
        o = acc * pltpu.roll(inv, HEAD_DIM, 1)
        heads.extend(o[j * tq:(j + 1) * tq] for j in range(grp))
    pairs = [jnp.where(lo, heads[2 * i], pltpu.roll(heads[2 * i + 1], HEAD_DIM, 1))
             for i in range(GQA_HEADS // 2)]
    o_ref[...] = jnp.concatenate(pairs, axis=1).astype(BF16)


def _gqa(q, k, v, n_b, seq, ctx_len):
    nt = q.shape[0]
    tq = TQ_GQA
    nq = seq // tq
    cblk0 = n_b * seq // ctx_len
    kw = k.shape[1]
    return pl.pallas_call(
        functools.partial(_gqa_kernel, seq // KC_GQA),
        out_shape=jax.ShapeDtypeStruct((nt, GQA_Q_W), BF16),
        grid=(n_b, nq),
        in_specs=[
            pl.BlockSpec((tq, GQA_Q_W), lambda b, j: (b * nq + j, 0)),
            pl.BlockSpec((seq, kw), lambda b, j: (b, 0)),
            pl.BlockSpec((seq, kw), lambda b, j: (b, 0)),
            pl.BlockSpec((ctx_len, kw), lambda b, j: (cblk0 + b, 0)),
            pl.BlockSpec((ctx_len, kw), lambda b, j: (cblk0 + b, 0)),
        ],
        out_specs=pl.BlockSpec((tq, GQA_Q_W), lambda b, j: (b * nq + j, 0)),
        compiler_params=_cparams(("parallel", "parallel")),
        name="gqa_flash",
    )(q, k, v, k, v)


def _na_kernel(n_rows, q_ref, k_ref, v_ref, kc_ref, vc_ref, t_ref, o_ref):
    rb = pl.program_id(2)
    win = WIN_ROWS * GRID_W
    tq = q_ref.shape[0]
    lo, hi = _half_masks(tq)
    qf = q_ref[...].astype(F32)
    kc = kc_ref[...]
    vc = vc_ref[...]
    halves = []
    for j in range(LANE // HEAD_DIM):
        qa = jnp.where(lo if j == 0 else hi, qf, 0.0).astype(BF16)
        s_ctx = _qk(qa, kc)
        m_ctx = jnp.max(s_ctx, axis=-1, keepdims=True)
        s_wins, ms, krows_l = [], [], []
        for i in range(NA_RB):
            r = rb * NA_RB + i
            r0 = jnp.clip(r - WIN_ROWS // 2, 0, n_rows - WIN_ROWS)
            shift = r0 - r + (WIN_ROWS - 1)
            krows = pl.ds(pl.multiple_of(r0 * GRID_W, GRID_W), win)
            rs = slice(i * GRID_W, (i + 1) * GRID_W)
            bias = jnp.concatenate([t_ref[j, 2 * wp + shift] for wp in range(WIN_ROWS // 2)], axis=1)
            s_win = _qk(qa[rs], k_ref[krows, :]) + bias
            s_wins.append(s_win)
            ms.append(jnp.maximum(jnp.max(s_win, axis=-1, keepdims=True), m_ctx[rs]))
            krows_l.append(krows)
        m_all = jnp.concatenate(ms, axis=0)
        p_ctx = jnp.exp2(s_ctx - m_all)
        l_ctx = jnp.sum(p_ctx, axis=-1, keepdims=True)
        o_ctx = jnp.dot(p_ctx.astype(BF16), vc, preferred_element_type=F32)
        outs = []
        for i in range(NA_RB):
            rs = slice(i * GRID_W, (i + 1) * GRID_W)
            p = jnp.exp2(s_wins[i] - ms[i])
            l = jnp.sum(p, axis=-1, keepdims=True) + l_ctx[rs]
            o = jnp.dot(p.astype(BF16), v_ref[krows_l[i], :], preferred_element_type=F32) + o_ctx[rs]
            outs.append(o / l)
        halves.append(jnp.concatenate(outs, axis=0))
    o_ref[...] = jnp.where(lo, halves[0], halves[1]).astype(BF16)


def _na(nq, nk, nv, table, n_b, seq, ctx_len):
    nt = nq.shape[0]
    n_rows = seq // GRID_W
    n_rb = n_rows // NA_RB
    tq = NA_RB * GRID_W
    cblk0 = n_b * seq // ctx_len
    hp = LANE // HEAD_DIM
    return pl.pallas_call(
        functools.partial(_na_kernel, n_rows),
        out_shape=jax.ShapeDtypeStruct((nt, NA_W), BF16),
        grid=(n_b, NA_W // LANE, n_rb),
        in_specs=[
            pl.BlockSpec((tq, LANE), lambda b, h, r: (b * n_rb + r, h)),
            pl.BlockSpec((seq, LANE), lambda b, h, r: (b, h)),
            pl.BlockSpec((seq, LANE), lambda b, h, r: (b, h)),
            pl.BlockSpec((ctx_len, LANE), lambda b, h, r: (cblk0 + b, h)),
            pl.BlockSpec((ctx_len, LANE), lambda b, h, r: (cblk0 + b, h)),
            pl.BlockSpec((hp, 2 * WIN_ROWS - 2, GRID_W, LANE), lambda b, h, r: (h, 0, 0, 0)),
        ],
        out_specs=pl.BlockSpec((tq, LANE), lambda b, h, r: (b * n_rb + r, h)),
        compiler_params=_cparams(("parallel", "parallel", "parallel")),
        name="na_attn",
    )(nq, nk, nv, nk, nv, table)


def _na_bias_table(rpb):
    cols = np.arange(GRID_W)
    c0 = np.clip(cols - WIN_COLS // 2, 0, GRID_W - WIN_COLS)
    kc = np.arange(GRID_W)
    rel = kc[None, :] - cols[:, None] + (WIN_COLS - 1)
    valid = (kc[None, :] >= c0[:, None]) & (kc[None, :] < c0[:, None] + WIN_COLS)
    rel = np.clip(rel, 0, 2 * WIN_COLS - 2)
    t = rpb[:, :, rel]
    t = jnp.where(jnp.asarray(valid)[None, None], t * LOG2E, MASK_VALUE).astype(F32)
    return jnp.concatenate([t[:, :-1], t[:, 1:]], axis=-1)


def _out_kernel(oa_ref, ob_ref, oc_ref, od_ref, x_ref, mod_ref, g_ref, w_ref, wr_cat_ref, wr_hi_ref, br_ref,
                x1_ref, h2_ref, e_ref, pos_ref, gate_ref, cnt_ref, carry_ref):
    acc = jnp.dot(oa_ref[...], w_ref[0:CONV_W, :], preferred_element_type=F32)
    acc = acc + jnp.dot(ob_ref[...], w_ref[CONV_W:CONV_W + GQA_Q_W, :], preferred_element_type=F32)
    acc = acc + jnp.dot(oc_ref[...], w_ref[CONV_W + GQA_Q_W:MIX_W - NA_W, :], preferred_element_type=F32)
    acc = acc + jnp.dot(od_ref[...], w_ref[MIX_W - NA_W:MIX_W, :], preferred_element_type=F32)
    x1 = x_ref[...] + mod_ref[0, 2:3, :] * acc
    x1_ref[...] = x1
    ms = jnp.mean(x1 * x1, axis=-1, keepdims=True)
    y = x1 * lax.rsqrt(ms + EPS) * g_ref[...]
    h2 = y * (1 + mod_ref[0, 4:5, :]) + mod_ref[0, 3:4, :]
    h2_ref[...] = h2
    hi, lo = _split_bf16(h2)
    n_e = br_ref.shape[1]
    a = jnp.dot(hi, wr_cat_ref[...], preferred_element_type=F32)
    b = jnp.dot(lo, wr_hi_ref[...], preferred_element_type=F32)
    logits = a[:, :n_e] + a[:, n_e:] + b + br_ref[...]
    _route_tile(logits, e_ref, pos_ref, gate_ref, cnt_ref, carry_ref)


def _out_proj(oa, ob, oc, od, xs, mod, g2, w_out, layer, wr_cat, wr_hi, br, seg_of_tile):
    nt, d = xs.shape
    tm = TM_PROJ
    n_e = wr_hi.shape[1]
    row = lambda i: (i, 0)
    const = lambda i: (0, 0)
    return pl.pallas_call(
        _out_kernel,
        out_shape=[jax.ShapeDtypeStruct((nt, d), F32), jax.ShapeDtypeStruct((nt, d), F32),
                   jax.ShapeDtypeStruct((nt, TOP_K), I32), jax.ShapeDtypeStruct((nt, TOP_K), I32),
                   jax.ShapeDtypeStruct((nt, TOP_K), F32), jax.ShapeDtypeStruct((1, n_e), I32)],
        grid=(nt // tm,),
        in_specs=[
            pl.BlockSpec((tm, CONV_W), row), pl.BlockSpec((tm, GQA_Q_W), row),
            pl.BlockSpec((tm, SC_W), row), pl.BlockSpec((tm, NA_W), row),
            pl.BlockSpec((tm, d), row),
            pl.BlockSpec((1, N_MOD, d), lambda i: (seg_of_tile(i), 0, 0)),
            pl.BlockSpec((1, d), const),
            _resident((None, MIX_W, d), lambda i: (layer, 0, 0)),
            pl.BlockSpec((d, 2 * n_e), const), pl.BlockSpec((d, n_e), const), pl.BlockSpec((1, n_e), const),
        ],
        out_specs=[pl.BlockSpec((tm, d), row), pl.BlockSpec((tm, d), row),
                   pl.BlockSpec((tm, TOP_K), row), pl.BlockSpec((tm, TOP_K), row), pl.BlockSpec((tm, TOP_K), row),
                   pl.BlockSpec((1, n_e), const)],
        scratch_shapes=[pltpu.VMEM((1, n_e), F32)],
        compiler_params=_cparams(("arbitrary",)),
        name="out_proj",
    )(oa, ob, oc, od, xs, mod, g2, w_out, wr_cat, wr_hi, br)


def _route_tile(lg, e_ref, pos_ref, gate_ref, cnt_ref, carry_ref):
    i = pl.program_id(0)

    @pl.when(i == 0)
    def _():
        carry_ref[...] = jnp.zeros_like(carry_ref)

    tr, n_e = lg.shape
    lane = lax.broadcasted_iota(I32, (tr, n_e), 1)
    work = lg
    vals, idxs = [], []
    for _ in range(TOP_K):
        m = jnp.max(work, axis=-1, keepdims=True)
        idx = jnp.min(jnp.where(work == m, lane, n_e), axis=-1, keepdims=True)
        vals.append(m)
        idxs.append(idx)
        work = jnp.where(lane == idx, -jnp.inf, work)
    exps = [jnp.exp(v - vals[0]) for v in vals]
    den = exps[0] + exps[1] + exps[2] + exps[3]

    mask = jnp.zeros((tr, n_e), F32)
    for idx in idxs:
        mask = mask + jnp.where(lane == idx, 1.0, 0.0)
    r_i = lax.broadcasted_iota(I32, (tr, tr), 0)
    c_i = lax.broadcasted_iota(I32, (tr, tr), 1)
    tri = jnp.where(c_i <= r_i, 1.0, 0.0).astype(BF16)
    incl = jnp.dot(tri, mask.astype(BF16), preferred_element_type=F32)
    before = carry_ref[...] + incl - mask

    lane_o = lax.broadcasted_iota(I32, (tr, LANE), 1)
    e_out = jnp.zeros((tr, LANE), I32)
    pos_out = jnp.zeros((tr, LANE), I32)
    gate_out = jnp.zeros((tr, LANE), F32)
    for k in range(TOP_K):
        pos_k = jnp.sum(jnp.where(lane == idxs[k], before, 0.0), axis=-1, keepdims=True)
        e_out = jnp.where(lane_o == k, idxs[k], e_out)
        pos_out = jnp.where(lane_o == k, pos_k.astype(I32), pos_out)
        gate_out = jnp.where(lane_o == k, exps[k] / den, gate_out)
    e_ref[...] = e_out[:, :TOP_K]
    pos_ref[...] = pos_out[:, :TOP_K]
    gate_ref[...] = gate_out[:, :TOP_K]
    total = carry_ref[...] + incl[tr - 1:tr, :]
    carry_ref[...] = total
    cnt_ref[...] = total.astype(I32)


def _row_copy(src_ref, src_row, dst_ref, dst_row, sem):
    return pltpu.make_async_copy(src_ref.at[pl.ds(src_row, 1)], dst_ref.at[pl.ds(dst_row, 1)], sem)


def _dispatch_kernel(dest_ref, pad_end_ref, h_ref, xs_ref, zbuf_ref, sem, zsem):
    td = h_ref.shape[0]
    blk = zbuf_ref.shape[0]
    n_e = pad_end_ref.shape[0]
    base = pl.program_id(0) * td * TOP_K

    @pl.when(pl.program_id(0) == 0)
    def _():
        zbuf_ref[...] = jnp.zeros_like(zbuf_ref)

        def fill(e):
            start = pl.multiple_of(pad_end_ref[e] - blk, blk)
            return pltpu.make_async_copy(zbuf_ref, xs_ref.at[pl.ds(start, blk)], zsem)

        def nonempty(e):
            prev = pad_end_ref[jnp.maximum(e - 1, 0)]
            return pad_end_ref[e] > jnp.where(e == 0, 0, prev)

        def start(e, _):
            @pl.when(nonempty(e))
            def _():
                fill(e).start()
            return 0

        def wait(e, _):
            @pl.when(nonempty(e))
            def _():
                fill(e).wait()
            return 0

        lax.fori_loop(0, n_e, start, 0)
        lax.fori_loop(0, n_e, wait, 0)

    def issue(t, _):
        for k in range(TOP_K):
            _row_copy(h_ref, t, xs_ref, dest_ref[base + t * TOP_K + k], sem).start()
        return 0

    lax.fori_loop(0, td, issue, 0, unroll=DMA_UNROLL)
    n_rows = td * TOP_K
    pltpu.make_async_copy(xs_ref.at[pl.ds(0, n_rows)], xs_ref.at[pl.ds(0, n_rows)], sem).wait()


def _dispatch(dest, pad_end, h2, n_slots):
    nt, d = h2.shape
    td = TD_MOE
    return pl.pallas_call(
        _dispatch_kernel,
        out_shape=jax.ShapeDtypeStruct((n_slots, d), F32),
        grid_spec=pltpu.PrefetchScalarGridSpec(
            num_scalar_prefetch=2,
            grid=(nt // td,),
            in_specs=[pl.BlockSpec((td, d), lambda i, dest, pe: (i, 0))],
            out_specs=pl.BlockSpec(memory_space=pl.ANY),
            scratch_shapes=[pltpu.VMEM((BLK_MOE, d), F32), pltpu.SemaphoreType.DMA, pltpu.SemaphoreType.DMA],
        ),
        compiler_params=_cparams(("arbitrary",)),
        name="moe_dispatch",
    )(dest, pad_end, h2)


def _ffn_kernel(blk_e_ref, n_act_ref, x_ref, wgu_ref, bgu_ref, wd_ref, bd_ref, y_ref, wd_bf_ref):
    b = pl.program_id(0)

    @pl.when(b < n_act_ref[0])
    def _():
        @pl.when(jnp.logical_or(b == 0, blk_e_ref[b] != blk_e_ref[jnp.maximum(b - 1, 0)]))
        def _():
            wd_bf_ref[...] = wd_ref[0].astype(BF16)

        d_e = wd_ref.shape[1]
        xb = x_ref[...].astype(BF16)
        h = jnp.dot(xb, wgu_ref[0], preferred_element_type=F32) + bgu_ref[0]
        glu = jnp.minimum(h[:, :d_e], SWIGLU_LIMIT)
        lin = jnp.clip(h[:, d_e:], -SWIGLU_LIMIT, SWIGLU_LIMIT)
        act = glu * jax.nn.sigmoid(SWIGLU_ALPHA * glu) * (lin + 1)
        y_ref[...] = jnp.dot(act.astype(BF16), wd_bf_ref[...], preferred_element_type=F32) + bd_ref[0]


def _ffn(blk_e, n_act, xs, wgu, bgu, wd, bd, layer):
    ns, d = xs.shape
    blk = BLK_MOE
    d_e2 = wgu.shape[-1]
    d_e = d_e2 // 2

    def rows(b, be, na):
        return (jnp.minimum(b, na[0] - 1), 0)

    def expert4(b, be, na):
        return (layer, be[jnp.minimum(b, na[0] - 1)], 0, 0)

    def expert4_gu(b, be, na):
        return (0, be[jnp.minimum(b, na[0] - 1)], 0, 0)

    return pl.pallas_call(
        _ffn_kernel,
        out_shape=jax.ShapeDtypeStruct((ns, d), F32),
        grid_spec=pltpu.PrefetchScalarGridSpec(
            num_scalar_prefetch=2,
            grid=(ns // blk,),
            in_specs=[
                pl.BlockSpec((blk, d), rows),
                pl.BlockSpec((None, 1, d, d_e2), expert4_gu), pl.BlockSpec((None, 1, 1, d_e2), expert4),
                pl.BlockSpec((None, 1, d_e, d), expert4), pl.BlockSpec((None, 1, 1, d), expert4),
            ],
            out_specs=pl.BlockSpec((blk, d), rows),
            scratch_shapes=[pltpu.VMEM((d_e, d), BF16)],
        ),
        compiler_params=_cparams(("arbitrary",)),
        name="moe_ffn",
    )(blk_e, n_act, xs, wgu, bgu, wd, bd)


def _combine_kernel(final, dest_ref, ys_ref, gate_ref, x1_ref, mod_ref, fg_ref, o_ref, buf_ref, sems):
    td = x1_ref.shape[0]
    i = pl.program_id(0)
    slot = i % 2

    def gather(tile, dst_slot):
        base = tile * td * TOP_K

        def issue(t, _):
            for k in range(TOP_K):
                _row_copy(ys_ref, dest_ref[base + t * TOP_K + k], buf_ref.at[dst_slot, k], t,
                          sems.at[dst_slot]).start()
            return 0

        lax.fori_loop(0, td, issue, 0, unroll=DMA_UNROLL)

    @pl.when(i == 0)
    def _():
        gather(0, 0)

    @pl.when(i + 1 < pl.num_programs(0))
    def _():
        gather(i + 1, 1 - slot)

    pltpu.make_async_copy(buf_ref.at[slot], buf_ref.at[slot], sems.at[slot]).wait()

    g = gate_ref[...]
    y = g[:, 0:1] * buf_ref[slot, 0]
    for k in range(1, TOP_K):
        y = y + g[:, k:k + 1] * buf_ref[slot, k]
    x = x1_ref[...] + mod_ref[0, 5:6, :] * y
    if final:
        ms = jnp.mean(x * x, axis=-1, keepdims=True)
        x = x * lax.rsqrt(ms + EPS) * fg_ref[...]
    o_ref[...] = x


def _combine(dest, ys, gate, x1, mod, seg_of_tile, final_g, n_rows):
    d = x1.shape[1]
    td = TD_MOE
    ratio = TM_PROJ // td
    final = final_g is not None
    fg = final_g if final else jnp.ones((1, d), F32)
    return pl.pallas_call(
        functools.partial(_combine_kernel, final),
        out_shape=jax.ShapeDtypeStruct((n_rows, d), F32),
        grid_spec=pltpu.PrefetchScalarGridSpec(
            num_scalar_prefetch=1,
            grid=(n_rows // td,),
            in_specs=[
                pl.BlockSpec(memory_space=pl.ANY),
                pl.BlockSpec((td, TOP_K), lambda i, dest: (i, 0)),
                pl.BlockSpec((td, d), lambda i, dest: (i, 0)),
                pl.BlockSpec((1, N_MOD, d), lambda i, dest: (seg_of_tile(i // ratio), 0, 0)),
                pl.BlockSpec((1, d), lambda i, dest: (0, 0)),
            ],
            out_specs=pl.BlockSpec((td, d), lambda i, dest: (i, 0)),
            scratch_shapes=[pltpu.VMEM((2, TOP_K, td, d), F32), pltpu.SemaphoreType.DMA((2,))],
        ),
        compiler_params=_cparams(("arbitrary",)),
        name="moe_combine",
    )(dest, ys, gate, x1, mod, fg)


def _rope_tables(n_b, seq, ctx_len):
    t = np.arange(seq)
    axis_dim = HEAD_DIM // 2
    inv = (ROPE_THETA ** (-np.arange(0, axis_dim, 2, dtype=np.float32) / axis_dim)).astype(np.float32)
    pos = np.stack([t // GRID_W, t % GRID_W], axis=-1).astype(np.float32)
    ang = jnp.asarray(pos[:, :, None] * inv[None, None, :])
    cos, sin = jnp.cos(ang), jnp.sin(ang)
    cos_h = jnp.concatenate([cos, cos], axis=-1).reshape(seq, HEAD_DIM)
    sin_h = jnp.concatenate([-sin, sin], axis=-1).reshape(seq, HEAD_DIM)
    cos_x = jnp.tile(cos_h, (n_b, LANE // HEAD_DIM))
    sin_x = jnp.tile(sin_h, (n_b, LANE // HEAD_DIM))
    ones = jnp.ones((n_b * ctx_len, LANE), F32)
    return (jnp.concatenate([cos_x, ones], axis=0),
            jnp.concatenate([sin_x, jnp.zeros_like(ones)], axis=0))


def kernel(x, c, ctx, c_ctx, w_ada, b_ada, norm1_g, norm2_g, w_in, w_out, conv_dw_w, conv_dw_b, conv_ln_g,
           conv_ln_b, q_norm_g, k_norm_g, sc_conv_w, na_rpb, w_router, b_router, w_gate_up, b_gate_up, w_down,
           b_down, final_norm_g):
    n_b, seq, d = x.shape
    ctx_len = ctx.shape[1]
    depth = w_ada.shape[0]
    n_e = w_router.shape[2]
    d_e = w_down.shape[2]
    n_x_rows = n_b * seq
    nt = n_x_rows + n_b * ctx_len
    assert d == D_MODEL and seq % GRID_W == 0 and (seq // GRID_W) % NA_RB == 0
    assert seq % TM_PROJ == 0 and (n_b * ctx_len) % TM_PROJ == 0
    assert seq % TM_CONV == 0 and ctx_len % TM_CONV == 0 and seq % ctx_len == 0
    assert seq % KC_GQA == 0 and nt % TD_MOE == 0 and n_b + 1 <= 8

    tiles_x = n_x_rows // TM_PROJ
    tiles_per_b = seq // TM_PROJ

    def seg_of_tile(i):
        return jnp.where(i < tiles_x, 1 + i // tiles_per_b, 0)

    w_in_b = w_in.astype(BF16)
    w_out_b = w_out.astype(BF16)
    bgu_p = jnp.concatenate([b_gate_up[..., 0::2], b_gate_up[..., 1::2]], axis=-1)[:, :, None, :]
    bd_p = b_down[:, :, None, :]
    wr_hi = w_router.astype(BF16)
    wr_lo = (w_router - wr_hi.astype(F32)).astype(BF16)
    wr_cat = jnp.concatenate([wr_hi, wr_lo], axis=-1)
    bd = jnp.asarray(np.kron(np.eye(GQA_HEADS, dtype=np.float32),
                             np.full((HEAD_DIM, HEAD_DIM), 1.0 / HEAD_DIM, np.float32))).astype(BF16)
    cos_t, sin_t = _rope_tables(n_b, seq, ctx_len)

    cvec = jnp.zeros((8, d), F32).at[0].set(c_ctx).at[1:1 + n_b].set(c)
    mods = _ada_all(cvec, w_ada, b_ada).reshape(depth, 8, N_MOD, d)

    n_assign = nt * TOP_K
    n_blk = -(-n_assign // BLK_MOE) + n_e

    xs = jnp.concatenate([x.reshape(n_x_rows, d), ctx.reshape(n_b * ctx_len, d)], axis=0)
    cblk0 = n_x_rows // ctx_len

    pending = None
    for l in range(depth):
        mod = mods[l]
        in_args = (mod, norm1_g[l][None], w_in_b, l, cos_t, sin_t,
                   jnp.tile(q_norm_g[l], GQA_HEADS)[None], jnp.tile(k_norm_g[l], GQA_KV_HEADS)[None], bd,
                   seg_of_tile)
        if pending is not None:
            xs = _combine(*pending, seg_of_tile, None, nt)
        pa, q, k, v, ps, nq, nk, nv = _in_proj(xs, *in_args)
        oa, oc, wgu_b = _conv_mix(pa, ps, conv_dw_w[l], conv_dw_b[l][None], conv_ln_g[l][None],
                                  conv_ln_b[l][None], sc_conv_w[l], w_gate_up, l, seq, n_x_rows, ctx_len)
        ob = _gqa(q, k, v, n_b, seq, ctx_len)
        ob = _ctx_attn(q, k, v, ob, n_b, ctx_len, cblk0, GQA_HEADS // GQA_KV_HEADS, LANE)
        od = _na(nq, nk, nv, _na_bias_table(na_rpb[l]), n_b, seq, ctx_len)
        od = _ctx_attn(nq, nk, nv, od, n_b, ctx_len, cblk0, 1, HEAD_DIM)
        x1, h2, e_o, pos_o, gate_o, cnt = _out_proj(oa, ob, oc, od, xs, mod, norm2_g[l][None], w_out_b, l,
                                                    wr_cat[l], wr_hi[l], b_router[l][None], seg_of_tile)
        counts = cnt[0]
        padded = (counts + BLK_MOE - 1) // BLK_MOE * BLK_MOE
        pad_end = jnp.cumsum(padded)
        pad_start = pad_end - padded
        dest = (pad_start[e_o] + pos_o).reshape(-1).astype(I32)
        blk_lo = jnp.arange(n_blk, dtype=I32) * BLK_MOE
        blk_e = jnp.minimum(jnp.sum((pad_end[None, :] <= blk_lo[:, None]).astype(I32), axis=1), n_e - 1)
        n_act = (pad_end[-1:] // BLK_MOE).astype(I32)

        x_sorted = _dispatch(dest, pad_end.astype(I32), h2, n_blk * BLK_MOE)
        ys = _ffn(blk_e, n_act, x_sorted, wgu_b, bgu_p, w_down, bd_p, l)
        pending = (dest, ys, gate_o, x1, mod)

    out = _combine(*pending, seg_of_tile, final_norm_g[None], n_x_rows)
    return out.reshape(n_b, seq, d)
```

```python
import functools

import jax
import jax.numpy as jnp
import numpy as np
from jax import lax
from jax.experimental import pallas as pl
from jax.experimental.pallas import tpu as pltpu

F32 = jnp.float32
BF16 = jnp.bfloat16
I32 = jnp.int32

D_MODEL = 2048
GRID_W = 64
HEAD_DIM = 64
CONV_W = 512
GQA_HEADS = 8
GQA_KV_HEADS = 2
SC_W = 512
NA_HEADS = 8
GQA_Q_W = GQA_HEADS * HEAD_DIM
GQA_KV_W = GQA_KV_HEADS * HEAD_DIM
NA_W = NA_HEADS * HEAD_DIM
MIX_W = CONV_W + GQA_Q_W + SC_W + NA_W
IN_COLS = 2 * CONV_W + GQA_Q_W + 2 * GQA_KV_W + 3 * SC_W + 3 * NA_W
CONV_K = 31
SC_K = 3
WIN_ROWS = 8
WIN_COLS = 16
ROPE_THETA = 10000.0
ATTN_SCALE = HEAD_DIM ** -0.5
TOP_K = 4
SWIGLU_LIMIT = 7.0
SWIGLU_ALPHA = 1.702
N_MOD = 6
EPS = 1e-6
MASK_VALUE = -1e30
LOG2E = 1.4426950408889634
Q_SCALE = ATTN_SCALE * LOG2E

C_AV, C_GQ = 0, 2 * CONV_W
C_GK = C_GQ + GQA_Q_W
C_GV = C_GK + GQA_KV_W
C_SH = C_GV + GQA_KV_W
C_NQ = C_SH + 3 * SC_W
C_NK = C_NQ + NA_W
C_NV = C_NK + NA_W

VMEM_LIMIT = 56 * 1024 * 1024
LANE = 128

TM_PROJ = 512
TM_CONV = 256
HALO = 16
TQ_GQA = 256
KC_GQA = 512
NA_RB = 8
TD_MOE = 256
BLK_MOE = 256
DMA_UNROLL = 4


def _cparams(sem):
    return pltpu.CompilerParams(dimension_semantics=sem, vmem_limit_bytes=VMEM_LIMIT)


def _resident(shape, index_map):
    return pl.BlockSpec(shape, index_map, pipeline_mode=pl.Buffered(1))


def _split_bf16(a):
    hi = a.astype(BF16)
    lo = (a - hi.astype(F32)).astype(BF16)
    return hi, lo


PREP_COLS = 256
PREP_ROWS = 1024


def _deint_kernel(w_ref, p_ref, o_ref):
    n = w_ref.shape[1]
    half = PREP_COLS // 2
    for b in range(n // PREP_COLS):
        t = jnp.dot(w_ref[:, b * PREP_COLS:(b + 1) * PREP_COLS].astype(BF16), p_ref[...],
                    preferred_element_type=F32).astype(BF16)
        o_ref[:, b * half:(b + 1) * half] = t[:, :half]
        o_ref[:, n // 2 + b * half:n // 2 + (b + 1) * half] = t[:, half:]


def _deinterleave_perm():
    perm = np.zeros((PREP_COLS, PREP_COLS), np.float32)
    j = np.arange(PREP_COLS // 2)
    perm[2 * j, j] = 1.0
    perm[2 * j + 1, PREP_COLS // 2 + j] = 1.0
    return jnp.asarray(perm, BF16)


def _ada_kernel(c_ref, w_ref, b_ref, o_ref):
    c = c_ref[...]
    s = c * jax.nn.sigmoid(c)
    hi, lo = _split_bf16(s)
    lhs = jnp.concatenate([hi, lo], axis=0)
    r = jnp.dot(lhs, w_ref[0].astype(BF16), preferred_element_type=F32)
    o_ref[0] = r[:8] + r[8:] + b_ref[0]


def _ada_all(cvec, w_ada, b_ada):
    n_l, d, n6 = w_ada.shape
    tn = 1024
    return pl.pallas_call(
        _ada_kernel,
        out_shape=jax.ShapeDtypeStruct((n_l, 8, n6), F32),
        grid=(n_l, n6 // tn),
        in_specs=[
            pl.BlockSpec((8, d), lambda l, j: (0, 0)),
            pl.BlockSpec((1, d, tn), lambda l, j: (l, 0, j)),
            pl.BlockSpec((1, 1, tn), lambda l, j: (l, 0, j)),
        ],
        out_specs=pl.BlockSpec((1, 8, tn), lambda l, j: (l, 0, j)),
        compiler_params=_cparams(("parallel", "parallel")),
        name="ada_mod",
    )(cvec, w_ada, b_ada.reshape(n_l, 1, n6))


def _rope(x, cos, sin_signed):
    n = x.shape[1]
    lane = lax.broadcasted_iota(I32, x.shape, 1)
    first = (lane % 32) < 16
    partner = jnp.where(first, pltpu.roll(x, n - 16, 1), pltpu.roll(x, 16, 1))
    return x * cos + partner * sin_signed


def _head_mean_sq(y, bd):
    hi, lo = _split_bf16(y * y)
    return (jnp.dot(hi, bd, preferred_element_type=F32)
            + jnp.dot(lo, bd, preferred_element_type=F32))


def _in_kernel(x_ref, *refs):
    _in_body(x_ref[...], *refs)


def _in_body(x, mod_ref, g_ref, w_ref, cos_ref, sin_ref, qg_ref, kg_ref, bd_ref,
             pa_ref, q_ref, k_ref, v_ref, ps_ref, nq_ref, nk_ref, nv_ref):
    ms = jnp.mean(x * x, axis=-1, keepdims=True)
    y = x * lax.rsqrt(ms + EPS) * g_ref[...]
    shift = mod_ref[0, 0:1, :]
    scale = mod_ref[0, 1:2, :]
    h = (y * (1 + scale) + shift).astype(BF16)

    def proj(lo, hi):
        return jnp.dot(h, w_ref[:, lo:hi], preferred_element_type=F32)

    pa_ref[...] = proj(C_AV, C_GQ).astype(BF16)

    cos = cos_ref[...]
    sin = sin_ref[...]
    gq = proj(C_GQ, C_GK)
    qn = gq * lax.rsqrt(_head_mean_sq(gq, bd_ref[...]) + EPS) * qg_ref[...]
    cos4 = jnp.concatenate([cos] * 4, axis=1)
    sin4 = jnp.concatenate([sin] * 4, axis=1)
    q_ref[...] = (_rope(qn, cos4, sin4) * Q_SCALE).astype(BF16)

    gk = proj(C_GK, C_GV)
    kn = gk * lax.rsqrt(_head_mean_sq(gk, bd_ref[0:GQA_KV_W, 0:GQA_KV_W]) + EPS) * kg_ref[...]
    kr = _rope(kn, cos, sin).astype(BF16)
    hd = HEAD_DIM
    k_ref[...] = jnp.concatenate([kr[:, :hd], kr[:, :hd], kr[:, hd:], kr[:, hd:]], axis=1)
    vv = proj(C_GV, C_SH).astype(BF16)
    ones = jnp.ones((vv.shape[0], hd), BF16)
    v_ref[...] = jnp.concatenate([vv[:, :hd], ones, vv[:, hd:], ones], axis=1)
    ps_ref[...] = proj(C_SH, C_NQ).astype(BF16)
    nq_ref[...] = (proj(C_NQ, C_NK) * Q_SCALE).astype(BF16)
    nk_ref[...] = proj(C_NK, C_NV).astype(BF16)
    nv_ref[...] = proj(C_NV, IN_COLS).astype(BF16)


def _in_proj(xs, mod, g1, w_in, layer, cos_t, sin_t, qg, kg, bd, seg_of_tile):
    nt, d = xs.shape
    tm = TM_PROJ
    row = lambda i: (i, 0)
    const = lambda i: (0, 0)
    widths = (2 * CONV_W, GQA_Q_W, 2 * GQA_KV_W, 2 * GQA_KV_W, 3 * SC_W, NA_W, NA_W, NA_W)
    return pl.pallas_call(
        _in_kernel,
        out_shape=[jax.ShapeDtypeStruct((nt, w), BF16) for w in widths],
        grid=(nt // tm,),
        in_specs=[
            pl.BlockSpec((tm, d), row),
            pl.BlockSpec((1, N_MOD, d), lambda i: (seg_of_tile(i), 0, 0)),
            pl.BlockSpec((1, d), const),
            _resident((None, d, IN_COLS), lambda i: (layer, 0, 0)),
            pl.BlockSpec((tm, LANE), row),
            pl.BlockSpec((tm, LANE), row),
            pl.BlockSpec((1, GQA_Q_W), const),
            pl.BlockSpec((1, GQA_KV_W), const),
            pl.BlockSpec((GQA_Q_W, GQA_Q_W), const),
        ],
        out_specs=[pl.BlockSpec((tm, w), row) for w in widths],
        compiler_params=_cparams(("parallel",)),
        name="in_proj",
    )(xs, mod, g1, w_in, cos_t, sin_t, qg, kg, bd)


def _conv_kernel(seq, n_x_rows, ctx_len,
                 pa_c, pa_p, pa_n, ps_c, ps_p, ps_n, dww_ref, dwb_ref, lng_ref, lnb_ref, scw_ref, wgu_ref, perm_ref,
                 oa_ref, oc_ref, wgu_o_ref, ext_ref, ext2_ref, acc_ref, sh_ref):
    _deint_kernel(wgu_ref, perm_ref, wgu_o_ref)
    tc = TM_CONV
    row0 = pl.program_id(0) * tc
    in_x = row0 < n_x_rows
    rel = jnp.where(in_x, row0, row0 - n_x_rows)
    seg = jnp.where(in_x, seq, ctx_len)
    keep_p = jnp.where(rel % seg == 0, 0.0, 1.0).astype(F32)
    keep_n = jnp.where((rel + tc) % seg == 0, 0.0, 1.0).astype(F32)

    def glu(ref):
        a = ref[...].astype(F32)
        return a[:, :CONV_W] * jax.nn.sigmoid(a[:, CONV_W:])

    ext_ref[0:HALO, :] = glu(pa_p) * keep_p
    ext_ref[HALO:HALO + tc, :] = glu(pa_c)
    ext_ref[HALO + tc:, :] = glu(pa_n) * keep_n

    pad = HALO - CONV_K // 2
    n_sh = sh_ref.shape[1]
    for b in range(8):
        sh_ref[b] = ext_ref[b:b + n_sh, :]
    rc = 64
    for c in range(CONV_W // LANE):
        cs = slice(c * LANE, (c + 1) * LANE)
        for r in range(tc // rc):
            acc = jnp.zeros((rc, LANE), F32)
            for k in range(CONV_K):
                a, b = divmod(k + pad, 8)
                lo = r * rc + 8 * a
                acc = acc + dww_ref[k:k + 1, cs] * sh_ref[b, lo:lo + rc, cs]
            acc_ref[r * rc:(r + 1) * rc, cs] = acc + dwb_ref[:, cs]

    u = acc_ref[...]
    mu = jnp.mean(u, axis=-1, keepdims=True)
    var = jnp.mean(jnp.square(u - mu), axis=-1, keepdims=True)
    yn = (u - mu) * lax.rsqrt(var + EPS) * lng_ref[...] + lnb_ref[...]
    oa_ref[...] = (yn * jax.nn.sigmoid(yn)).astype(BF16)

    def gated(ref, lo, hi):
        a = ref[lo:hi, :].astype(F32)
        return a[:, 2 * SC_W:] * a[:, :SC_W]

    ext2_ref[0:8, :] = gated(ps_p, HALO - 8, HALO) * keep_p
    ext2_ref[8:8 + tc, :] = gated(ps_c, 0, tc)
    ext2_ref[8 + tc:, :] = gated(ps_n, 0, 8) * keep_n
    conv = (scw_ref[0:1, :] * ext2_ref[7:7 + tc, :]
            + scw_ref[1:2, :] * ext2_ref[8:8 + tc, :]
            + scw_ref[2:3, :] * ext2_ref[9:9 + tc, :])
    oc_ref[...] = (ps_c[:, SC_W:2 * SC_W].astype(F32) * conv).astype(BF16)


def _conv_mix(pa, ps, dww, dwb, lng, lnb, scw, w_gate_up, layer, seq, n_x_rows, ctx_len):
    nt = pa.shape[0]
    tc = TM_CONV
    hb = tc // HALO
    n_hb = nt // HALO
    n_steps = nt // tc
    row = lambda i: (i, 0)
    prev = lambda i: (jnp.maximum(i * hb - 1, 0), 0)
    nxt = lambda i: (jnp.minimum((i + 1) * hb, n_hb - 1), 0)
    const = lambda i: (0, 0)
    wa, ws = pa.shape[1], ps.shape[1]
    n_l, n_e, d, n_gu = w_gate_up.shape
    rows_l = n_e * d
    w_tiles = min(rows_l // PREP_ROWS, 1 << (n_steps.bit_length() - 1))
    w_rows = rows_l // w_tiles
    assert rows_l % w_tiles == 0 and w_rows % 8 == 0
    wtile = lambda i: (layer * w_tiles + jnp.minimum(i, w_tiles - 1), 0)
    wtile_o = lambda i: (jnp.minimum(i, w_tiles - 1), 0)
    oa, oc, wgu_b = pl.pallas_call(
        functools.partial(_conv_kernel, seq, n_x_rows, ctx_len),
        out_shape=[jax.ShapeDtypeStruct((nt, CONV_W), BF16), jax.ShapeDtypeStruct((nt, SC_W), BF16),
                   jax.ShapeDtypeStruct((rows_l, n_gu), BF16)],
        grid=(n_steps,),
        in_specs=[
            pl.BlockSpec((tc, wa), row), pl.BlockSpec((HALO, wa), prev), pl.BlockSpec((HALO, wa), nxt),
            pl.BlockSpec((tc, ws), row), pl.BlockSpec((HALO, ws), prev), pl.BlockSpec((HALO, ws), nxt),
            pl.BlockSpec((CONV_K, CONV_W), const), pl.BlockSpec((1, CONV_W), const),
            pl.BlockSpec((1, CONV_W), const), pl.BlockSpec((1, CONV_W), const),
            pl.BlockSpec((SC_K, SC_W), const),
            pl.BlockSpec((w_rows, n_gu), wtile), pl.BlockSpec((PREP_COLS, PREP_COLS), const),
        ],
        out_specs=[pl.BlockSpec((tc, CONV_W), row), pl.BlockSpec((tc, SC_W), row),
                   pl.BlockSpec((w_rows, n_gu), wtile_o)],
        scratch_shapes=[pltpu.VMEM((tc + 2 * HALO, CONV_W), F32),
                        pltpu.VMEM((tc + 16, SC_W), F32),
                        pltpu.VMEM((tc, CONV_W), F32),
                        pltpu.VMEM((8, tc + 2 * HALO - 8, CONV_W), F32)],
        compiler_params=_cparams(("arbitrary",)),
        name="conv_mix",
    )(pa, pa, pa, ps, ps, ps, dww, dwb, lng, lnb, scw, w_gate_up.reshape(n_l * rows_l, n_gu), _deinterleave_perm())
    return oa, oc, wgu_b.reshape(1, n_e, d, n_gu)


def _qk(q, k):
    return lax.dot_general(q, k, (((1,), (1,)), ((), ())), preferred_element_type=F32)


def _softmax_pv(s, v):
    m = jnp.max(s, axis=-1, keepdims=True)
    p = jnp.exp2(s - m)
    l = jnp.sum(p, axis=-1, keepdims=True)
    return jnp.dot(p.astype(BF16), v, preferred_element_type=F32) / l


def _half_masks(rows):
    lane = lax.broadcasted_iota(I32, (rows, LANE), 1)
    lo = lane < HEAD_DIM
    return lo, jnp.logical_not(lo)


def _ctx_attn_kernel(kv_rep, kv_step, q_ref, k_ref, v_ref, o_in_ref, o_ref):
    del o_in_ref
    n_h = q_ref.shape[1] // HEAD_DIM
    outs = []
    for h in range(n_h):
        off = (h // kv_rep) * kv_step
        q = q_ref[:, h * HEAD_DIM:(h + 1) * HEAD_DIM]
        k = k_ref[:, off:off + HEAD_DIM]
        v = v_ref[:, off:off + HEAD_DIM]
        outs.append(_softmax_pv(_qk(q, k), v))
    o_ref[...] = jnp.concatenate(outs, axis=1).astype(BF16)


def _ctx_attn(q, k, v, o_all, n_b, ctx_len, blk0, kv_rep, kv_step):
    kw = k.shape[1]
    qw = q.shape[1]
    rows = lambda b: (blk0 + b, 0)
    return pl.pallas_call(
        functools.partial(_ctx_attn_kernel, kv_rep, kv_step),
        out_shape=jax.ShapeDtypeStruct(o_all.shape, o_all.dtype),
        grid=(n_b,),
        in_specs=[pl.BlockSpec((ctx_len, qw), rows), pl.BlockSpec((ctx_len, kw), rows),
                  pl.BlockSpec((ctx_len, kw), rows), pl.BlockSpec(memory_space=pl.ANY)],
        out_specs=pl.BlockSpec((ctx_len, qw), rows),
        input_output_aliases={3: 0},
        compiler_params=_cparams(("parallel",)),
        name="ctx_attn",
    )(q, k, v, o_all)


def _gqa_kernel(n_chunks, q_ref, kx_ref, vx_ref, kc_ref, vc_ref, o_ref):
    tq = q_ref.shape[0]
    grp = GQA_HEADS // GQA_KV_HEADS
    lo, hi = _half_masks(tq)

    def q_group(g):
        parts = []
        for j in range(grp):
            h = g * grp + j
            blk = q_ref[:, (h // 2) * LANE:(h // 2 + 1) * LANE].astype(F32)
            parts.append(jnp.where(lo if h % 2 == 0 else hi, blk, 0.0).astype(BF16))
        return jnp.concatenate(parts, axis=0)

    def update(q, k, v, m, acc):
        s = _qk(q, k)
        m_new = jnp.maximum(m, jnp.max(s, axis=-1, keepdims=True))
        p = jnp.exp2(s - m_new)
        acc = jnp.exp2(m - m_new) * acc + jnp.dot(p.astype(BF16), v, preferred_element_type=F32)
        return m_new, acc

    qs = [q_group(g) for g in range(GQA_KV_HEADS)]
    gsl = [slice(g * LANE, (g + 1) * LANE) for g in range(GQA_KV_HEADS)]
    state = []
    for g in range(GQA_KV_HEADS):
        s = _qk(qs[g], kc_ref[:, gsl[g]])
        m = jnp.max(s, axis=-1, keepdims=True)
        p = jnp.exp2(s - m)
        state += [m, jnp.dot(p.astype(BF16), vc_ref[:, gsl[g]], preferred_element_type=F32)]

    def body(c, carry):
        rows = pl.ds(pl.multiple_of(c * KC_GQA, KC_GQA), KC_GQA)
        out = []
        for g in range(GQA_KV_HEADS):
            out += update(qs[g], kx_ref[rows, gsl[g]], vx_ref[rows, gsl[g]], carry[2 * g], carry[2 * g + 1])
        return tuple(out)

    state = lax.fori_loop(0, n_chunks, body, tuple(state), unroll=16)

    lo4, _ = _half_masks(grp * tq)
    heads = []
    for g in range(GQA_KV_HEADS):
        acc = state[2 * g + 1]
        inv = 1.0 / jnp.where(lo4, 1.0, acc)
        o = acc * pltpu.roll(inv, HEAD_DIM, 1)
        heads.extend(o[j * tq:(j + 1) * tq] for j in range(grp))
    pairs = [jnp.where(lo, heads[2 * i], pltpu.roll(heads[2 * i + 1], HEAD_DIM, 1))
             for i in range(GQA_HEADS // 2)]
    o_ref[...] = jnp.concatenate(pairs, axis=1).astype(BF16)


def _gqa(q, k, v, n_b, seq, ctx_len):
    nt = q.shape[0]
    tq = TQ_GQA
    nq = seq // tq
    cblk0 = n_b * seq // ctx_len
    kw = k.shape[1]
    return pl.pallas_call(
        functools.partial(_gqa_kernel, seq // KC_GQA),
        out_shape=jax.ShapeDtypeStruct((nt, GQA_Q_W), BF16),
        grid=(n_b, nq),
        in_specs=[
            pl.BlockSpec((tq, GQA_Q_W), lambda b, j: (b * nq + j, 0)),
            pl.BlockSpec((seq, kw), lambda b, j: (b, 0)),
            pl.BlockSpec((seq, kw), lambda b, j: (b, 0)),
            pl.BlockSpec((ctx_len, kw), lambda b, j: (cblk0 + b, 0)),
            pl.BlockSpec((ctx_len, kw), lambda b, j: (cblk0 + b, 0)),
        ],
        out_specs=pl.BlockSpec((tq, GQA_Q_W), lambda b, j: (b * nq + j, 0)),
        compiler_params=_cparams(("parallel", "parallel")),
        name="gqa_flash",
    )(q, k, v, k, v)


def _na_kernel(n_rows, q_ref, k_ref, v_ref, kc_ref, vc_ref, t_ref, o_ref):
    rb = pl.program_id(2)
    win = WIN_ROWS * GRID_W
    tq = q_ref.shape[0]
    lo, hi = _half_masks(tq)
    qf = q_ref[...].astype(F32)
    kc = kc_ref[...]
    vc = vc_ref[...]
    halves = []
    for j in range(LANE // HEAD_DIM):
        qa = jnp.where(lo if j == 0 else hi, qf, 0.0).astype(BF16)
        s_ctx = _qk(qa, kc)
        m_ctx = jnp.max(s_ctx, axis=-1, keepdims=True)
        s_wins, ms, krows_l = [], [], []
        for i in range(NA_RB):
            r = rb * NA_RB + i
            r0 = jnp.clip(r - WIN_ROWS // 2, 0, n_rows - WIN_ROWS)
            shift = r0 - r + (WIN_ROWS - 1)
            krows = pl.ds(pl.multiple_of(r0 * GRID_W, GRID_W), win)
            rs = slice(i * GRID_W, (i + 1) * GRID_W)
            bias = jnp.concatenate([t_ref[j, 2 * wp + shift] for wp in range(WIN_ROWS // 2)], axis=1)
            s_win = _qk(qa[rs], k_ref[krows, :]) + bias
            s_wins.append(s_win)
            ms.append(jnp.maximum(jnp.max(s_win, axis=-1, keepdims=True), m_ctx[rs]))
            krows_l.append(krows)
        m_all = jnp.concatenate(ms, axis=0)
        p_ctx = jnp.exp2(s_ctx - m_all)
        l_ctx = jnp.sum(p_ctx, axis=-1, keepdims=True)
        o_ctx = jnp.dot(p_ctx.astype(BF16), vc, preferred_element_type=F32)
        outs = []
        for i in range(NA_RB):
            rs = slice(i * GRID_W, (i + 1) * GRID_W)
            p = jnp.exp2(s_wins[i] - ms[i])
            l = jnp.sum(p, axis=-1, keepdims=True) + l_ctx[rs]
            o = jnp.dot(p.astype(BF16), v_ref[krows_l[i], :], preferred_element_type=F32) + o_ctx[rs]
            outs.append(o / l)
        halves.append(jnp.concatenate(outs, axis=0))
    o_ref[...] = jnp.where(lo, halves[0], halves[1]).astype(BF16)


def _na(nq, nk, nv, table, n_b, seq, ctx_len):
    nt = nq.shape[0]
    n_rows = seq // GRID_W
    n_rb = n_rows // NA_RB
    tq = NA_RB * GRID_W
    cblk0 = n_b * seq // ctx_len
    hp = LANE // HEAD_DIM
    return pl.pallas_call(
        functools.partial(_na_kernel, n_rows),
        out_shape=jax.ShapeDtypeStruct((nt, NA_W), BF16),
        grid=(n_b, NA_W // LANE, n_rb),
        in_specs=[
            pl.BlockSpec((tq, LANE), lambda b, h, r: (b * n_rb + r, h)),
            pl.BlockSpec((seq, LANE), lambda b, h, r: (b, h)),
            pl.BlockSpec((seq, LANE), lambda b, h, r: (b, h)),
            pl.BlockSpec((ctx_len, LANE), lambda b, h, r: (cblk0 + b, h)),
            pl.BlockSpec((ctx_len, LANE), lambda b, h, r: (cblk0 + b, h)),
            pl.BlockSpec((hp, 2 * WIN_ROWS - 2, GRID_W, LANE), lambda b, h, r: (h, 0, 0, 0)),
        ],
        out_specs=pl.BlockSpec((tq, LANE), lambda b, h, r: (b * n_rb + r, h)),
        compiler_params=_cparams(("parallel", "parallel", "parallel")),
        name="na_attn",
    )(nq, nk, nv, nk, nv, table)


def _na_bias_table(rpb):
    cols = np.arange(GRID_W)
    c0 = np.clip(cols - WIN_COLS // 2, 0, GRID_W - WIN_COLS)
    kc = np.arange(GRID_W)
    rel = kc[None, :] - cols[:, None] + (WIN_COLS - 1)
    valid = (kc[None, :] >= c0[:, None]) & (kc[None, :] < c0[:, None] + WIN_COLS)
    rel = np.clip(rel, 0, 2 * WIN_COLS - 2)
    t = rpb[:, :, rel]
    t = jnp.where(jnp.asarray(valid)[None, None], t * LOG2E, MASK_VALUE).astype(F32)
    return jnp.concatenate([t[:, :-1], t[:, 1:]], axis=-1)


def _out_kernel(oa_ref, ob_ref, oc_ref, od_ref, x_ref, mod_ref, g_ref, w_ref, wr_cat_ref, wr_hi_ref, br_ref,
                x1_ref, h2_ref, e_ref, pos_ref, gate_ref, cnt_ref, carry_ref):
    acc = jnp.dot(oa_ref[...], w_ref[0:CONV_W, :], preferred_element_type=F32)
    acc = acc + jnp.dot(ob_ref[...], w_ref[CONV_W:CONV_W + GQA_Q_W, :], preferred_element_type=F32)
    acc = acc + jnp.dot(oc_ref[...], w_ref[CONV_W + GQA_Q_W:MIX_W - NA_W, :], preferred_element_type=F32)
    acc = acc + jnp.dot(od_ref[...], w_ref[MIX_W - NA_W:MIX_W, :], preferred_element_type=F32)
    x1 = x_ref[...] + mod_ref[0, 2:3, :] * acc
    x1_ref[...] = x1
    ms = jnp.mean(x1 * x1, axis=-1, keepdims=True)
    y = x1 * lax.rsqrt(ms + EPS) * g_ref[...]
    h2 = y * (1 + mod_ref[0, 4:5, :]) + mod_ref[0, 3:4, :]
    h2_ref[...] = h2
    hi, lo = _split_bf16(h2)
    n_e = br_ref.shape[1]
    a = jnp.dot(hi, wr_cat_ref[...], preferred_element_type=F32)
    b = jnp.dot(lo, wr_hi_ref[...], preferred_element_type=F32)
    logits = a[:, :n_e] + a[:, n_e:] + b + br_ref[...]
    _route_tile(logits, e_ref, pos_ref, gate_ref, cnt_ref, carry_ref)


def _out_proj(oa, ob, oc, od, xs, mod, g2, w_out, layer, wr_cat, wr_hi, br, seg_of_tile):
    nt, d = xs.shape
    tm = TM_PROJ
    n_e = wr_hi.shape[1]
    row = lambda i: (i, 0)
    const = lambda i: (0, 0)
    return pl.pallas_call(
        _out_kernel,
        out_shape=[jax.ShapeDtypeStruct((nt, d), F32), jax.ShapeDtypeStruct((nt, d), F32),
                   jax.ShapeDtypeStruct((nt, TOP_K), I32), jax.ShapeDtypeStruct((nt, TOP_K), I32),
                   jax.ShapeDtypeStruct((nt, TOP_K), F32), jax.ShapeDtypeStruct((1, n_e), I32)],
        grid=(nt // tm,),
        in_specs=[
            pl.BlockSpec((tm, CONV_W), row), pl.BlockSpec((tm, GQA_Q_W), row),
            pl.BlockSpec((tm, SC_W), row), pl.BlockSpec((tm, NA_W), row),
            pl.BlockSpec((tm, d), row),
            pl.BlockSpec((1, N_MOD, d), lambda i: (seg_of_tile(i), 0, 0)),
            pl.BlockSpec((1, d), const),
            _resident((None, MIX_W, d), lambda i: (layer, 0, 0)),
            pl.BlockSpec((d, 2 * n_e), const), pl.BlockSpec((d, n_e), const), pl.BlockSpec((1, n_e), const),
        ],
        out_specs=[pl.BlockSpec((tm, d), row), pl.BlockSpec((tm, d), row),
                   pl.BlockSpec((tm, TOP_K), row), pl.BlockSpec((tm, TOP_K), row), pl.BlockSpec((tm, TOP_K), row),
                   pl.BlockSpec((1, n_e), const)],
        scratch_shapes=[pltpu.VMEM((1, n_e), F32)],
        compiler_params=_cparams(("arbitrary",)),
        name="out_proj",
    )(oa, ob, oc, od, xs, mod, g2, w_out, wr_cat, wr_hi, br)


def _route_tile(lg, e_ref, pos_ref, gate_ref, cnt_ref, carry_ref):
    i = pl.program_id(0)

    @pl.when(i == 0)
    def _():
        carry_ref[...] = jnp.zeros_like(carry_ref)

    tr, n_e = lg.shape
    lane = lax.broadcasted_iota(I32, (tr, n_e), 1)
    work = lg
    vals, idxs = [], []
    for _ in range(TOP_K):
        m = jnp.max(work, axis=-1, keepdims=True)
        idx = jnp.min(jnp.where(work == m, lane, n_e), axis=-1, keepdims=True)
        vals.append(m)
        idxs.append(idx)
        work = jnp.where(lane == idx, -jnp.inf, work)
    exps = [jnp.exp(v - vals[0]) for v in vals]
    den = exps[0] + exps[1] + exps[2] + exps[3]

    mask = jnp.zeros((tr, n_e), F32)
    for idx in idxs:
        mask = mask + jnp.where(lane == idx, 1.0, 0.0)
    r_i = lax.broadcasted_iota(I32, (tr, tr), 0)
    c_i = lax.broadcasted_iota(I32, (tr, tr), 1)
    tri = jnp.where(c_i <= r_i, 1.0, 0.0).astype(BF16)
    incl = jnp.dot(tri, mask.astype(BF16), preferred_element_type=F32)
    before = carry_ref[...] + incl - mask

    lane_o = lax.broadcasted_iota(I32, (tr, LANE), 1)
    e_out = jnp.zeros((tr, LANE), I32)
    pos_out = jnp.zeros((tr, LANE), I32)
    gate_out = jnp.zeros((tr, LANE), F32)
    for k in range(TOP_K):
        pos_k = jnp.sum(jnp.where(lane == idxs[k], before, 0.0), axis=-1, keepdims=True)
        e_out = jnp.where(lane_o == k, idxs[k], e_out)
        pos_out = jnp.where(lane_o == k, pos_k.astype(I32), pos_out)
        gate_out = jnp.where(lane_o == k, exps[k] / den, gate_out)
    e_ref[...] = e_out[:, :TOP_K]
    pos_ref[...] = pos_out[:, :TOP_K]
    gate_ref[...] = gate_out[:, :TOP_K]
    total = carry_ref[...] + incl[tr - 1:tr, :]
    carry_ref[...] = total
    cnt_ref[...] = total.astype(I32)


def _row_copy(src_ref, src_row, dst_ref, dst_row, sem):
    return pltpu.make_async_copy(src_ref.at[pl.ds(src_row, 1)], dst_ref.at[pl.ds(dst_row, 1)], sem)


def _dispatch_kernel(dest_ref, pad_end_ref, h_ref, xs_ref, zbuf_ref, sem, zsem):
    td = h_ref.shape[0]
    blk = zbuf_ref.shape[0]
    n_e = pad_end_ref.shape[0]
    base = pl.program_id(0) * td * TOP_K

    @pl.when(pl.program_id(0) == 0)
    def _():
        zbuf_ref[...] = jnp.zeros_like(zbuf_ref)

        def fill(e):
            start = pl.multiple_of(pad_end_ref[e] - blk, blk)
            return pltpu.make_async_copy(zbuf_ref, xs_ref.at[pl.ds(start, blk)], zsem)

        def nonempty(e):
            prev = pad_end_ref[jnp.maximum(e - 1, 0)]
            return pad_end_ref[e] > jnp.where(e == 0, 0, prev)

        def start(e, _):
            @pl.when(nonempty(e))
            def _():
                fill(e).start()
            return 0

        def wait(e, _):
            @pl.when(nonempty(e))
            def _():
                fill(e).wait()
            return 0

        lax.fori_loop(0, n_e, start, 0)
        lax.fori_loop(0, n_e, wait, 0)

    def issue(t, _):
        for k in range(TOP_K):
            _row_copy(h_ref, t, xs_ref, dest_ref[base + t * TOP_K + k], sem).start()
        return 0

    lax.fori_loop(0, td, issue, 0, unroll=DMA_UNROLL)
    n_rows = td * TOP_K
    pltpu.make_async_copy(xs_ref.at[pl.ds(0, n_rows)], xs_ref.at[pl.ds(0, n_rows)], sem).wait()


def _dispatch(dest, pad_end, h2, n_slots):
    nt, d = h2.shape
    td = TD_MOE
    return pl.pallas_call(
        _dispatch_kernel,
        out_shape=jax.ShapeDtypeStruct((n_slots, d), F32),
        grid_spec=pltpu.PrefetchScalarGridSpec(
            num_scalar_prefetch=2,
            grid=(nt // td,),
            in_specs=[pl.BlockSpec((td, d), lambda i, dest, pe: (i, 0))],
            out_specs=pl.BlockSpec(memory_space=pl.ANY),
            scratch_shapes=[pltpu.VMEM((BLK_MOE, d), F32), pltpu.SemaphoreType.DMA, pltpu.SemaphoreType.DMA],
        ),
        compiler_params=_cparams(("arbitrary",)),
        name="moe_dispatch",
    )(dest, pad_end, h2)


def _ffn_kernel(blk_e_ref, n_act_ref, x_ref, wgu_ref, bgu_ref, wd_ref, bd_ref, y_ref, wd_bf_ref):
    b = pl.program_id(0)

    @pl.when(b < n_act_ref[0])
    def _():
        @pl.when(jnp.logical_or(b == 0, blk_e_ref[b] != blk_e_ref[jnp.maximum(b - 1, 0)]))
        def _():
            wd_bf_ref[...] = wd_ref[0].astype(BF16)

        d_e = wd_ref.shape[1]
        xb = x_ref[...].astype(BF16)
        h = jnp.dot(xb, wgu_ref[0], preferred_element_type=F32) + bgu_ref[0]
        glu = jnp.minimum(h[:, :d_e], SWIGLU_LIMIT)
        lin = jnp.clip(h[:, d_e:], -SWIGLU_LIMIT, SWIGLU_LIMIT)
        act = glu * jax.nn.sigmoid(SWIGLU_ALPHA * glu) * (lin + 1)
        y_ref[...] = jnp.dot(act.astype(BF16), wd_bf_ref[...], preferred_element_type=F32) + bd_ref[0]


def _ffn(blk_e, n_act, xs, wgu, bgu, wd, bd, layer):
    ns, d = xs.shape
    blk = BLK_MOE
    d_e2 = wgu.shape[-1]
    d_e = d_e2 // 2

    def rows(b, be, na):
        return (jnp.minimum(b, na[0] - 1), 0)

    def expert4(b, be, na):
        return (layer, be[jnp.minimum(b, na[0] - 1)], 0, 0)

    def expert4_gu(b, be, na):
        return (0, be[jnp.minimum(b, na[0] - 1)], 0, 0)

    return pl.pallas_call(
        _ffn_kernel,
        out_shape=jax.ShapeDtypeStruct((ns, d), F32),
        grid_spec=pltpu.PrefetchScalarGridSpec(
            num_scalar_prefetch=2,
            grid=(ns // blk,),
            in_specs=[
                pl.BlockSpec((blk, d), rows),
                pl.BlockSpec((None, 1, d, d_e2), expert4_gu), pl.BlockSpec((None, 1, 1, d_e2), expert4),
                pl.BlockSpec((None, 1, d_e, d), expert4), pl.BlockSpec((None, 1, 1, d), expert4),
            ],
            out_specs=pl.BlockSpec((blk, d), rows),
            scratch_shapes=[pltpu.VMEM((d_e, d), BF16)],
        ),
        compiler_params=_cparams(("arbitrary",)),
        name="moe_ffn",
    )(blk_e, n_act, xs, wgu, bgu, wd, bd)


def _combine_kernel(final, dest_ref, ys_ref, gate_ref, x1_ref, mod_ref, fg_ref, o_ref, buf_ref, sems):
    td = x1_ref.shape[0]
    i = pl.program_id(0)
    slot = i % 2

    def gather(tile, dst_slot):
        base = tile * td * TOP_K

        def issue(t, _):
            for k in range(TOP_K):
                _row_copy(ys_ref, dest_ref[base + t * TOP_K + k], buf_ref.at[dst_slot, k], t,
                          sems.at[dst_slot]).start()
            return 0

        lax.fori_loop(0, td, issue, 0, unroll=DMA_UNROLL)

    @pl.when(i == 0)
    def _():
        gather(0, 0)

    @pl.when(i + 1 < pl.num_programs(0))
    def _():
        gather(i + 1, 1 - slot)

    pltpu.make_async_copy(buf_ref.at[slot], buf_ref.at[slot], sems.at[slot]).wait()

    g = gate_ref[...]
    y = g[:, 0:1] * buf_ref[slot, 0]
    for k in range(1, TOP_K):
        y = y + g[:, k:k + 1] * buf_ref[slot, k]
    x = x1_ref[...] + mod_ref[0, 5:6, :] * y
    if final:
        ms = jnp.mean(x * x, axis=-1, keepdims=True)
        x = x * lax.rsqrt(ms + EPS) * fg_ref[...]
    o_ref[...] = x


def _combine(dest, ys, gate, x1, mod, seg_of_tile, final_g, n_rows):
    d = x1.shape[1]
    td = TD_MOE
    ratio = TM_PROJ // td
    final = final_g is not None
    fg = final_g if final else jnp.ones((1, d), F32)
    return pl.pallas_call(
        functools.partial(_combine_kernel, final),
        out_shape=jax.ShapeDtypeStruct((n_rows, d), F32),
        grid_spec=pltpu.PrefetchScalarGridSpec(
            num_scalar_prefetch=1,
            grid=(n_rows // td,),
            in_specs=[
                pl.BlockSpec(memory_space=pl.ANY),
                pl.BlockSpec((td, TOP_K), lambda i, dest: (i, 0)),
                pl.BlockSpec((td, d), lambda i, dest: (i, 0)),
                pl.BlockSpec((1, N_MOD, d), lambda i, dest: (seg_of_tile(i // ratio), 0, 0)),
                pl.BlockSpec((1, d), lambda i, dest: (0, 0)),
            ],
            out_specs=pl.BlockSpec((td, d), lambda i, dest: (i, 0)),
            scratch_shapes=[pltpu.VMEM((2, TOP_K, td, d), F32), pltpu.SemaphoreType.DMA((2,))],
        ),
        compiler_params=_cparams(("arbitrary",)),
        name="moe_combine",
    )(dest, ys, gate, x1, mod, fg)


def _rope_tables(n_b, seq, ctx_len):
    t = np.arange(seq)
    axis_dim = HEAD_DIM // 2
    inv = (ROPE_THETA ** (-np.arange(0, axis_dim, 2, dtype=np.float32) / axis_dim)).astype(np.float32)
    pos = np.stack([t // GRID_W, t % GRID_W], axis=-1).astype(np.float32)
    ang = jnp.asarray(pos[:, :, None] * inv[None, None, :])
    cos, sin = jnp.cos(ang), jnp.sin(ang)
    cos_h = jnp.concatenate([cos, cos], axis=-1).reshape(seq, HEAD_DIM)
    sin_h = jnp.concatenate([-sin, sin], axis=-1).reshape(seq, HEAD_DIM)
    cos_x = jnp.tile(cos_h, (n_b, LANE // HEAD_DIM))
    sin_x = jnp.tile(sin_h, (n_b, LANE // HEAD_DIM))
    ones = jnp.ones((n_b * ctx_len, LANE), F32)
    return (jnp.concatenate([cos_x, ones], axis=0),
            jnp.concatenate([sin_x, jnp.zeros_like(ones)], axis=0))


def kernel(x, c, ctx, c_ctx, w_ada, b_ada, norm1_g, norm2_g, w_in, w_out, conv_dw_w, conv_dw_b, conv_ln_g,
           conv_ln_b, q_norm_g, k_norm_g, sc_conv_w, na_rpb, w_router, b_router, w_gate_up, b_gate_up, w_down,
           b_down, final_norm_g):
    n_b, seq, d = x.shape
    ctx_len = ctx.shape[1]
    depth = w_ada.shape[0]
    n_e = w_router.shape[2]
    d_e = w_down.shape[2]
    n_x_rows = n_b * seq
    nt = n_x_rows + n_b * ctx_len
    assert d == D_MODEL and seq % GRID_W == 0 and (seq // GRID_W) % NA_RB == 0
    assert seq % TM_PROJ == 0 and (n_b * ctx_len) % TM_PROJ == 0
    assert seq % TM_CONV == 0 and ctx_len % TM_CONV == 0 and seq % ctx_len == 0
    assert seq % KC_GQA == 0 and nt % TD_MOE == 0 and n_b + 1 <= 8

    tiles_x = n_x_rows // TM_PROJ
    tiles_per_b = seq // TM_PROJ

    def seg_of_tile(i):
        return jnp.where(i < tiles_x, 1 + i // tiles_per_b, 0)

    w_in_b = w_in.astype(BF16)
    w_out_b = w_out.astype(BF16)
    bgu_p = jnp.concatenate([b_gate_up[..., 0::2], b_gate_up[..., 1::2]], axis=-1)[:, :, None, :]
    bd_p = b_down[:, :, None, :]
    wr_hi = w_router.astype(BF16)
    wr_lo = (w_router - wr_hi.astype(F32)).astype(BF16)
    wr_cat = jnp.concatenate([wr_hi, wr_lo], axis=-1)
    bd = jnp.asarray(np.kron(np.eye(GQA_HEADS, dtype=np.float32),
                             np.full((HEAD_DIM, HEAD_DIM), 1.0 / HEAD_DIM, np.float32))).astype(BF16)
    cos_t, sin_t = _rope_tables(n_b, seq, ctx_len)

    cvec = jnp.zeros((8, d), F32).at[0].set(c_ctx).at[1:1 + n_b].set(c)
    mods = _ada_all(cvec, w_ada, b_ada).reshape(depth, 8, N_MOD, d)

    n_assign = nt * TOP_K
    n_blk = -(-n_assign // BLK_MOE) + n_e

    xs = jnp.concatenate([x.reshape(n_x_rows, d), ctx.reshape(n_b * ctx_len, d)], axis=0)
    cblk0 = n_x_rows // ctx_len

    pending = None
    for l in range(depth):
        mod = mods[l]
        in_args = (mod, norm1_g[l][None], w_in_b, l, cos_t, sin_t,
                   jnp.tile(q_norm_g[l], GQA_HEADS)[None], jnp.tile(k_norm_g[l], GQA_KV_HEADS)[None], bd,
                   seg_of_tile)
        if pending is not None:
            xs = _combine(*pending, seg_of_tile, None, nt)
        pa, q, k, v, ps, nq, nk, nv = _in_proj(xs, *in_args)
        oa, oc, wgu_b = _conv_mix(pa, ps, conv_dw_w[l], conv_dw_b[l][None], conv_ln_g[l][None],
                                  conv_ln_b[l][None], sc_conv_w[l], w_gate_up, l, seq, n_x_rows, ctx_len)
        ob = _gqa(q, k, v, n_b, seq, ctx_len)
        ob = _ctx_attn(q, k, v, ob, n_b, ctx_len, cblk0, GQA_HEADS // GQA_KV_HEADS, LANE)
        od = _na(nq, nk, nv, _na_bias_table(na_rpb[l]), n_b, seq, ctx_len)
        od = _ctx_attn(nq, nk, nv, od, n_b, ctx_len, cblk0, 1, HEAD_DIM)
        x1, h2, e_o, pos_o, gate_o, cnt = _out_proj(oa, ob, oc, od, xs, mod, norm2_g[l][None], w_out_b, l,
                                                    wr_cat[l], wr_hi[l], b_router[l][None], seg_of_tile)
        counts = cnt[0]
        padded = (counts + BLK_MOE - 1) // BLK_MOE * BLK_MOE
        pad_end = jnp.cumsum(padded)
        pad_start = pad_end - padded
        dest = (pad_start[e_o] + pos_o).reshape(-1).astype(I32)
        blk_lo = jnp.arange(n_blk, dtype=I32) * BLK_MOE
        blk_e = jnp.minimum(jnp.sum((pad_end[None, :] <= blk_lo[:, None]).astype(I32), axis=1), n_e - 1)
        n_act = (pad_end[-1:] // BLK_MOE).astype(I32)

        x_sorted = _dispatch(dest, pad_end.astype(I32), h2, n_blk * BLK_MOE)
        ys = _ffn(blk_e, n_act, x_sorted, wgu_b, bgu_p, w_down, bd_p, l)
        pending = (dest, ys, gate_o, x1, mod)

    out = _combine(*pending, seg_of_tile, final_norm_g[None], n_x_rows)
    return out.reshape(n_b, seq, d)
```

```python
import functools

import jax
import jax.numpy as jnp
import numpy as np
from jax import lax
from jax.experimental import pallas as pl
from jax.experimental.pallas import tpu as pltpu

F32 = jnp.float32
BF16 = jnp.bfloat16
I32 = jnp.int32

D_MODEL = 2048
GRID_W = 64
HEAD_DIM = 64
CONV_W = 512
GQA_HEADS = 8
GQA_KV_HEADS = 2
SC_W = 512
NA_HEADS = 8
GQA_Q_W = GQA_HEADS * HEAD_DIM
GQA_KV_W = GQA_KV_HEADS * HEAD_DIM
NA_W = NA_HEADS * HEAD_DIM
MIX_W = CONV_W + GQA_Q_W + SC_W + NA_W
IN_COLS = 2 * CONV_W + GQA_Q_W + 2 * GQA_KV_W + 3 * SC_W + 3 * NA_W
CONV_K = 31
SC_K = 3
WIN_ROWS = 8
WIN_COLS = 16
ROPE_THETA = 10000.0
ATTN_SCALE = HEAD_DIM ** -0.5
TOP_K = 4
SWIGLU_LIMIT = 7.0
SWIGLU_ALPHA = 1.702
N_MOD = 6
EPS = 1e-6
MASK_VALUE = -1e30
LOG2E = 1.4426950408889634
Q_SCALE = ATTN_SCALE * LOG2E

C_AV, C_GQ = 0, 2 * CONV_W
C_GK = C_GQ + GQA_Q_W
C_GV = C_GK + GQA_KV_W
C_SH = C_GV + GQA_KV_W
C_NQ = C_SH + 3 * SC_W
C_NK = C_NQ + NA_W
C_NV = C_NK + NA_W

VMEM_LIMIT = 56 * 1024 * 1024
LANE = 128

TM_PROJ = 512
TM_CONV = 256
HALO = 16
TQ_GQA = 256
KC_GQA = 512
NA_RB = 8
TD_MOE = 256
BLK_MOE = 256
DMA_UNROLL = 4


def _cparams(sem):
    return pltpu.CompilerParams(dimension_semantics=sem, vmem_limit_bytes=VMEM_LIMIT)


def _resident(shape, index_map):
    return pl.BlockSpec(shape, index_map, pipeline_mode=pl.Buffered(1))


def _split_bf16(a):
    hi = a.astype(BF16)
    lo = (a - hi.astype(F32)).astype(BF16)
    return hi, lo


PREP_COLS = 256
PREP_ROWS = 1024


def _deint_kernel(w_ref, p_ref, o_ref):
    n = w_ref.shape[1]
    half = PREP_COLS // 2
    for b in range(n // PREP_COLS):
        t = jnp.dot(w_ref[:, b * PREP_COLS:(b + 1) * PREP_COLS].astype(BF16), p_ref[...],
                    preferred_element_type=F32).astype(BF16)
        o_ref[:, b * half:(b + 1) * half] = t[:, :half]
        o_ref[:, n // 2 + b * half:n // 2 + (b + 1) * half] = t[:, half:]


def _deinterleave_perm():
    perm = np.zeros((PREP_COLS, PREP_COLS), np.float32)
    j = np.arange(PREP_COLS // 2)
    perm[2 * j, j] = 1.0
    perm[2 * j + 1, PREP_COLS // 2 + j] = 1.0
    return jnp.asarray(perm, BF16)


def _ada_kernel(c_ref, w_ref, b_ref, o_ref):
    c = c_ref[...]
    s = c * jax.nn.sigmoid(c)
    hi, lo = _split_bf16(s)
    lhs = jnp.concatenate([hi, lo], axis=0)
    r = jnp.dot(lhs, w_ref[0].astype(BF16), preferred_element_type=F32)
    o_ref[0] = r[:8] + r[8:] + b_ref[0]


def _ada_all(cvec, w_ada, b_ada):
    n_l, d, n6 = w_ada.shape
    tn = 1024
    return pl.pallas_call(
        _ada_kernel,
        out_shape=jax.ShapeDtypeStruct((n_l, 8, n6), F32),
        grid=(n_l, n6 // tn),
        in_specs=[
            pl.BlockSpec((8, d), lambda l, j: (0, 0)),
            pl.BlockSpec((1, d, tn), lambda l, j: (l, 0, j)),
            pl.BlockSpec((1, 1, tn), lambda l, j: (l, 0, j)),
        ],
        out_specs=pl.BlockSpec((1, 8, tn), lambda l, j: (l, 0, j)),
        compiler_params=_cparams(("parallel", "parallel")),
        name="ada_mod",
    )(cvec, w_ada, b_ada.reshape(n_l, 1, n6))


def _rope(x, cos, sin_signed):
    n = x.shape[1]
    lane = lax.broadcasted_iota(I32, x.shape, 1)
    first = (lane % 32) < 16
    partner = jnp.where(first, pltpu.roll(x, n - 16, 1), pltpu.roll(x, 16, 1))
    return x * cos + partner * sin_signed


def _head_mean_sq(y, bd):
    hi, lo = _split_bf16(y * y)
    return (jnp.dot(hi, bd, preferred_element_type=F32)
            + jnp.dot(lo, bd, preferred_element_type=F32))


def _in_kernel(x_ref, *refs):
    _in_body(x_ref[...], *refs)


def _in_body(x, mod_ref, g_ref, w_ref, cos_ref, sin_ref, qg_ref, kg_ref, bd_ref,
             pa_ref, q_ref, k_ref, v_ref, ps_ref, nq_ref, nk_ref, nv_ref):
    ms = jnp.mean(x * x, axis=-1, keepdims=True)
    y = x * lax.rsqrt(ms + EPS) * g_ref[...]
    shift = mod_ref[0, 0:1, :]
    scale = mod_ref[0, 1:2, :]
    h = (y * (1 + scale) + shift).astype(BF16)

    def proj(lo, hi):
        return jnp.dot(h, w_ref[:, lo:hi], preferred_element_type=F32)

    pa_ref[...] = proj(C_AV, C_GQ).astype(BF16)

    cos = cos_ref[...]
    sin = sin_ref[...]
    gq = proj(C_GQ, C_GK)
    qn = gq * lax.rsqrt(_head_mean_sq(gq, bd_ref[...]) + EPS) * qg_ref[...]
    cos4 = jnp.concatenate([cos] * 4, axis=1)
    sin4 = jnp.concatenate([sin] * 4, axis=1)
    q_ref[...] = (_rope(qn, cos4, sin4) * Q_SCALE).astype(BF16)

    gk = proj(C_GK, C_GV)
    kn = gk * lax.rsqrt(_head_mean_sq(gk, bd_ref[0:GQA_KV_W, 0:GQA_KV_W]) + EPS) * kg_ref[...]
    kr = _rope(kn, cos, sin).astype(BF16)
    hd = HEAD_DIM
    k_ref[...] = jnp.concatenate([kr[:, :hd], kr[:, :hd], kr[:, hd:], kr[:, hd:]], axis=1)
    vv = proj(C_GV, C_SH).astype(BF16)
    ones = jnp.ones((vv.shape[0], hd), BF16)
    v_ref[...] = jnp.concatenate([vv[:, :hd], ones, vv[:, hd:], ones], axis=1)
    ps_ref[...] = proj(C_SH, C_NQ).astype(BF16)
    nq_ref[...] = (proj(C_NQ, C_NK) * Q_SCALE).astype(BF16)
    nk_ref[...] = proj(C_NK, C_NV).astype(BF16)
    nv_ref[...] = proj(C_NV, IN_COLS).astype(BF16)


def _in_proj(xs, mod, g1, w_in, layer, cos_t, sin_t, qg, kg, bd, seg_of_tile):
    nt, d = xs.shape
    tm = TM_PROJ
    row = lambda i: (i, 0)
    const = lambda i: (0, 0)
    widths = (2 * CONV_W, GQA_Q_W, 2 * GQA_KV_W, 2 * GQA_KV_W, 3 * SC_W, NA_W, NA_W, NA_W)
    return pl.pallas_call(
        _in_kernel,
        out_shape=[jax.ShapeDtypeStruct((nt, w), BF16) for w in widths],
        grid=(nt // tm,),
        in_specs=[
            pl.BlockSpec((tm, d), row),
            pl.BlockSpec((1, N_MOD, d), lambda i: (seg_of_tile(i), 0, 0)),
            pl.BlockSpec((1, d), const),
            _resident((None, d, IN_COLS), lambda i: (layer, 0, 0)),
            pl.BlockSpec((tm, LANE), row),
            pl.BlockSpec((tm, LANE), row),
            pl.BlockSpec((1, GQA_Q_W), const),
            pl.BlockSpec((1, GQA_KV_W), const),
            pl.BlockSpec((GQA_Q_W, GQA_Q_W), const),
        ],
        out_specs=[pl.BlockSpec((tm, w), row) for w in widths],
        compiler_params=_cparams(("parallel",)),
        name="in_proj",
    )(xs, mod, g1, w_in, cos_t, sin_t, qg, kg, bd)


def _conv_kernel(seq, n_x_rows, ctx_len,
                 pa_c, pa_p, pa_n, ps_c, ps_p, ps_n, dww_ref, dwb_ref, lng_ref, lnb_ref, scw_ref, wgu_ref, perm_ref,
                 oa_ref, oc_ref, wgu_o_ref, ext_ref, ext2_ref, acc_ref, sh_ref):
    _deint_kernel(wgu_ref, perm_ref, wgu_o_ref)
    tc = TM_CONV
    row0 = pl.program_id(0) * tc
    in_x = row0 < n_x_rows
    rel = jnp.where(in_x, row0, row0 - n_x_rows)
    seg = jnp.where(in_x, seq, ctx_len)
    keep_p = jnp.where(rel % seg == 0, 0.0, 1.0).astype(F32)
    keep_n = jnp.where((rel + tc) % seg == 0, 0.0, 1.0).astype(F32)

    def glu(ref):
        a = ref[...].astype(F32)
        return a[:, :CONV_W] * jax.nn.sigmoid(a[:, CONV_W:])

    ext_ref[0:HALO, :] = glu(pa_p) * keep_p
    ext_ref[HALO:HALO + tc, :] = glu(pa_c)
    ext_ref[HALO + tc:, :] = glu(pa_n) * keep_n

    pad = HALO - CONV_K // 2
    n_sh = sh_ref.shape[1]
    for b in range(8):
        sh_ref[b] = ext_ref[b:b + n_sh, :]
    rc = 64
    for c in range(CONV_W // LANE):
        cs = slice(c * LANE, (c + 1) * LANE)
        for r in range(tc // rc):
            acc = jnp.zeros((rc, LANE), F32)
            for k in range(CONV_K):
                a, b = divmod(k + pad, 8)
                lo = r * rc + 8 * a
                acc = acc + dww_ref[k:k + 1, cs] * sh_ref[b, lo:lo + rc, cs]
            acc_ref[r * rc:(r + 1) * rc, cs] = acc + dwb_ref[:, cs]

    u = acc_ref[...]
    mu = jnp.mean(u, axis=-1, keepdims=True)
    var = jnp.mean(jnp.square(u - mu), axis=-1, keepdims=True)
    yn = (u - mu) * lax.rsqrt(var + EPS) * lng_ref[...] + lnb_ref[...]
    oa_ref[...] = (yn * jax.nn.sigmoid(yn)).astype(BF16)

    def gated(ref, lo, hi):
        a = ref[lo:hi, :].astype(F32)
        return a[:, 2 * SC_W:] * a[:, :SC_W]

    ext2_ref[0:8, :] = gated(ps_p, HALO - 8, HALO) * keep_p
    ext2_ref[8:8 + tc, :] = gated(ps_c, 0, tc)
    ext2_ref[8 + tc:, :] = gated(ps_n, 0, 8) * keep_n
    conv = (scw_ref[0:1, :] * ext2_ref[7:7 + tc, :]
            + scw_ref[1:2, :] * ext2_ref[8:8 + tc, :]
            + scw_ref[2:3, :] * ext2_ref[9:9 + tc, :])
    oc_ref[...] = (ps_c[:, SC_W:2 * SC_W].astype(F32) * conv).astype(BF16)


def _conv_mix(pa, ps, dww, dwb, lng, lnb, scw, w_gate_up, layer, seq, n_x_rows, ctx_len):
    nt = pa.shape[0]
    tc = TM_CONV
    hb = tc // HALO
    n_hb = nt // HALO
    n_steps = nt // tc
    row = lambda i: (i, 0)
    prev = lambda i: (jnp.maximum(i * hb - 1, 0), 0)
    nxt = lambda i: (jnp.minimum((i + 1) * hb, n_hb - 1), 0)
    const = lambda i: (0, 0)
    wa, ws = pa.shape[1], ps.shape[1]
    n_l, n_e, d, n_gu = w_gate_up.shape
    rows_l = n_e * d
    w_tiles = min(rows_l // PREP_ROWS, 1 << (n_steps.bit_length() - 1))
    w_rows = rows_l // w_tiles
    assert rows_l % w_tiles == 0 and w_rows % 8 == 0
    wtile = lambda i: (layer * w_tiles + jnp.minimum(i, w_tiles - 1), 0)
    wtile_o = lambda i: (jnp.minimum(i, w_tiles - 1), 0)
    oa, oc, wgu_b = pl.pallas_call(
        functools.partial(_conv_kernel, seq, n_x_rows, ctx_len),
        out_shape=[jax.ShapeDtypeStruct((nt, CONV_W), BF16), jax.ShapeDtypeStruct((nt, SC_W), BF16),
                   jax.ShapeDtypeStruct((rows_l, n_gu), BF16)],
        grid=(n_steps,),
        in_specs=[
            pl.BlockSpec((tc, wa), row), pl.BlockSpec((HALO, wa), prev), pl.BlockSpec((HALO, wa), nxt),
            pl.BlockSpec((tc, ws), row), pl.BlockSpec((HALO, ws), prev), pl.BlockSpec((HALO, ws), nxt),
            pl.BlockSpec((CONV_K, CONV_W), const), pl.BlockSpec((1, CONV_W), const),
            pl.BlockSpec((1, CONV_W), const), pl.BlockSpec((1, CONV_W), const),
            pl.BlockSpec((SC_K, SC_W), const),
            pl.BlockSpec((w_rows, n_gu), wtile), pl.BlockSpec((PREP_COLS, PREP_COLS), const),
        ],
        out_specs=[pl.BlockSpec((tc, CONV_W), row), pl.BlockSpec((tc, SC_W), row),
                   pl.BlockSpec((w_rows, n_gu), wtile_o)],
        scratch_shapes=[pltpu.VMEM((tc + 2 * HALO, CONV_W), F32),
                        pltpu.VMEM((tc + 16, SC_W), F32),
                        pltpu.VMEM((tc, CONV_W), F32),
                        pltpu.VMEM((8, tc + 2 * HALO - 8, CONV_W), F32)],
        compiler_params=_cparams(("arbitrary",)),
        name="conv_mix",
    )(pa, pa, pa, ps, ps, ps, dww, dwb, lng, lnb, scw, w_gate_up.reshape(n_l * rows_l, n_gu), _deinterleave_perm())
    return oa, oc, wgu_b.reshape(1, n_e, d, n_gu)


def _qk(q, k):
    return lax.dot_general(q, k, (((1,), (1,)), ((), ())), preferred_element_type=F32)


def _softmax_pv(s, v):
    m = jnp.max(s, axis=-1, keepdims=True)
    p = jnp.exp2(s - m)
    l = jnp.sum(p, axis=-1, keepdims=True)
    return jnp.dot(p.astype(BF16), v, preferred_element_type=F32) / l


def _half_masks(rows):
    lane = lax.broadcasted_iota(I32, (rows, LANE), 1)
    lo = lane < HEAD_DIM
    return lo, jnp.logical_not(lo)


def _ctx_attn_kernel(kv_rep, kv_step, q_ref, k_ref, v_ref, o_in_ref, o_ref):
    del o_in_ref
    n_h = q_ref.shape[1] // HEAD_DIM
    outs = []
    for h in range(n_h):
        off = (h // kv_rep) * kv_step
        q = q_ref[:, h * HEAD_DIM:(h + 1) * HEAD_DIM]
        k = k_ref[:, off:off + HEAD_DIM]
        v = v_ref[:, off:off + HEAD_DIM]
        outs.append(_softmax_pv(_qk(q, k), v))
    o_ref[...] = jnp.concatenate(outs, axis=1).astype(BF16)


def _ctx_attn(q, k, v, o_all, n_b, ctx_len, blk0, kv_rep, kv_step):
    kw = k.shape[1]
    qw = q.shape[1]
    rows = lambda b: (blk0 + b, 0)
    return pl.pallas_call(
        functools.partial(_ctx_attn_kernel, kv_rep, kv_step),
        out_shape=jax.ShapeDtypeStruct(o_all.shape, o_all.dtype),
        grid=(n_b,),
        in_specs=[pl.BlockSpec((ctx_len, qw), rows), pl.BlockSpec((ctx_len, kw), rows),
                  pl.BlockSpec((ctx_len, kw), rows), pl.BlockSpec(memory_space=pl.ANY)],
        out_specs=pl.BlockSpec((ctx_len, qw), rows),
        input_output_aliases={3: 0},
        compiler_params=_cparams(("parallel",)),
        name="ctx_attn",
    )(q, k, v, o_all)


def _gqa_kernel(n_chunks, q_ref, kx_ref, vx_ref, kc_ref, vc_ref, o_ref):
    tq = q_ref.shape[0]
    grp = GQA_HEADS // GQA_KV_HEADS
    lo, hi = _half_masks(tq)

    def q_group(g):
        parts = []
        for j in range(grp):
            h = g * grp + j
            blk = q_ref[:, (h // 2) * LANE:(h // 2 + 1) * LANE].astype(F32)
            parts.append(jnp.where(lo if h % 2 == 0 else hi, blk, 0.0).astype(BF16))
        return jnp.concatenate(parts, axis=0)

    def update(q, k, v, m, acc):
        s = _qk(q, k)
        m_new = jnp.maximum(m, jnp.max(s, axis=-1, keepdims=True))
        p = jnp.exp2(s - m_new)
        acc = jnp.exp2(m - m_new) * acc + jnp.dot(p.astype(BF16), v, preferred_element_type=F32)
        return m_new, acc

    qs = [q_group(g) for g in range(GQA_KV_HEADS)]
    gsl = [slice(g * LANE, (g + 1) * LANE) for g in range(GQA_KV_HEADS)]
    state = []
    for g in range(GQA_KV_HEADS):
        s = _qk(qs[g], kc_ref[:, gsl[g]])
        m = jnp.max(s, axis=-1, keepdims=True)
        p = jnp.exp2(s - m)
        state += [m, jnp.dot(p.astype(BF16), vc_ref[:, gsl[g]], preferred_element_type=F32)]

    def body(c, carry):
        rows = pl.ds(pl.multiple_of(c * KC_GQA, KC_GQA), KC_GQA)
        out = []
        for g in range(GQA_KV_HEADS):
            out += update(qs[g], kx_ref[rows, gsl[g]], vx_ref[rows, gsl[g]], carry[2 * g], carry[2 * g + 1])
        return tuple(out)

    state = lax.fori_loop(0, n_chunks, body, tuple(state), unroll=16)

    lo4, _ = _half_masks(grp * tq)
    heads = []
    for g in range(GQA_KV_HEADS):
        acc = state[2 * g + 1]
        inv = 1.0 / jnp.where(lo4, 1.0, acc)
        o = acc * pltpu.roll(inv, HEAD_DIM, 1)
        heads.extend(o[j * tq:(j + 1) * tq] for j in range(grp))
    pairs = [jnp.where(lo, heads[2 * i], pltpu.roll(heads[2 * i + 1], HEAD_DIM, 1))
             for i in range(GQA_HEADS // 2)]
    o_ref[...] = jnp.concatenate(pairs, axis=1).astype(BF16)


def _gqa(q, k, v, n_b, seq, ctx_len):
    nt = q.shape[0]
    tq = TQ_GQA
    nq = seq // tq
    cblk0 = n_b * seq // ctx_len
    kw = k.shape[1]
    return pl.pallas_call(
        functools.partial(_gqa_kernel, seq // KC_GQA),
        out_shape=jax.ShapeDtypeStruct((nt, GQA_Q_W), BF16),
        grid=(n_b, nq),
        in_specs=[
            pl.BlockSpec((tq, GQA_Q_W), lambda b, j: (b * nq + j, 0)),
            pl.BlockSpec((seq, kw), lambda b, j: (b, 0)),
            pl.BlockSpec((seq, kw), lambda b, j: (b, 0)),
            pl.BlockSpec((ctx_len, kw), lambda b, j: (cblk0 + b, 0)),
            pl.BlockSpec((ctx_len, kw), lambda b, j: (cblk0 + b, 0)),
        ],
        out_specs=pl.BlockSpec((tq, GQA_Q_W), lambda b, j: (b * nq + j, 0)),
        compiler_params=_cparams(("parallel", "parallel")),
        name="gqa_flash",
    )(q, k, v, k, v)


def _na_kernel(n_rows, q_ref, k_ref, v_ref, kc_ref, vc_ref, t_ref, o_ref):
    rb = pl.program_id(2)
    win = WIN_ROWS * GRID_W
    tq = q_ref.shape[0]
    lo, hi = _half_masks(tq)
    qf = q_ref[...].astype(F32)
    kc = kc_ref[...]
    vc = vc_ref[...]
    halves = []
    for j in range(LANE // HEAD_DIM):
        qa = jnp.where(lo if j == 0 else hi, qf, 0.0).astype(BF16)
        s_ctx = _qk(qa, kc)
        m_ctx = jnp.max(s_ctx, axis=-1, keepdims=True)
        s_wins, ms, krows_l = [], [], []
        for i in range(NA_RB):
            r = rb * NA_RB + i
            r0 = jnp.clip(r - WIN_ROWS // 2, 0, n_rows - WIN_ROWS)
            shift = r0 - r + (WIN_ROWS - 1)
            krows = pl.ds(pl.multiple_of(r0 * GRID_W, GRID_W), win)
            rs = slice(i * GRID_W, (i + 1) * GRID_W)
            bias = jnp.concatenate([t_ref[j, 2 * wp + shift] for wp in range(WIN_ROWS // 2)], axis=1)
            s_win = _qk(qa[rs], k_ref[krows, :]) + bias
            s_wins.append(s_win)
            ms.append(jnp.maximum(jnp.max(s_win, axis=-1, keepdims=True), m_ctx[rs]))
            krows_l.append(krows)
        m_all = jnp.concatenate(ms, axis=0)
        p_ctx = jnp.exp2(s_ctx - m_all)
        l_ctx = jnp.sum(p_ctx, axis=-1, keepdims=True)
        o_ctx = jnp.dot(p_ctx.astype(BF16), vc, preferred_element_type=F32)
        outs = []
        for i in range(NA_RB):
            rs = slice(i * GRID_W, (i + 1) * GRID_W)
            p = jnp.exp2(s_wins[i] - ms[i])
            l = jnp.sum(p, axis=-1, keepdims=True) + l_ctx[rs]
            o = jnp.dot(p.astype(BF16), v_ref[krows_l[i], :], preferred_element_type=F32) + o_ctx[rs]
            outs.append(o / l)
        halves.append(jnp.concatenate(outs, axis=0))
    o_ref[...] = jnp.where(lo, halves[0], halves[1]).astype(BF16)


def _na(nq, nk, nv, table, n_b, seq, ctx_len):
    nt = nq.shape[0]
    n_rows = seq // GRID_W
    n_rb = n_rows // NA_RB
    tq = NA_RB * GRID_W
    cblk0 = n_b * seq // ctx_len
    hp = LANE // HEAD_DIM
    return pl.pallas_call(
        functools.partial(_na_kernel, n_rows),
        out_shape=jax.ShapeDtypeStruct((nt, NA_W), BF16),
        grid=(n_b, NA_W // LANE, n_rb),
        in_specs=[
            pl.BlockSpec((tq, LANE), lambda b, h, r: (b * n_rb + r, h)),
            pl.BlockSpec((seq, LANE), lambda b, h, r: (b, h)),
            pl.BlockSpec((seq, LANE), lambda b, h, r: (b, h)),
            pl.BlockSpec((ctx_len, LANE), lambda b, h, r: (cblk0 + b, h)),
            pl.BlockSpec((ctx_len, LANE), lambda b, h, r: (cblk0 + b, h)),
            pl.BlockSpec((hp, 2 * WIN_ROWS - 2, GRID_W, LANE), lambda b, h, r: (h, 0, 0, 0)),
        ],
        out_specs=pl.BlockSpec((tq, LANE), lambda b, h, r: (b * n_rb + r, h)),
        compiler_params=_cparams(("parallel", "parallel", "parallel")),
        name="na_attn",
    )(nq, nk, nv, nk, nv, table)


def _na_bias_table(rpb):
    cols = np.arange(GRID_W)
    c0 = np.clip(cols - WIN_COLS // 2, 0, GRID_W - WIN_COLS)
    kc = np.arange(GRID_W)
    rel = kc[None, :] - cols[:, None] + (WIN_COLS - 1)
    valid = (kc[None, :] >= c0[:, None]) & (kc[None, :] < c0[:, None] + WIN_COLS)
    rel = np.clip(rel, 0, 2 * WIN_COLS - 2)
    t = rpb[:, :, rel]
    t = jnp.where(jnp.asarray(valid)[None, None], t * LOG2E, MASK_VALUE).astype(F32)
    return jnp.concatenate([t[:, :-1], t[:, 1:]], axis=-1)


def _out_kernel(oa_ref, ob_ref, oc_ref, od_ref, x_ref, mod_ref, g_ref, w_ref, wr_cat_ref, wr_hi_ref, br_ref,
                x1_ref, h2_ref, e_ref, pos_ref, gate_ref, cnt_ref, carry_ref):
    acc = jnp.dot(oa_ref[...], w_ref[0:CONV_W, :], preferred_element_type=F32)
    acc = acc + jnp.dot(ob_ref[...], w_ref[CONV_W:CONV_W + GQA_Q_W, :], preferred_element_type=F32)
    acc = acc + jnp.dot(oc_ref[...], w_ref[CONV_W + GQA_Q_W:MIX_W - NA_W, :], preferred_element_type=F32)
    acc = acc + jnp.dot(od_ref[...], w_ref[MIX_W - NA_W:MIX_W, :], preferred_element_type=F32)
    x1 = x_ref[...] + mod_ref[0, 2:3, :] * acc
    x1_ref[...] = x1
    ms = jnp.mean(x1 * x1, axis=-1, keepdims=True)
    y = x1 * lax.rsqrt(ms + EPS) * g_ref[...]
    h2 = y * (1 + mod_ref[0, 4:5, :]) + mod_ref[0, 3:4, :]
    h2_ref[...] = h2
    hi, lo = _split_bf16(h2)
    n_e = br_ref.shape[1]
    a = jnp.dot(hi, wr_cat_ref[...], preferred_element_type=F32)
    b = jnp.dot(lo, wr_hi_ref[...], preferred_element_type=F32)
    logits = a[:, :n_e] + a[:, n_e:] + b + br_ref[...]
    _route_tile(logits, e_ref, pos_ref, gate_ref, cnt_ref, carry_ref)


def _out_proj(oa, ob, oc, od, xs, mod, g2, w_out, layer, wr_cat, wr_hi, br, seg_of_tile):
    nt, d = xs.shape
    tm = TM_PROJ
    n_e = wr_hi.shape[1]
    row = lambda i: (i, 0)
    const = lambda i: (0, 0)
    return pl.pallas_call(
        _out_kernel,
        out_shape=[jax.ShapeDtypeStruct((nt, d), F32), jax.ShapeDtypeStruct((nt, d), F32),
                   jax.ShapeDtypeStruct((nt, TOP_K), I32), jax.ShapeDtypeStruct((nt, TOP_K), I32),
                   jax.ShapeDtypeStruct((nt, TOP_K), F32), jax.ShapeDtypeStruct((1, n_e), I32)],
        grid=(nt // tm,),
        in_specs=[
            pl.BlockSpec((tm, CONV_W), row), pl.BlockSpec((tm, GQA_Q_W), row),
            pl.BlockSpec((tm, SC_W), row), pl.BlockSpec((tm, NA_W), row),
            pl.BlockSpec((tm, d), row),
            pl.BlockSpec((1, N_MOD, d), lambda i: (seg_of_tile(i), 0, 0)),
            pl.BlockSpec((1, d), const),
            _resident((None, MIX_W, d), lambda i: (layer, 0, 0)),
            pl.BlockSpec((d, 2 * n_e), const), pl.BlockSpec((d, n_e), const), pl.BlockSpec((1, n_e), const),
        ],
        out_specs=[pl.BlockSpec((tm, d), row), pl.BlockSpec((tm, d), row),
                   pl.BlockSpec((tm, TOP_K), row), pl.BlockSpec((tm, TOP_K), row), pl.BlockSpec((tm, TOP_K), row),
                   pl.BlockSpec((1, n_e), const)],
        scratch_shapes=[pltpu.VMEM((1, n_e), F32)],
        compiler_params=_cparams(("arbitrary",)),
        name="out_proj",
    )(oa, ob, oc, od, xs, mod, g2, w_out, wr_cat, wr_hi, br)


def _route_tile(lg, e_ref, pos_ref, gate_ref, cnt_ref, carry_ref):
    i = pl.program_id(0)

    @pl.when(i == 0)
    def _():
        carry_ref[...] = jnp.zeros_like(carry_ref)

    tr, n_e = lg.shape
    lane = lax.broadcasted_iota(I32, (tr, n_e), 1)
    work = lg
    vals, idxs = [], []
    for _ in range(TOP_K):
        m = jnp.max(work, axis=-1, keepdims=True)
        idx = jnp.min(jnp.where(work == m, lane, n_e), axis=-1, keepdims=True)
        vals.append(m)
        idxs.append(idx)
        work = jnp.where(lane == idx, -jnp.inf, work)
    exps = [jnp.exp(v - vals[0]) for v in vals]
    den = exps[0] + exps[1] + exps[2] + exps[3]

    mask = jnp.zeros((tr, n_e), F32)
    for idx in idxs:
        mask = mask + jnp.where(lane == idx, 1.0, 0.0)
    r_i = lax.broadcasted_iota(I32, (tr, tr), 0)
    c_i = lax.broadcasted_iota(I32, (tr, tr), 1)
    tri = jnp.where(c_i <= r_i, 1.0, 0.0).astype(BF16)
    incl = jnp.dot(tri, mask.astype(BF16), preferred_element_type=F32)
    before = carry_ref[...] + incl - mask

    lane_o = lax.broadcasted_iota(I32, (tr, LANE), 1)
    e_out = jnp.zeros((tr, LANE), I32)
    pos_out = jnp.zeros((tr, LANE), I32)
    gate_out = jnp.zeros((tr, LANE), F32)
    for k in range(TOP_K):
        pos_k = jnp.sum(jnp.where(lane == idxs[k], before, 0.0), axis=-1, keepdims=True)
        e_out = jnp.where(lane_o == k, idxs[k], e_out)
        pos_out = jnp.where(lane_o == k, pos_k.astype(I32), pos_out)
        gate_out = jnp.where(lane_o == k, exps[k] / den, gate_out)
    e_ref[...] = e_out[:, :TOP_K]
    pos_ref[...] = pos_out[:, :TOP_K]
    gate_ref[...] = gate_out[:, :TOP_K]
    total = carry_ref[...] + incl[tr - 1:tr, :]
    carry_ref[...] = total
    cnt_ref[...] = total.astype(I32)


def _row_copy(src_ref, src_row, dst_ref, dst_row, sem):
    return pltpu.make_async_copy(src_ref.at[pl.ds(src_row, 1)], dst_ref.at[pl.ds(dst_row, 1)], sem)


def _dispatch_kernel(dest_ref, pad_end_ref, h_ref, xs_ref, zbuf_ref, sem, zsem):
    td = h_ref.shape[0]
    blk = zbuf_ref.shape[0]
    n_e = pad_end_ref.shape[0]
    base = pl.program_id(0) * td * TOP_K

    @pl.when(pl.program_id(0) == 0)
    def _():
        zbuf_ref[...] = jnp.zeros_like(zbuf_ref)

        def fill(e):
            start = pl.multiple_of(pad_end_ref[e] - blk, blk)
            return pltpu.make_async_copy(zbuf_ref, xs_ref.at[pl.ds(start, blk)], zsem)

        def nonempty(e):
            prev = pad_end_ref[jnp.maximum(e - 1, 0)]
            return pad_end_ref[e] > jnp.where(e == 0, 0, prev)

        def start(e, _):
            @pl.when(nonempty(e))
            def _():
                fill(e).start()
            return 0

        def wait(e, _):
            @pl.when(nonempty(e))
            def _():
                fill(e).wait()
            return 0

        lax.fori_loop(0, n_e, start, 0)
        lax.fori_loop(0, n_e, wait, 0)

    def issue(t, _):
        for k in range(TOP_K):
            _row_copy(h_ref, t, xs_ref, dest_ref[base + t * TOP_K + k], sem).start(priority=k % 2)
        return 0

    lax.fori_loop(0, td, issue, 0, unroll=DMA_UNROLL)
    n_rows = td * TOP_K
    pltpu.make_async_copy(xs_ref.at[pl.ds(0, n_rows)], xs_ref.at[pl.ds(0, n_rows)], sem).wait()


def _dispatch(dest, pad_end, h2, n_slots):
    nt, d = h2.shape
    td = TD_MOE
    return pl.pallas_call(
        _dispatch_kernel,
        out_shape=jax.ShapeDtypeStruct((n_slots, d), F32),
        grid_spec=pltpu.PrefetchScalarGridSpec(
            num_scalar_prefetch=2,
            grid=(nt // td,),
            in_specs=[pl.BlockSpec((td, d), lambda i, dest, pe: (i, 0))],
            out_specs=pl.BlockSpec(memory_space=pl.ANY),
            scratch_shapes=[pltpu.VMEM((BLK_MOE, d), F32), pltpu.SemaphoreType.DMA, pltpu.SemaphoreType.DMA],
        ),
        compiler_params=_cparams(("arbitrary",)),
        name="moe_dispatch",
    )(dest, pad_end, h2)


def _ffn_kernel(blk_e_ref, n_act_ref, x_ref, wgu_ref, bgu_ref, wd_ref, bd_ref, y_ref, wd_bf_ref):
    b = pl.program_id(0)

    @pl.when(b < n_act_ref[0])
    def _():
        @pl.when(jnp.logical_or(b == 0, blk_e_ref[b] != blk_e_ref[jnp.maximum(b - 1, 0)]))
        def _():
            wd_bf_ref[...] = wd_ref[0].astype(BF16)

        d_e = wd_ref.shape[1]
        xb = x_ref[...].astype(BF16)
        h = jnp.dot(xb, wgu_ref[0], preferred_element_type=F32) + bgu_ref[0]
        glu = jnp.minimum(h[:, :d_e], SWIGLU_LIMIT)
        lin = jnp.clip(h[:, d_e:], -SWIGLU_LIMIT, SWIGLU_LIMIT)
        act = glu * jax.nn.sigmoid(SWIGLU_ALPHA * glu) * (lin + 1)
        y_ref[...] = jnp.dot(act.astype(BF16), wd_bf_ref[...], preferred_element_type=F32) + bd_ref[0]


def _ffn(blk_e, n_act, xs, wgu, bgu, wd, bd, layer):
    ns, d = xs.shape
    blk = BLK_MOE
    d_e2 = wgu.shape[-1]
    d_e = d_e2 // 2

    def rows(b, be, na):
        return (jnp.minimum(b, na[0] - 1), 0)

    def expert4(b, be, na):
        return (layer, be[jnp.minimum(b, na[0] - 1)], 0, 0)

    def expert4_gu(b, be, na):
        return (0, be[jnp.minimum(b, na[0] - 1)], 0, 0)

    return pl.pallas_call(
        _ffn_kernel,
        out_shape=jax.ShapeDtypeStruct((ns, d), F32),
        grid_spec=pltpu.PrefetchScalarGridSpec(
            num_scalar_prefetch=2,
            grid=(ns // blk,),
            in_specs=[
                pl.BlockSpec((blk, d), rows),
                pl.BlockSpec((None, 1, d, d_e2), expert4_gu), pl.BlockSpec((None, 1, 1, d_e2), expert4),
                pl.BlockSpec((None, 1, d_e, d), expert4), pl.BlockSpec((None, 1, 1, d), expert4),
            ],
            out_specs=pl.BlockSpec((blk, d), rows),
            scratch_shapes=[pltpu.VMEM((d_e, d), BF16)],
        ),
        compiler_params=_cparams(("arbitrary",)),
        name="moe_ffn",
    )(blk_e, n_act, xs, wgu, bgu, wd, bd)


def _combine_kernel(final, dest_ref, ys_ref, gate_ref, x1_ref, mod_ref, fg_ref, o_ref, buf_ref, sems):
    td = x1_ref.shape[0]
    i = pl.program_id(0)
    slot = i % 2

    def gather(tile, dst_slot):
        base = tile * td * TOP_K

        def issue(t, _):
            for k in range(TOP_K):
                _row_copy(ys_ref, dest_ref[base + t * TOP_K + k], buf_ref.at[dst_slot, k], t,
                          sems.at[dst_slot]).start(priority=k % 2)
            return 0

        lax.fori_loop(0, td, issue, 0, unroll=DMA_UNROLL)

    @pl.when(i == 0)
    def _():
        gather(0, 0)

    @pl.when(i + 1 < pl.num_programs(0))
    def _():
        gather(i + 1, 1 - slot)

    pltpu.make_async_copy(buf_ref.at[slot], buf_ref.at[slot], sems.at[slot]).wait()

    g = gate_ref[...]
    y = g[:, 0:1] * buf_ref[slot, 0]
    for k in range(1, TOP_K):
        y = y + g[:, k:k + 1] * buf_ref[slot, k]
    x = x1_ref[...] + mod_ref[0, 5:6, :] * y
    if final:
        ms = jnp.mean(x * x, axis=-1, keepdims=True)
        x = x * lax.rsqrt(ms + EPS) * fg_ref[...]
    o_ref[...] = x


def _combine(dest, ys, gate, x1, mod, seg_of_tile, final_g, n_rows):
    d = x1.shape[1]
    td = TD_MOE
    ratio = TM_PROJ // td
    final = final_g is not None
    fg = final_g if final else jnp.ones((1, d), F32)
    return pl.pallas_call(
        functools.partial(_combine_kernel, final),
        out_shape=jax.ShapeDtypeStruct((n_rows, d), F32),
        grid_spec=pltpu.PrefetchScalarGridSpec(
            num_scalar_prefetch=1,
            grid=(n_rows // td,),
            in_specs=[
                pl.BlockSpec(memory_space=pl.ANY),
                pl.BlockSpec((td, TOP_K), lambda i, dest: (i, 0)),
                pl.BlockSpec((td, d), lambda i, dest: (i, 0)),
                pl.BlockSpec((1, N_MOD, d), lambda i, dest: (seg_of_tile(i // ratio), 0, 0)),
                pl.BlockSpec((1, d), lambda i, dest: (0, 0)),
            ],
            out_specs=pl.BlockSpec((td, d), lambda i, dest: (i, 0)),
            scratch_shapes=[pltpu.VMEM((2, TOP_K, td, d), F32), pltpu.SemaphoreType.DMA((2,))],
        ),
        compiler_params=_cparams(("arbitrary",)),
        name="moe_combine",
    )(dest, ys, gate, x1, mod, fg)


def _rope_tables(n_b, seq, ctx_len):
    t = np.arange(seq)
    axis_dim = HEAD_DIM // 2
    inv = (ROPE_THETA ** (-np.arange(0, axis_dim, 2, dtype=np.float32) / axis_dim)).astype(np.float32)
    pos = np.stack([t // GRID_W, t % GRID_W], axis=-1).astype(np.float32)
    ang = jnp.asarray(pos[:, :, None] * inv[None, None, :])
    cos, sin = jnp.cos(ang), jnp.sin(ang)
    cos_h = jnp.concatenate([cos, cos], axis=-1).reshape(seq, HEAD_DIM)
    sin_h = jnp.concatenate([-sin, sin], axis=-1).reshape(seq, HEAD_DIM)
    cos_x = jnp.tile(cos_h, (n_b, LANE // HEAD_DIM))
    sin_x = jnp.tile(sin_h, (n_b, LANE // HEAD_DIM))
    ones = jnp.ones((n_b * ctx_len, LANE), F32)
    return (jnp.concatenate([cos_x, ones], axis=0),
            jnp.concatenate([sin_x, jnp.zeros_like(ones)], axis=0))


def kernel(x, c, ctx, c_ctx, w_ada, b_ada, norm1_g, norm2_g, w_in, w_out, conv_dw_w, conv_dw_b, conv_ln_g,
           conv_ln_b, q_norm_g, k_norm_g, sc_conv_w, na_rpb, w_router, b_router, w_gate_up, b_gate_up, w_down,
           b_down, final_norm_g):
    n_b, seq, d = x.shape
    ctx_len = ctx.shape[1]
    depth = w_ada.shape[0]
    n_e = w_router.shape[2]
    d_e = w_down.shape[2]
    n_x_rows = n_b * seq
    nt = n_x_rows + n_b * ctx_len
    assert d == D_MODEL and seq % GRID_W == 0 and (seq // GRID_W) % NA_RB == 0
    assert seq % TM_PROJ == 0 and (n_b * ctx_len) % TM_PROJ == 0
    assert seq % TM_CONV == 0 and ctx_len % TM_CONV == 0 and seq % ctx_len == 0
    assert seq % KC_GQA == 0 and nt % TD_MOE == 0 and n_b + 1 <= 8

    tiles_x = n_x_rows // TM_PROJ
    tiles_per_b = seq // TM_PROJ

    def seg_of_tile(i):
        return jnp.where(i < tiles_x, 1 + i // tiles_per_b, 0)

    w_in_b = w_in.astype(BF16)
    w_out_b = w_out.astype(BF16)
    bgu_p = jnp.concatenate([b_gate_up[..., 0::2], b_gate_up[..., 1::2]], axis=-1)[:, :, None, :]
    bd_p = b_down[:, :, None, :]
    wr_hi = w_router.astype(BF16)
    wr_lo = (w_router - wr_hi.astype(F32)).astype(BF16)
    wr_cat = jnp.concatenate([wr_hi, wr_lo], axis=-1)
    bd = jnp.asarray(np.kron(np.eye(GQA_HEADS, dtype=np.float32),
                             np.full((HEAD_DIM, HEAD_DIM), 1.0 / HEAD_DIM, np.float32))).astype(BF16)
    cos_t, sin_t = _rope_tables(n_b, seq, ctx_len)

    cvec = jnp.zeros((8, d), F32).at[0].set(c_ctx).at[1:1 + n_b].set(c)
    mods = _ada_all(cvec, w_ada, b_ada).reshape(depth, 8, N_MOD, d)

    n_assign = nt * TOP_K
    n_blk = -(-n_assign // BLK_MOE) + n_e

    xs = jnp.concatenate([x.reshape(n_x_rows, d), ctx.reshape(n_b * ctx_len, d)], axis=0)
    cblk0 = n_x_rows // ctx_len

    pending = None
    for l in range(depth):
        mod = mods[l]
        in_args = (mod, norm1_g[l][None], w_in_b, l, cos_t, sin_t,
                   jnp.tile(q_norm_g[l], GQA_HEADS)[None], jnp.tile(k_norm_g[l], GQA_KV_HEADS)[None], bd,
                   seg_of_tile)
        if pending is not None:
            xs = _combine(*pending, seg_of_tile, None, nt)
        pa, q, k, v, ps, nq, nk, nv = _in_proj(xs, *in_args)
        oa, oc, wgu_b = _conv_mix(pa, ps, conv_dw_w[l], conv_dw_b[l][None], conv_ln_g[l][None],
                                  conv_ln_b[l][None], sc_conv_w[l], w_gate_up, l, seq, n_x_rows, ctx_len)
        ob = _gqa(q, k, v, n_b, seq, ctx_len)
        ob = _ctx_attn(q, k, v, ob, n_b, ctx_len, cblk0, GQA_HEADS // GQA_KV_HEADS, LANE)
        od = _na(nq, nk, nv, _na_bias_table(na_rpb[l]), n_b, seq, ctx_len)
        od = _ctx_attn(nq, nk, nv, od, n_b, ctx_len, cblk0, 1, HEAD_DIM)
        x1, h2, e_o, pos_o, gate_o, cnt = _out_proj(oa, ob, oc, od, xs, mod, norm2_g[l][None], w_out_b, l,
                                                    wr_cat[l], wr_hi[l], b_router[l][None], seg_of_tile)
        counts = cnt[0]
        padded = (counts + BLK_MOE - 1) // BLK_MOE * BLK_MOE
        pad_end = jnp.cumsum(padded)
        pad_start = pad_end - padded
        dest = (pad_start[e_o] + pos_o).reshape(-1).astype(I32)
        blk_lo = jnp.arange(n_blk, dtype=I32) * BLK_MOE
        blk_e = jnp.minimum(jnp.sum((pad_end[None, :] <= blk_lo[:, None]).astype(I32), axis=1), n_e - 1)
        n_act = (pad_end[-1:] // BLK_MOE).astype(I32)

        x_sorted = _dispatch(dest, pad_end.astype(I32), h2, n_blk * BLK_MOE)
        ys = _ffn(blk_e, n_act, x_sorted, wgu_b, bgu_p, w_down, bd_p, l)
        pending = (dest, ys, gate_o, x1, mod)

    out = _combine(*pending, seg_of_tile, final_norm_g[None], n_x_rows)
    return out.reshape(n_b, seq, d)
```
